```python
import math
import jax, jax.numpy as jnp
from jax import lax
import numpy as np

D_MODEL = 1024
BATCH = 8
SEQ = 2048
DEPTH = 2
DEC_BATCH = 128
DEC_SEQ = 8
PAST_LEN = 16384
PAGE_SIZE = 128

D_MIX = D_MODEL
D_CONV = D_MIX // 4
CONV_W = 3
GDN_HEADS = 4
GDN_DK = D_MIX // 8
GDN_DV = D_MIX // 8
D_GDN = GDN_HEADS * GDN_DV
D_GDN_QKV = 2 * GDN_HEADS * GDN_DK + D_GDN
GDN_CONV_W = 4
ML_HEADS = 4
ML_DH = D_MIX // 16
D_ML = ML_HEADS * ML_DH
D_FF = (11 * D_MODEL) // 4
FFN_CONV_W = 3
CHUNK = 64
ALPHA = (2.0 * DEPTH) ** 0.25
BETA = (8.0 * DEPTH) ** -0.25
LN_EPS = 1e-5
NORM_EPS = 1e-6
NEG = -1e30
IN_SIZES = (D_CONV, D_CONV, D_CONV,
            GDN_HEADS * GDN_DK, GDN_HEADS * GDN_DK, D_GDN, D_GDN, GDN_HEADS, GDN_HEADS,
            D_ML, D_ML, D_ML, D_ML, ML_HEADS, ML_HEADS)
D_IN = sum(IN_SIZES)

kernel_name = 'hybrid_conv_gdn_mlstm_deepnorm_step'


def layer_norm(x, g, b):
    xf = x.astype(jnp.float32)
    mu = jnp.mean(xf, -1, keepdims=True)
    var = jnp.mean(jnp.square(xf - mu), -1, keepdims=True)
    return ((xf - mu) * lax.rsqrt(var + LN_EPS) * g.astype(jnp.float32) + b.astype(jnp.float32)).astype(x.dtype)


def l2norm(x):
    return x * lax.rsqrt(jnp.sum(x * x, -1, keepdims=True) + NORM_EPS)


def causal_dwconv(x, buf, w):
    width = w.shape[0]
    T = x.shape[1]
    xp = jnp.concatenate([buf.astype(x.dtype), x], axis=1)
    y = xp[:, 0:T] * w[0]
    for j in range(1, width):
        y = y + xp[:, j:j + T] * w[j]
    return y, xp[:, -(width - 1):]


def pad_time(a, pad, value=0.0):
    return jnp.pad(a, [(0, 0), (0, pad)] + [(0, 0)] * (a.ndim - 2), constant_values=value)


def to_chunks(a, L):
    Bn, T = a.shape[:2]
    a = a.reshape((Bn, T // L, L) + a.shape[2:])
    return jnp.moveaxis(jnp.moveaxis(a, 3, 2), 1, 0)


def from_chunks(o, T):
    n, Bn, H, L, D = o.shape
    return jnp.transpose(o, (1, 0, 3, 2, 4)).reshape(Bn, n * L, H, D)[:, :T]


def gated_delta_chunked(q, k, v, g, beta, s0):
    T = q.shape[1]
    DV = v.shape[-1]
    L = min(CHUNK, T)
    n = -(-T // L)
    pad = n * L - T
    qc, kc, vc, gc, bc = [to_chunks(pad_time(a, pad), L) for a in (q, k, v, g, beta)]
    G = jnp.cumsum(gc, axis=-1)
    causal = jnp.tril(jnp.ones((L, L), bool))
    strict = jnp.tril(jnp.ones((L, L), bool), -1)
    decay = jnp.exp(jnp.where(causal, G[..., :, None] - G[..., None, :], -jnp.inf))
    kb = kc * bc[..., None]
    kk = jnp.einsum('nbhik,nbhjk->nbhij', kb, kc) * decay
    M = jnp.eye(L, dtype=kk.dtype) + jnp.where(strict, kk, 0.0)
    rhs = jnp.concatenate([vc * bc[..., None], kb * jnp.exp(G)[..., None]], axis=-1)
    sol = lax.linalg.triangular_solve(M, rhs, left_side=True, lower=True, unit_diagonal=True)
    w_v, w_k = sol[..., :DV], sol[..., DV:]
    qk = jnp.einsum('nbhik,nbhjk->nbhij', qc, kc) * decay

    def step(S, inp):
        wv_i, wk_i, q_i, k_i, qk_i, G_i = inp
        u = wv_i - jnp.einsum('bhik,bhkv->bhiv', wk_i, S)
        o = (jnp.einsum('bhik,bhkv->bhiv', q_i * jnp.exp(G_i)[..., None], S)
             + jnp.einsum('bhij,bhjv->bhiv', qk_i, u))
        g_last = G_i[..., -1]
        S = (S * jnp.exp(g_last)[..., None, None]
             + jnp.einsum('bhik,bhiv->bhkv', k_i * jnp.exp(g_last[..., None] - G_i)[..., None], u))
        return S, o

    S, o = lax.scan(step, s0, (w_v, w_k, qc, kc, qk, G))
    return from_chunks(o, T), S


def mlstm_chunked(q, k, v, ig, lf, c0, n0, m0):
    T = q.shape[1]
    L = min(CHUNK, T)
    n = -(-T // L)
    pad = n * L - T
    qc, kc, vc, fc = [to_chunks(pad_time(a, pad), L) for a in (q, k, v, lf)]
    ic = to_chunks(pad_time(ig, pad, NEG), L)
    F = jnp.cumsum(fc, axis=-1)
    causal = jnp.tril(jnp.ones((L, L), bool))

    def step(carry, inp):
        C, nv, m = carry
        q_i, k_i, v_i, ig_i, F_i = inp
        D = jnp.where(causal, F_i[..., :, None] - F_i[..., None, :] + ig_i[..., None, :], -jnp.inf)
        inter = F_i + m[..., None]
        m_i = jnp.maximum(inter, jnp.max(D, -1))
        wD = jnp.exp(D - m_i[..., None])
        wI = jnp.exp(inter - m_i)
        s = jnp.einsum('bhid,bhjd->bhij', q_i, k_i) * wD
        num = wI[..., None] * jnp.einsum('bhid,bhde->bhie', q_i, C) + jnp.einsum('bhij,bhje->bhie', s, v_i)
        qn = wI * jnp.einsum('bhid,bhd->bhi', q_i, nv) + jnp.sum(s, -1)
        h = num / jnp.maximum(jnp.abs(qn), jnp.exp(-m_i))[..., None]
        m_new = m_i[..., -1]
        wk = jnp.exp(F_i[..., -1:] - F_i + ig_i - m_new[..., None])
        dc = jnp.exp(F_i[..., -1] + m - m_new)
        C = dc[..., None, None] * C + jnp.einsum('bhjd,bhje->bhde', k_i * wk[..., None], v_i)
        nv = dc[..., None] * nv + jnp.einsum('bhjd,bhj->bhd', k_i, wk)
        return (C, nv, m_new), h

    (C, nv, m), h = lax.scan(step, (c0, n0, m0), (qc, kc, vc, ic, F))
    return from_chunks(h, T), C, nv, m


def layer_forward(x, conv_buf, gdn_buf, gdn_s, ml_c, ml_n, ml_m, ffn_buf,
                  w_in, conv_w, gdn_conv_w, gdn_a_log, gdn_dt_bias, gdn_norm_w,
                  ml_i_bias, ml_f_bias, ml_norm_w, w_o, ln1_g, ln1_b,
                  w_up, ffn_conv_w, w_down, ln2_g, ln2_b):
    dt = x.dtype
    f32 = jnp.float32
    Bn, T, _ = x.shape
    proj = jnp.einsum('btd,de->bte', x, w_in)
    idx = np.cumsum(IN_SIZES)[:-1].tolist()
    (a_b, a_c, a_h, g_q, g_k, g_v, g_z, g_a, g_b,
     m_q, m_k, m_v, m_o, m_i, m_f) = jnp.split(proj, idx, axis=-1)
    a_conv, conv_buf_new = causal_dwconv(a_c * a_h, conv_buf, conv_w)
    y_a = (a_b * a_conv).astype(dt)
    qkv, gdn_buf_new = causal_dwconv(jnp.concatenate([g_q, g_k, g_v], -1), gdn_buf, gdn_conv_w)
    qkv = jax.nn.silu(qkv.astype(f32))
    q, k, v = jnp.split(qkv, [GDN_HEADS * GDN_DK, 2 * GDN_HEADS * GDN_DK], axis=-1)
    q = l2norm(q.reshape(Bn, T, GDN_HEADS, GDN_DK)) * (GDN_DK ** -0.5)
    k = l2norm(k.reshape(Bn, T, GDN_HEADS, GDN_DK))
    v = v.reshape(Bn, T, GDN_HEADS, GDN_DV)
    beta = jax.nn.sigmoid(g_b.astype(f32))
    g = -jnp.exp(gdn_a_log.astype(f32)) * jax.nn.softplus(g_a.astype(f32) + gdn_dt_bias.astype(f32))
    o, gdn_s_new = gated_delta_chunked(q, k, v, g, beta, gdn_s.astype(f32))
    o = o * lax.rsqrt(jnp.mean(o * o, -1, keepdims=True) + NORM_EPS) * gdn_norm_w.astype(f32)
    o = o * jax.nn.silu(g_z.astype(f32).reshape(Bn, T, GDN_HEADS, GDN_DV))
    y_b = o.reshape(Bn, T, D_GDN).astype(dt)
    mq = m_q.astype(f32).reshape(Bn, T, ML_HEADS, ML_DH)
    mk = m_k.astype(f32).reshape(Bn, T, ML_HEADS, ML_DH) * (ML_DH ** -0.5)
    mv = m_v.astype(f32).reshape(Bn, T, ML_HEADS, ML_DH)
    ig = m_i.astype(f32) + ml_i_bias.astype(f32)
    lf = jax.nn.log_sigmoid(m_f.astype(f32) + ml_f_bias.astype(f32))
    h, ml_c_new, ml_n_new, ml_m_new = mlstm_chunked(mq, mk, mv, ig, lf, ml_c.astype(f32),
                                                     ml_n.astype(f32), ml_m.astype(f32))
    h = jax.nn.sigmoid(m_o.astype(f32)).reshape(Bn, T, ML_HEADS, ML_DH) * h
    mu = jnp.mean(h, -1, keepdims=True)
    var = jnp.mean(jnp.square(h - mu), -1, keepdims=True)
    h = (h - mu) * lax.rsqrt(var + LN_EPS) * ml_norm_w.astype(f32).reshape(ML_HEADS, ML_DH)
    y_c = h.reshape(Bn, T, D_ML).astype(dt)
    mix = jnp.einsum('bte,ed->btd', jnp.concatenate([y_a, y_b, y_c], -1), w_o)
    x = layer_norm(ALPHA * x + mix, ln1_g, ln1_b)
    up = jnp.einsum('btd,df->btf', x, w_up)
    f_gate, f_val = jnp.split(up, 2, axis=-1)
    f_gate, ffn_buf_new = causal_dwconv(f_gate, ffn_buf, ffn_conv_w)
    ffn = jnp.einsum('btf,fd->btd', jax.nn.silu(f_gate) * f_val, w_down)
    x = layer_norm(ALPHA * x + ffn, ln2_g, ln2_b)
    return (x, conv_buf_new.astype(dt), gdn_buf_new.astype(dt), gdn_s_new.astype(dt),
            ml_c_new.astype(dt), ml_n_new.astype(dt), ml_m_new.astype(dt), ffn_buf_new.astype(dt))


def run_trunk(x, conv_buf, gdn_buf, gdn_s, ml_c, ml_n, ml_m, ffn_buf, weights):
    new = []
    for l in range(DEPTH):
        x, *st = layer_forward(x, conv_buf[l], gdn_buf[l], gdn_s[l], ml_c[l], ml_n[l], ml_m[l], ffn_buf[l],
                               *[w[l] for w in weights])
        new.append(st)
    return x, [jnp.stack([s[i] for s in new]) for i in range(len(new[0]))]


def setup_inputs(seed: int = 0) -> dict:
    key = jax.random.key(seed)
    ks = jax.random.split(key, 32)
    f32 = jnp.float32

    def nrm(k, shape, s):
        return jax.random.normal(k, shape, f32) * s

    dtv = jnp.exp(jax.random.uniform(ks[13], (DEPTH, GDN_HEADS), f32, math.log(1e-3), math.log(1e-1)))
    return {
        'x_prompt': nrm(ks[0], (BATCH, SEQ, D_MODEL), 1.0),
        'x_sample': nrm(ks[1], (DEC_BATCH, DEC_SEQ, D_MODEL), 1.0),
        'state_conv_mix': nrm(ks[2], (DEPTH, DEC_BATCH, CONV_W - 1, D_CONV), 1.0),
        'state_gdn_conv': nrm(ks[3], (DEPTH, DEC_BATCH, GDN_CONV_W - 1, D_GDN_QKV), 1.0),
        'state_gdn': nrm(ks[4], (DEPTH, DEC_BATCH, GDN_HEADS, GDN_DK, GDN_DV), 0.1),
        'state_mlstm_c': nrm(ks[5], (DEPTH, DEC_BATCH, ML_HEADS, ML_DH, ML_DH), 0.1),
        'state_mlstm_n': nrm(ks[6], (DEPTH, DEC_BATCH, ML_HEADS, ML_DH), 0.1),
        'state_mlstm_m': jax.random.uniform(ks[7], (DEPTH, DEC_BATCH, ML_HEADS), f32, 0.0, 4.0),
        'state_ffn_conv': nrm(ks[8], (DEPTH, DEC_BATCH, FFN_CONV_W - 1, D_FF), 1.0),
        'w_in': nrm(ks[9], (DEPTH, D_MODEL, D_IN), D_MODEL ** -0.5),
        'conv_w': nrm(ks[10], (DEPTH, CONV_W, D_CONV), CONV_W ** -0.5),
        'gdn_conv_w': nrm(ks[11], (DEPTH, GDN_CONV_W, D_GDN_QKV), GDN_CONV_W ** -0.5),
        'gdn_a_log': jnp.log(jax.random.uniform(ks[12], (DEPTH, GDN_HEADS), f32, 1.0, 16.0)),
        'gdn_dt_bias': dtv + jnp.log(-jnp.expm1(-dtv)),
        'gdn_norm_w': 1.0 + nrm(ks[14], (DEPTH, GDN_DV), 0.02),
        'ml_i_bias': nrm(ks[15], (DEPTH, ML_HEADS), 0.1),
        'ml_f_bias': jax.random.uniform(ks[16], (DEPTH, ML_HEADS), f32, 3.0, 6.0),
        'ml_norm_w': 1.0 + nrm(ks[17], (DEPTH, D_ML), 0.02),
        'w_o': nrm(ks[18], (DEPTH, D_MIX, D_MODEL), (D_MIX ** -0.5) * BETA),
        'ln1_g': 1.0 + nrm(ks[19], (DEPTH, D_MODEL), 0.02),
        'ln1_b': nrm(ks[20], (DEPTH, D_MODEL), 0.02),
        'w_up': nrm(ks[21], (DEPTH, D_MODEL, 2 * D_FF), D_MODEL ** -0.5),
        'ffn_conv_w': nrm(ks[22], (DEPTH, FFN_CONV_W, D_FF), FFN_CONV_W ** -0.5),
        'w_down': nrm(ks[23], (DEPTH, D_FF, D_MODEL), (D_FF ** -0.5) * BETA),
        'ln2_g': 1.0 + nrm(ks[24], (DEPTH, D_MODEL), 0.02),
        'ln2_b': nrm(ks[25], (DEPTH, D_MODEL), 0.02),
    }


def reference(x_prompt, x_sample, state_conv_mix, state_gdn_conv, state_gdn, state_mlstm_c,
              state_mlstm_n, state_mlstm_m, state_ffn_conv,
              w_in, conv_w, gdn_conv_w, gdn_a_log, gdn_dt_bias, gdn_norm_w,
              ml_i_bias, ml_f_bias, ml_norm_w, w_o, ln1_g, ln1_b,
              w_up, ffn_conv_w, w_down, ln2_g, ln2_b):
    weights = (w_in, conv_w, gdn_conv_w, gdn_a_log, gdn_dt_bias, gdn_norm_w,
               ml_i_bias, ml_f_bias, ml_norm_w, w_o, ln1_g, ln1_b,
               w_up, ffn_conv_w, w_down, ln2_g, ln2_b)
    Bp = x_prompt.shape[0]
    dtp = x_prompt.dtype
    y_prompt, (p_conv, p_gconv, p_gdn, p_c, p_n, p_m, p_ffn) = run_trunk(
        x_prompt,
        jnp.zeros((DEPTH, Bp, CONV_W - 1, D_CONV), dtp),
        jnp.zeros((DEPTH, Bp, GDN_CONV_W - 1, D_GDN_QKV), dtp),
        jnp.zeros((DEPTH, Bp, GDN_HEADS, GDN_DK, GDN_DV), dtp),
        jnp.zeros((DEPTH, Bp, ML_HEADS, ML_DH, ML_DH), dtp),
        jnp.zeros((DEPTH, Bp, ML_HEADS, ML_DH), dtp),
        jnp.zeros((DEPTH, Bp, ML_HEADS), dtp),
        jnp.zeros((DEPTH, Bp, FFN_CONV_W - 1, D_FF), dtp),
        weights)
    y_sample, (s_conv, s_gconv, s_gdn, s_c, s_n, s_m, s_ffn) = run_trunk(
        x_sample, state_conv_mix, state_gdn_conv, state_gdn, state_mlstm_c, state_mlstm_n,
        state_mlstm_m, state_ffn_conv, weights)
    return (y_prompt, y_sample, p_conv, p_gconv, p_gdn, p_c, p_n, p_m, p_ffn,
            s_conv, s_gconv, s_gdn, s_c, s_n, s_m, s_ffn)
```

```python
import functools

import jax
import jax.numpy as jnp
from jax import lax
from jax.experimental import pallas as pl
from jax.experimental.pallas import tpu as pltpu

f32 = jnp.float32
bf16 = jnp.bfloat16

D_MODEL = 1024
DEPTH = 2
D_CONV = 256
GDN_HEADS = 4
GDN_D = 128
D_GDN = GDN_HEADS * GDN_D
D_QKV = 3 * D_GDN
ML_HEADS = 4
ML_DH = 64
D_ML = ML_HEADS * ML_DH
D_FF = 2816
CHUNK = 64
ALPHA = (2.0 * DEPTH) ** 0.25
LN_EPS = 1e-5
NORM_EPS = 1e-6

LANE = 128
HIST = 8
COL_A = 0
COL_G = 3 * D_CONV
COL_M = COL_G + 4 * D_GDN
COL_GATE = COL_M + 4 * D_ML
D_IN_PACKED = COL_GATE + LANE
G_LANE, B_LANE, I_LANE, F_LANE = 0, 4, 8, 12
VMEM_LIMIT = 56 * 1024 * 1024


def _dot(a, b):
    return jnp.dot(a.astype(bf16), b.astype(bf16), preferred_element_type=f32)


def _dot_nt(a, b):
    return lax.dot_general(a.astype(bf16), b.astype(bf16), (((1,), (1,)), ((), ())),
                           preferred_element_type=f32)


def _dot_tn(a, b):
    return lax.dot_general(a.astype(bf16), b.astype(bf16), (((0,), (0,)), ((), ())),
                           preferred_element_type=f32)


def _softplus(z):
    return jnp.maximum(z, 0.0) + jnp.log1p(jnp.exp(-jnp.abs(z)))


def _cumsum_rows(tril, x):
    hi = x.astype(bf16)
    r1 = x - hi.astype(f32)
    mid = r1.astype(bf16)
    lo = (r1 - mid.astype(f32)).astype(bf16)
    return (jnp.dot(tril, hi, preferred_element_type=f32)
            + jnp.dot(tril, mid, preferred_element_type=f32)
            + jnp.dot(tril, lo, preferred_element_type=f32))


def _layer_norm(x, g, b):
    mu = jnp.mean(x, -1, keepdims=True)
    xc = x - mu
    var = jnp.mean(xc * xc, -1, keepdims=True)
    return xc * lax.rsqrt(var + LN_EPS) * g + b


def _proj_kernel(x_ref, ha_ref, hg_ref, w_ref, cw_ref, gcw_ref, gp_ref,
                 ya_ref, qkv_ref, gz_ref, m4_ref, gates_ref, ta_ref, tg_ref,
                 sa_ref, sg_ref, *, bb, tt):
    t = pl.program_id(1)

    @pl.when(t == 0)
    def _():
        sa_ref[:, 0:HIST, :] = ha_ref[...]
        sg_ref[:, 0:HIST, :] = hg_ref[...]

    rows = bb * tt
    xb = x_ref[...].reshape(rows, D_MODEL).astype(bf16)

    def proj(lo, hi):
        return jnp.dot(xb, w_ref[:, lo:hi], preferred_element_type=f32)

    pa = proj(COL_A, COL_G)
    a_b = pa[:, 0:D_CONV]
    sa_ref[:, HIST:, :] = (pa[:, D_CONV:2 * D_CONV] * pa[:, 2 * D_CONV:3 * D_CONV]).reshape(bb, tt, D_CONV)
    conv = cw_ref[0:1, :] * sa_ref[:, pl.ds(HIST - 2, tt), :]
    for j in range(1, 3):
        conv = conv + cw_ref[j:j + 1, :] * sa_ref[:, pl.ds(HIST - 2 + j, tt), :]
    ya_ref[...] = a_b.reshape(bb, tt, D_CONV) * conv
    tail = sa_ref[:, tt:tt + HIST, :]
    ta_ref[...] = tail
    sa_ref[:, 0:HIST, :] = tail

    pg = proj(COL_G, COL_M)
    sg_ref[:, HIST:, :] = pg[:, 0:D_QKV].reshape(bb, tt, D_QKV)
    gz_ref[...] = jax.nn.silu(pg[:, D_QKV:]).reshape(bb, tt, D_GDN)
    for cb in range(D_QKV // LANE):
        cs = slice(cb * LANE, (cb + 1) * LANE)
        c = gcw_ref[0:1, cs] * sg_ref[:, pl.ds(HIST - 3, tt), cs]
        for j in range(1, 4):
            c = c + gcw_ref[j:j + 1, cs] * sg_ref[:, pl.ds(HIST - 3 + j, tt), cs]
        c = jax.nn.silu(c)
        if cb < 2 * GDN_HEADS:
            c = c * lax.rsqrt(jnp.sum(c * c, -1, keepdims=True) + NORM_EPS)
            if cb < GDN_HEADS:
                c = c * (GDN_D ** -0.5)
        qkv_ref[:, :, cs] = c
    tail = sg_ref[:, tt:tt + HIST, :]
    tg_ref[...] = tail
    sg_ref[:, 0:HIST, :] = tail

    pm = proj(COL_M, COL_GATE)
    m4_ref[:, :, 0:D_ML] = pm[:, 0:D_ML].reshape(bb, tt, D_ML)
    m4_ref[:, :, D_ML:2 * D_ML] = (pm[:, D_ML:2 * D_ML] * (ML_DH ** -0.5)).reshape(bb, tt, D_ML)
    m4_ref[:, :, 2 * D_ML:3 * D_ML] = pm[:, 2 * D_ML:3 * D_ML].reshape(bb, tt, D_ML)
    m4_ref[:, :, 3 * D_ML:4 * D_ML] = jax.nn.sigmoid(pm[:, 3 * D_ML:4 * D_ML]).reshape(bb, tt, D_ML)

    z = proj(COL_GATE, D_IN_PACKED) + gp_ref[0:1, :]
    lane = lax.broadcasted_iota(jnp.int32, z.shape, 1)
    g = -jnp.exp(gp_ref[1:2, :]) * _softplus(z)
    gates = jnp.where(lane < B_LANE, g,
                      jnp.where(lane < I_LANE, jax.nn.sigmoid(z),
                                jnp.where(lane < F_LANE, z,
                                          jnp.where(lane < F_LANE + ML_HEADS, -_softplus(-z), 0.0))))
    gates_ref[...] = gates.reshape(bb, tt, LANE)


def _proj_call(x, hist_a, hist_g, w_in, conv_w, gdn_conv_w, gate_p, *, bb, tt):
    B, T, _ = x.shape
    grid = (B // bb, T // tt)
    tile = lambda c: pl.BlockSpec((bb, tt, c), lambda b, t: (b, t, 0))
    head = lambda c: pl.BlockSpec((bb, HIST, c), lambda b, t: (b, 0, 0))
    full = lambda a: pl.BlockSpec(a.shape, lambda b, t: (0,) * a.ndim)
    out_shape = (
        jax.ShapeDtypeStruct((B, T, D_CONV), f32),
        jax.ShapeDtypeStruct((B, T, D_QKV), f32),
        jax.ShapeDtypeStruct((B, T, D_GDN), f32),
        jax.ShapeDtypeStruct((B, T, 4 * D_ML), f32),
        jax.ShapeDtypeStruct((B, T, LANE), f32),
        jax.ShapeDtypeStruct((B, HIST, D_CONV), f32),
        jax.ShapeDtypeStruct((B, HIST, D_QKV), f32),
    )
    return pl.pallas_call(
        functools.partial(_proj_kernel, bb=bb, tt=tt),
        grid=grid,
        in_specs=[tile(D_MODEL), head(D_CONV), head(D_QKV), full(w_in), full(conv_w),
                  full(gdn_conv_w), full(gate_p)],
        out_specs=(tile(D_CONV), tile(D_QKV), tile(D_GDN), tile(4 * D_ML), tile(LANE),
                   head(D_CONV), head(D_QKV)),
        out_shape=out_shape,
        scratch_shapes=[pltpu.VMEM((bb, tt + HIST, D_CONV), f32),
                        pltpu.VMEM((bb, tt + HIST, D_QKV), f32)],
        compiler_params=pltpu.CompilerParams(dimension_semantics=("arbitrary", "arbitrary"),
                                             vmem_limit_bytes=VMEM_LIMIT),
        name="in_proj",
    )(x, hist_a, hist_g, w_in, conv_w, gdn_conv_w, gate_p)


def _gdn_kernel(qkv_ref, gates_ref, gz_ref, s0_ref, nw_ref, yb_ref, s_ref, *, bb, nc, L):
    t = pl.program_id(1)

    @pl.when(t == 0)
    def _():
        s_ref[...] = s0_ref[...]

    r = lax.broadcasted_iota(jnp.int32, (L, L), 0)
    c = lax.broadcasted_iota(jnp.int32, (L, L), 1)
    causal = c <= r
    strict = c < r
    tril = causal.astype(bf16)
    steps = L.bit_length() - 2
    nw = nw_ref[...]

    for bi in range(bb):
        for ci in range(nc):
            rs = slice(ci * L, (ci + 1) * L)
            gt = gates_ref[bi, rs, :]
            g_all = _cumsum_rows(tril, gt)
            g_all_t = g_all.T
            for h in range(GDN_HEADS):
                hs = slice(h * GDN_D, (h + 1) * GDN_D)
                q = qkv_ref[bi, rs, hs]
                k = qkv_ref[bi, rs, D_GDN + h * GDN_D:D_GDN + (h + 1) * GDN_D]
                v = qkv_ref[bi, rs, 2 * D_GDN + h * GDN_D:2 * D_GDN + (h + 1) * GDN_D]
                G = g_all[:, G_LANE + h:G_LANE + h + 1]
                g_row = g_all_t[G_LANE + h:G_LANE + h + 1, :]
                g_last = g_all[L - 1:L, G_LANE + h:G_LANE + h + 1]
                beta = gt[:, B_LANE + h:B_LANE + h + 1]
                decay = jnp.where(causal, jnp.exp(jnp.where(causal, G - g_row, 0.0)), 0.0)
                kb = k * beta
                kk = _dot_nt(kb, k) * decay
                qk = _dot_nt(q, k) * decay
                P = jnp.where(strict, kk, 0.0)
                R = -P
                for _ in range(steps):
                    P = _dot(P, P)
                    R = R + P + _dot(R, P)
                eG = jnp.exp(G)
                rhs = jnp.concatenate([v * beta, kb * eG], axis=1)
                w = rhs + _dot(R, rhs)
                S = s_ref[bi, h]
                Sb = S.astype(bf16)
                u = w[:, 0:GDN_D] - _dot(w[:, GDN_D:], Sb)
                o = _dot(q * eG, Sb) + _dot(qk, u)
                s_ref[bi, h] = S * jnp.exp(g_last) + _dot_tn(k * jnp.exp(g_last - G), u)
                o = o * lax.rsqrt(jnp.mean(o * o, -1, keepdims=True) + NORM_EPS) * nw
                yb_ref[bi, rs, hs] = o * gz_ref[bi, rs, hs]


def _gdn_call(qkv, gates, gz, s0, norm_w, *, bb, nc, L):
    B, T, _ = qkv.shape
    tt = nc * L
    grid = (B // bb, T // tt)
    tile = lambda c: pl.BlockSpec((bb, tt, c), lambda b, t: (b, t, 0))
    st = pl.BlockSpec((bb, GDN_HEADS, GDN_D, GDN_D), lambda b, t: (b, 0, 0, 0))
    return pl.pallas_call(
        functools.partial(_gdn_kernel, bb=bb, nc=nc, L=L),
        grid=grid,
        in_specs=[tile(D_QKV), tile(LANE), tile(D_GDN), st,
                  pl.BlockSpec((1, GDN_D), lambda b, t: (0, 0))],
        out_specs=(tile(D_GDN), st),
        out_shape=(jax.ShapeDtypeStruct((B, T, D_GDN), f32),
                   jax.ShapeDtypeStruct(s0.shape, f32)),
        compiler_params=pltpu.CompilerParams(dimension_semantics=("arbitrary", "arbitrary"),
                                             vmem_limit_bytes=VMEM_LIMIT),
        name="gdn",
    )(qkv, gates, gz, s0, norm_w)


def _mlstm_kernel(m4_ref, gates_ref, c0_ref, n0_ref, m0_ref, nw_ref,
                  yc_ref, c_ref, n_ref, m_ref, *, bb, nc, L):
    t = pl.program_id(1)

    @pl.when(t == 0)
    def _():
        c_ref[...] = c0_ref[...]
        n_ref[...] = n0_ref[...]
        m_ref[...] = m0_ref[...]

    r = lax.broadcasted_iota(jnp.int32, (L, L), 0)
    c = lax.broadcasted_iota(jnp.int32, (L, L), 1)
    causal = c <= r
    tril = causal.astype(bf16)

    for bi in range(bb):
        for ci in range(nc):
            rs = slice(ci * L, (ci + 1) * L)
            gt = gates_ref[bi, rs, :]
            f_all = _cumsum_rows(tril, gt)
            f_all_t = f_all.T
            gt_t = gt.T
            for h in range(ML_HEADS):
                hs = slice(h * ML_DH, (h + 1) * ML_DH)
                q = m4_ref[bi, rs, hs]
                k = m4_ref[bi, rs, D_ML + h * ML_DH:D_ML + (h + 1) * ML_DH]
                v = m4_ref[bi, rs, 2 * D_ML + h * ML_DH:2 * D_ML + (h + 1) * ML_DH]
                og = m4_ref[bi, rs, 3 * D_ML + h * ML_DH:3 * D_ML + (h + 1) * ML_DH]
                F = f_all[:, F_LANE + h:F_LANE + h + 1]
                f_row = f_all_t[F_LANE + h:F_LANE + h + 1, :]
                f_last = f_all[L - 1:L, F_LANE + h:F_LANE + h + 1]
                ig = gt[:, I_LANE + h:I_LANE + h + 1]
                ig_row = gt_t[I_LANE + h:I_LANE + h + 1, :]
                m = m_ref[bi, 0:1, h:h + 1]
                C = c_ref[bi, h]
                n = n_ref[bi, h:h + 1, :]

                Dm = jnp.where(causal, F - f_row + ig_row, -jnp.inf)
                inter = F + m
                m_i = jnp.maximum(inter, jnp.max(Dm, -1, keepdims=True))
                wD = jnp.exp(Dm - m_i)
                wI = jnp.exp(inter - m_i)
                s = _dot_nt(q, k) * wD
                num = wI * _dot(q, C) + _dot(s, v)
                qn = wI * jnp.sum(q * n, -1, keepdims=True) + jnp.sum(s, -1, keepdims=True)
                hh = num / jnp.maximum(jnp.abs(qn), jnp.exp(-m_i))
                m_new = m_i[L - 1:L, :]
                wk = jnp.exp(f_last - F + ig - m_new)
                dc = jnp.exp(f_last + m - m_new)
                kw = k * wk
                c_ref[bi, h] = dc * C + _dot_tn(kw, v)
                n_ref[bi, h:h + 1, :] = dc * n + jnp.sum(kw, 0, keepdims=True)
                m_ref[bi, 0:1, h:h + 1] = m_new

                hh = og * hh
                mu = jnp.mean(hh, -1, keepdims=True)
                hc = hh - mu
                var = jnp.mean(hc * hc, -1, keepdims=True)
                yc_ref[bi, rs, hs] = hc * lax.rsqrt(var + LN_EPS) * nw_ref[0:1, hs]


def _mlstm_call(m4, gates, c0, n0, m0, norm_w, *, bb, nc, L):
    B, T, _ = m4.shape
    tt = nc * L
    grid = (B // bb, T // tt)
    tile = lambda c: pl.BlockSpec((bb, tt, c), lambda b, t: (b, t, 0))
    cst = pl.BlockSpec((bb, ML_HEADS, ML_DH, ML_DH), lambda b, t: (b, 0, 0, 0))
    nst = pl.BlockSpec((bb, ML_HEADS, ML_DH), lambda b, t: (b, 0, 0))
    mst = pl.BlockSpec((bb, 1, ML_HEADS), lambda b, t: (b, 0, 0))
    return pl.pallas_call(
        functools.partial(_mlstm_kernel, bb=bb, nc=nc, L=L),
        grid=grid,
        in_specs=[tile(4 * D_ML), tile(LANE), cst, nst, mst,
                  pl.BlockSpec((1, D_ML), lambda b, t: (0, 0))],
        out_specs=(tile(D_ML), cst, nst, mst),
        out_shape=(jax.ShapeDtypeStruct((B, T, D_ML), f32),
                   jax.ShapeDtypeStruct(c0.shape, f32),
                   jax.ShapeDtypeStruct(n0.shape, f32),
                   jax.ShapeDtypeStruct(m0.shape, f32)),
        compiler_params=pltpu.CompilerParams(dimension_semantics=("arbitrary", "arbitrary"),
                                             vmem_limit_bytes=VMEM_LIMIT),
        name="mlstm",
    )(m4, gates, c0, n0, m0, norm_w)


def _out_ffn_kernel(x_ref, ya_ref, yb_ref, yc_ref, hf_ref, wo_ref, ln1_ref, wup_ref, fcw_ref,
                    wdn_ref, ln2_ref, xo_ref, tf_ref, sf_ref, *, bb, tt):
    t = pl.program_id(1)

    @pl.when(t == 0)
    def _():
        sf_ref[:, 0:HIST, :] = hf_ref[...]

    rows = bb * tt
    x = x_ref[...].reshape(rows, D_MODEL)
    mix = (jnp.dot(ya_ref[...].reshape(rows, D_CONV).astype(bf16), wo_ref[0:D_CONV, :],
                   preferred_element_type=f32)
           + jnp.dot(yb_ref[...].reshape(rows, D_GDN).astype(bf16), wo_ref[D_CONV:D_CONV + D_GDN, :],
                     preferred_element_type=f32)
           + jnp.dot(yc_ref[...].reshape(rows, D_ML).astype(bf16), wo_ref[D_CONV + D_GDN:, :],
                     preferred_element_type=f32))
    x1 = _layer_norm(ALPHA * x + mix, ln1_ref[0:1, :], ln1_ref[1:2, :])
    x1b = x1.astype(bf16)

    half = D_FF // 2
    acc = None
    for ch in range(2):
        lo = ch * half
        gate = jnp.dot(x1b, wup_ref[:, lo:lo + half], preferred_element_type=f32)
        val = jnp.dot(x1b, wup_ref[:, D_FF + lo:D_FF + lo + half], preferred_element_type=f32)
        sf_ref[:, HIST:, lo:lo + half] = gate.reshape(bb, tt, half)
        conv = fcw_ref[0:1, lo:lo + half] * sf_ref[:, pl.ds(HIST - 2, tt), lo:lo + half]
        for j in range(1, 3):
            conv = conv + fcw_ref[j:j + 1, lo:lo + half] * sf_ref[:, pl.ds(HIST - 2 + j, tt), lo:lo + half]
        hmid = jax.nn.silu(conv).reshape(rows, half) * val
        part = jnp.dot(hmid.astype(bf16), wdn_ref[lo:lo + half, :], preferred_element_type=f32)
        acc = part if acc is None else acc + part
    tail = sf_ref[:, tt:tt + HIST, :]
    tf_ref[...] = tail
    sf_ref[:, 0:HIST, :] = tail
    x2 = _layer_norm(ALPHA * x1 + acc, ln2_ref[0:1, :], ln2_ref[1:2, :])
    xo_ref[...] = x2.reshape(bb, tt, D_MODEL)


def _out_ffn_call(x, ya, yb, yc, hist_f, w_o, ln1, w_up, ffn_conv_w, w_down, ln2, *, bb, tt):
    B, T, _ = x.shape
    grid = (B // bb, T // tt)
    tile = lambda c: pl.BlockSpec((bb, tt, c), lambda b, t: (b, t, 0))
    head = lambda c: pl.BlockSpec((bb, HIST, c), lambda b, t: (b, 0, 0))
    full = lambda a: pl.BlockSpec(a.shape, lambda b, t: (0,) * a.ndim,
                                  pipeline_mode=pl.Buffered(1))
    return pl.pallas_call(
        functools.partial(_out_ffn_kernel, bb=bb, tt=tt),
        grid=grid,
        in_specs=[tile(D_MODEL), tile(D_CONV), tile(D_GDN), tile(D_ML), head(D_FF),
                  full(w_o), full(ln1), full(w_up), full(ffn_conv_w), full(w_down), full(ln2)],
        out_specs=(tile(D_MODEL), head(D_FF)),
        out_shape=(jax.ShapeDtypeStruct((B, T, D_MODEL), f32),
                   jax.ShapeDtypeStruct((B, HIST, D_FF), f32)),
        scratch_shapes=[pltpu.VMEM((bb, tt + HIST, D_FF), f32)],
        compiler_params=pltpu.CompilerParams(dimension_semantics=("arbitrary", "arbitrary"),
                                             vmem_limit_bytes=VMEM_LIMIT),
        name="out_ffn",
    )(x, ya, yb, yc, hist_f, w_o, ln1, w_up, ffn_conv_w, w_down, ln2)


def _pad_hist(buf):
    return jnp.pad(buf, ((0, 0), (HIST - buf.shape[1], 0), (0, 0)))


def _prep_layer(l, w_in, conv_w, gdn_conv_w, gdn_a_log, gdn_dt_bias, gdn_norm_w,
                ml_i_bias, ml_f_bias, ml_norm_w, w_o, ln1_g, ln1_b,
                w_up, ffn_conv_w, w_down, ln2_g, ln2_b):
    w = w_in[l]
    o = 0
    seg = {}
    for name, n in (("a", 3 * D_CONV), ("g", 4 * D_GDN), ("ga", GDN_HEADS), ("gb", GDN_HEADS),
                    ("m", 4 * D_ML), ("mi", ML_HEADS), ("mf", ML_HEADS)):
        seg[name] = w[:, o:o + n]
        o += n
    gate_cols = jnp.concatenate([seg["ga"], seg["gb"], seg["mi"], seg["mf"]], axis=1)
    gate_cols = jnp.pad(gate_cols, ((0, 0), (0, LANE - gate_cols.shape[1])))
    w_packed = jnp.concatenate([seg["a"], seg["g"], seg["m"], gate_cols], axis=1).astype(bf16)
    zeros4 = jnp.zeros((4,), f32)
    bias_row = jnp.pad(jnp.concatenate([gdn_dt_bias[l], zeros4, ml_i_bias[l], ml_f_bias[l]]), (0, LANE - 16))
    alog_row = jnp.pad(gdn_a_log[l], (0, LANE - GDN_HEADS))
    gate_p = jnp.zeros((8, LANE), f32).at[0].set(bias_row).at[1].set(alog_row)
    return dict(
        w_in=w_packed, conv_w=conv_w[l], gdn_conv_w=gdn_conv_w[l], gate_p=gate_p,
        gdn_norm_w=gdn_norm_w[l].reshape(1, GDN_D), ml_norm_w=ml_norm_w[l].reshape(1, D_ML),
        w_o=w_o[l].astype(bf16), ln1=jnp.stack([ln1_g[l], ln1_b[l]]),
        w_up=w_up[l].astype(bf16), ffn_conv_w=ffn_conv_w[l], w_down=w_down[l].astype(bf16),
        ln2=jnp.stack([ln2_g[l], ln2_b[l]]),
    )


def _layer(x, conv_buf, gdn_buf, gdn_s, ml_c, ml_n, ml_m, ffn_buf, p, *, bb, tt, nc, L, rbb):
    B = x.shape[0]
    ya, qkv, gz, m4, gates, tail_a, tail_g = _proj_call(
        x, _pad_hist(conv_buf), _pad_hist(gdn_buf), p["w_in"], p["conv_w"], p["gdn_conv_w"],
        p["gate_p"], bb=bb, tt=tt)
    yb, s_new = _gdn_call(qkv, gates, gz, gdn_s, p["gdn_norm_w"], bb=rbb, nc=nc, L=L)
    yc, c_new, n_new, m_new = _mlstm_call(m4, gates, ml_c, ml_n, ml_m.reshape(B, 1, ML_HEADS),
                                          p["ml_norm_w"], bb=rbb, nc=nc, L=L)
    x_out, tail_f = _out_ffn_call(x, ya, yb, yc, _pad_hist(ffn_buf), p["w_o"], p["ln1"], p["w_up"],
                                  p["ffn_conv_w"], p["w_down"], p["ln2"], bb=bb, tt=tt)
    return x_out, (tail_a[:, HIST - 2:], tail_g[:, HIST - 3:], s_new, c_new, n_new,
                   m_new.reshape(B, ML_HEADS), tail_f[:, HIST - 2:])


def _trunk(x, states, params, **cfg):
    new = []
    for l in range(DEPTH):
        x, st = _layer(x, *[s[l] for s in states], params[l], **cfg)
        new.append(st)
    return x, [jnp.stack([s[i] for s in new]) for i in range(len(new[0]))]


def kernel(x_prompt, x_sample, state_conv_mix, state_gdn_conv, state_gdn, state_mlstm_c,
           state_mlstm_n, state_mlstm_m, state_ffn_conv,
           w_in, conv_w, gdn_conv_w, gdn_a_log, gdn_dt_bias, gdn_norm_w,
           ml_i_bias, ml_f_bias, ml_norm_w, w_o, ln1_g, ln1_b,
           w_up, ffn_conv_w, w_down, ln2_g, ln2_b):
    weights = (w_in, conv_w, gdn_conv_w, gdn_a_log, gdn_dt_bias, gdn_norm_w,
               ml_i_bias, ml_f_bias, ml_norm_w, w_o, ln1_g, ln1_b,
               w_up, ffn_conv_w, w_down, ln2_g, ln2_b)
    params = [_prep_layer(l, *weights) for l in range(DEPTH)]
    Bp, Tp, _ = x_prompt.shape
    Bs, Ts, _ = x_sample.shape
    zeros = lambda *s: jnp.zeros((DEPTH, Bp) + s, f32)
    p_states = (zeros(2, D_CONV), zeros(3, D_QKV), zeros(GDN_HEADS, GDN_D, GDN_D),
                zeros(ML_HEADS, ML_DH, ML_DH), zeros(ML_HEADS, ML_DH), zeros(ML_HEADS),
                zeros(2, D_FF))
    Lp = min(CHUNK, Tp)
    y_prompt, p_new = _trunk(x_prompt, p_states, params, bb=1, tt=256, nc=256 // Lp, L=Lp, rbb=1)
    s_states = (state_conv_mix, state_gdn_conv, state_gdn, state_mlstm_c, state_mlstm_n,
                state_mlstm_m, state_ffn_conv)
    Ls = min(CHUNK, Ts)
    y_sample, s_new = _trunk(x_sample, s_states, params, bb=32, tt=Ts, nc=Ts // Ls, L=Ls, rbb=4)
    return (y_prompt, y_sample, *p_new, *s_new)
```

```python
import functools

import jax
import jax.numpy as jnp
from jax import lax
from jax.experimental import pallas as pl
from jax.experimental.pallas import tpu as pltpu

f32 = jnp.float32
bf16 = jnp.bfloat16

D_MODEL = 1024
DEPTH = 2
D_CONV = 256
GDN_HEADS = 4
GDN_D = 128
D_GDN = GDN_HEADS * GDN_D
D_QKV = 3 * D_GDN
ML_HEADS = 4
ML_DH = 64
D_ML = ML_HEADS * ML_DH
D_FF = 2816
CHUNK = 64
ALPHA = (2.0 * DEPTH) ** 0.25
LN_EPS = 1e-5
NORM_EPS = 1e-6

LANE = 128
HIST = 8
COL_A = 0
COL_G = 3 * D_CONV
COL_M = COL_G + 4 * D_GDN
COL_GATE = COL_M + 4 * D_ML
D_IN_PACKED = COL_GATE + LANE
G_LANE, B_LANE, I_LANE, F_LANE = 0, 4, 8, 12
VMEM_LIMIT = 56 * 1024 * 1024


def _softplus(z):
    return jnp.maximum(z, 0.0) + jnp.log1p(jnp.exp(-jnp.abs(z)))


def _layer_norm(x, g, b):
    mu = jnp.mean(x, -1, keepdims=True)
    xc = x - mu
    var = jnp.mean(xc * xc, -1, keepdims=True)
    return xc * lax.rsqrt(var + LN_EPS) * g + b


def _proj_kernel(x_ref, ha_ref, hg_ref, w_ref, cw_ref, gcw_ref, gp_ref,
                 ya_ref, qkv_ref, gz_ref, m4_ref, gates_ref, ta_ref, tg_ref,
                 sa_ref, sg_ref, *, bb, tt):
    t = pl.program_id(1)

    @pl.when(t == 0)
    def _():
        sa_ref[:, 0:HIST, :] = ha_ref[...]
        sg_ref[:, 0:HIST, :] = hg_ref[...]

    rows = bb * tt
    xb = x_ref[...].reshape(rows, D_MODEL).astype(bf16)

    def proj(lo, hi):
        return jnp.dot(xb, w_ref[:, lo:hi], preferred_element_type=f32)

    pa = proj(COL_A, COL_G)
    a_b = pa[:, 0:D_CONV]
    sa_ref[:, HIST:, :] = (pa[:, D_CONV:2 * D_CONV] * pa[:, 2 * D_CONV:3 * D_CONV]).reshape(bb, tt, D_CONV)
    conv = cw_ref[0:1, :] * sa_ref[:, pl.ds(HIST - 2, tt), :]
    for j in range(1, 3):
        conv = conv + cw_ref[j:j + 1, :] * sa_ref[:, pl.ds(HIST - 2 + j, tt), :]
    ya_ref[...] = a_b.reshape(bb, tt, D_CONV) * conv
    tail = sa_ref[:, tt:tt + HIST, :]
    ta_ref[...] = tail
    sa_ref[:, 0:HIST, :] = tail

    pg = proj(COL_G, COL_M)
    sg_ref[:, HIST:, :] = pg[:, 0:D_QKV].reshape(bb, tt, D_QKV)
    gz_ref[...] = jax.nn.silu(pg[:, D_QKV:]).reshape(bb, tt, D_GDN)
    for cb in range(D_QKV // LANE):
        cs = slice(cb * LANE, (cb + 1) * LANE)
        c = gcw_ref[0:1, cs] * sg_ref[:, pl.ds(HIST - 3, tt), cs]
        for j in range(1, 4):
            c = c + gcw_ref[j:j + 1, cs] * sg_ref[:, pl.ds(HIST - 3 + j, tt), cs]
        c = jax.nn.silu(c)
        if cb < 2 * GDN_HEADS:
            c = c * lax.rsqrt(jnp.sum(c * c, -1, keepdims=True) + NORM_EPS)
            if cb < GDN_HEADS:
                c = c * (GDN_D ** -0.5)
        qkv_ref[:, :, cs] = c
    tail = sg_ref[:, tt:tt + HIST, :]
    tg_ref[...] = tail
    sg_ref[:, 0:HIST, :] = tail

    pm = proj(COL_M, COL_GATE)
    m4_ref[:, :, 0:D_ML] = pm[:, 0:D_ML].reshape(bb, tt, D_ML)
    m4_ref[:, :, D_ML:2 * D_ML] = (pm[:, D_ML:2 * D_ML] * (ML_DH ** -0.5)).reshape(bb, tt, D_ML)
    m4_ref[:, :, 2 * D_ML:3 * D_ML] = pm[:, 2 * D_ML:3 * D_ML].reshape(bb, tt, D_ML)
    m4_ref[:, :, 3 * D_ML:4 * D_ML] = jax.nn.sigmoid(pm[:, 3 * D_ML:4 * D_ML]).reshape(bb, tt, D_ML)

    z = proj(COL_GATE, D_IN_PACKED) + gp_ref[0:1, :]
    lane = lax.broadcasted_iota(jnp.int32, z.shape, 1)
    g = -jnp.exp(gp_ref[1:2, :]) * _softplus(z)
    gates = jnp.where(lane < B_LANE, g,
                      jnp.where(lane < I_LANE, jax.nn.sigmoid(z),
                                jnp.where(lane < F_LANE, z,
                                          jnp.where(lane < F_LANE + ML_HEADS, -_softplus(-z), 0.0))))
    gates_ref[...] = gates.reshape(bb, tt, LANE)


def _proj_call(x, hist_a, hist_g, w_in, conv_w, gdn_conv_w, gate_p, *, bb, tt):
    B, T, _ = x.shape
    grid = (B // bb, T // tt)
    tile = lambda c: pl.BlockSpec((bb, tt, c), lambda b, t: (b, t, 0))
    head = lambda c: pl.BlockSpec((bb, HIST, c), lambda b, t: (b, 0, 0))
    full = lambda a: pl.BlockSpec(a.shape, lambda b, t: (0,) * a.ndim)
    out_shape = (
        jax.ShapeDtypeStruct((B, T, D_CONV), f32),
        jax.ShapeDtypeStruct((B, T, D_QKV), f32),
        jax.ShapeDtypeStruct((B, T, D_GDN), f32),
        jax.ShapeDtypeStruct((B, T, 4 * D_ML), f32),
        jax.ShapeDtypeStruct((B, T, LANE), f32),
        jax.ShapeDtypeStruct((B, HIST, D_CONV), f32),
        jax.ShapeDtypeStruct((B, HIST, D_QKV), f32),
    )
    return pl.pallas_call(
        functools.partial(_proj_kernel, bb=bb, tt=tt),
        grid=grid,
        in_specs=[tile(D_MODEL), head(D_CONV), head(D_QKV), full(w_in), full(conv_w),
                  full(gdn_conv_w), full(gate_p)],
        out_specs=(tile(D_CONV), tile(D_QKV), tile(D_GDN), tile(4 * D_ML), tile(LANE),
                   head(D_CONV), head(D_QKV)),
        out_shape=out_shape,
        scratch_shapes=[pltpu.VMEM((bb, tt + HIST, D_CONV), f32),
                        pltpu.VMEM((bb, tt + HIST, D_QKV), f32)],
        compiler_params=pltpu.CompilerParams(dimension_semantics=("arbitrary", "arbitrary"),
                                             vmem_limit_bytes=VMEM_LIMIT),
        name="in_proj",
    )(x, hist_a, hist_g, w_in, conv_w, gdn_conv_w, gate_p)


def _bdot(a, b):
    return jnp.einsum('pik,pkj->pij', a.astype(bf16), b.astype(bf16), preferred_element_type=f32)


def _bdot_nt(a, b):
    return jnp.einsum('pik,pjk->pij', a.astype(bf16), b.astype(bf16), preferred_element_type=f32)


def _bdot_tn(a, b):
    return jnp.einsum('pki,pkj->pij', a.astype(bf16), b.astype(bf16), preferred_element_type=f32)


def _chunk_masks(L):
    r = lax.broadcasted_iota(jnp.int32, (L, L), 0)
    c = lax.broadcasted_iota(jnp.int32, (L, L), 1)
    return c <= r, c < r


def _gate_columns(gates_ref, tril, *, bb, nc, L, heads):
    gt = jnp.stack([gates_ref[bi, ci * L:(ci + 1) * L, :] for bi in range(bb) for ci in range(nc)])
    hi = gt.astype(bf16)
    r1 = gt - hi.astype(f32)
    mid = r1.astype(bf16)
    lo = (r1 - mid.astype(f32)).astype(bf16)
    trb = jnp.broadcast_to(tril, (bb * nc, L, L))
    cs = (jnp.einsum('pij,pjk->pik', trb, hi, preferred_element_type=f32)
          + jnp.einsum('pij,pjk->pik', trb, mid, preferred_element_type=f32)
          + jnp.einsum('pij,pjk->pik', trb, lo, preferred_element_type=f32))

    def col(x, lane0):
        return jnp.stack([x[:, :, lane0 + h:lane0 + h + 1] for h in range(heads)],
                         axis=1).reshape(bb * nc * heads, L, 1)

    def row(xt, lane0):
        return jnp.stack([xt[:, lane0 + h:lane0 + h + 1, :] for h in range(heads)],
                         axis=1).reshape(bb * nc * heads, 1, L)

    return gt, cs, col, row


def _gdn_kernel(qkv_ref, gates_ref, gz_ref, s0_ref, nw_ref, yb_ref, s_ref, *, bb, nc, L):
    t = pl.program_id(1)

    @pl.when(t == 0)
    def _():
        s_ref[...] = s0_ref[...]

    H = GDN_HEADS
    causal, strict = _chunk_masks(L)
    tril = causal.astype(bf16)
    steps = L.bit_length() - 2

    def heads_of(ref, col0):
        return jnp.stack([ref[bi, ci * L:(ci + 1) * L, col0 + h * GDN_D:col0 + (h + 1) * GDN_D]
                          for bi in range(bb) for ci in range(nc) for h in range(H)])

    gt, cs, col, row = _gate_columns(gates_ref, tril, bb=bb, nc=nc, L=L, heads=H)
    cs_t = jnp.swapaxes(cs, 1, 2)
    G = col(cs, G_LANE)
    g_row = row(cs_t, G_LANE)
    g_last = G[:, L - 1:L, :]
    beta = col(gt, B_LANE)
    q = heads_of(qkv_ref, 0)
    k = heads_of(qkv_ref, D_GDN)
    v = heads_of(qkv_ref, 2 * D_GDN)

    decay = jnp.where(causal, jnp.exp(jnp.where(causal, G - g_row, 0.0)), 0.0)
    kb = k * beta
    kk = _bdot_nt(kb, k) * decay
    qk = _bdot_nt(q, k) * decay
    P = jnp.where(strict, kk, 0.0)
    R = -P
    for _ in range(steps):
        P = _bdot(P, P)
        R = R + P + _bdot(R, P)
    eG = jnp.exp(G)
    rhs = jnp.concatenate([v * beta, kb * eG], axis=-1)
    w = rhs + _bdot(R, rhs)
    qg = q * eG
    kdec = k * jnp.exp(g_last - G)
    e_last = jnp.exp(g_last)

    def chunk(x, ci):
        return x.reshape((bb, nc, H) + x.shape[1:])[:, ci].reshape((bb * H,) + x.shape[1:])

    nw = nw_ref[...]
    S = s_ref[...].reshape(bb * H, GDN_D, GDN_D)
    for ci in range(nc):
        rs = slice(ci * L, (ci + 1) * L)
        w_c = chunk(w, ci)
        xs = _bdot(jnp.concatenate([w_c[:, :, GDN_D:], chunk(qg, ci)], axis=1), S)
        u = w_c[:, :, 0:GDN_D] - xs[:, 0:L]
        o = xs[:, L:] + _bdot(chunk(qk, ci), u)
        S = S * chunk(e_last, ci) + _bdot_tn(chunk(kdec, ci), u)
        o = o * lax.rsqrt(jnp.mean(o * o, -1, keepdims=True) + NORM_EPS) * nw
        o = o.reshape(bb, H, L, GDN_D)
        for bi in range(bb):
            for h in range(H):
                hs = slice(h * GDN_D, (h + 1) * GDN_D)
                yb_ref[bi, rs, hs] = o[bi, h] * gz_ref[bi, rs, hs]
    s_ref[...] = S.reshape(bb, H, GDN_D, GDN_D)


def _gdn_call(qkv, gates, gz, s0, norm_w, *, bb, nc, L):
    B, T, _ = qkv.shape
    tt = nc * L
    grid = (B // bb, T // tt)
    tile = lambda c: pl.BlockSpec((bb, tt, c), lambda b, t: (b, t, 0))
    st = pl.BlockSpec((bb, GDN_HEADS, GDN_D, GDN_D), lambda b, t: (b, 0, 0, 0))
    return pl.pallas_call(
        functools.partial(_gdn_kernel, bb=bb, nc=nc, L=L),
        grid=grid,
        in_specs=[tile(D_QKV), tile(LANE), tile(D_GDN), st,
                  pl.BlockSpec((1, GDN_D), lambda b, t: (0, 0))],
        out_specs=(tile(D_GDN), st),
        out_shape=(jax.ShapeDtypeStruct((B, T, D_GDN), f32),
                   jax.ShapeDtypeStruct(s0.shape, f32)),
        compiler_params=pltpu.CompilerParams(dimension_semantics=("arbitrary", "arbitrary"),
                                             vmem_limit_bytes=VMEM_LIMIT),
        name="gdn",
    )(qkv, gates, gz, s0, norm_w)


def _mlstm_kernel(m4_ref, gates_ref, c0_ref, n0_ref, m0_ref, nw_ref,
                  yc_ref, c_ref, n_ref, m_ref, *, bb, nc, L):
    t = pl.program_id(1)

    @pl.when(t == 0)
    def _():
        c_ref[...] = c0_ref[...]
        n_ref[...] = n0_ref[...]
        m_ref[...] = m0_ref[...]

    H = ML_HEADS
    causal, _ = _chunk_masks(L)
    tril = causal.astype(bf16)

    def heads_of(col0):
        return jnp.stack([m4_ref[bi, ci * L:(ci + 1) * L, col0 + h * ML_DH:col0 + (h + 1) * ML_DH]
                          for bi in range(bb) for ci in range(nc) for h in range(H)])

    gt, cs, col, row = _gate_columns(gates_ref, tril, bb=bb, nc=nc, L=L, heads=H)
    cs_t = jnp.swapaxes(cs, 1, 2)
    gt_t = jnp.swapaxes(gt, 1, 2)
    F = col(cs, F_LANE)
    f_row = row(cs_t, F_LANE)
    f_last = F[:, L - 1:L, :]
    ig = col(gt, I_LANE)
    ig_row = row(gt_t, I_LANE)
    q = heads_of(0)
    k = heads_of(D_ML)
    v = heads_of(2 * D_ML)
    og = heads_of(3 * D_ML)
    Dm = jnp.where(causal, F - f_row + ig_row, -jnp.inf)
    d_max = jnp.max(Dm, -1, keepdims=True)
    s_raw = _bdot_nt(q, k)

    def chunk(x, ci):
        return x.reshape((bb, nc, H) + x.shape[1:])[:, ci].reshape((bb * H,) + x.shape[1:])

    C = c_ref[...].reshape(bb * H, ML_DH, ML_DH)
    n = jnp.stack([n_ref[:, h:h + 1, :] for h in range(H)], axis=1).reshape(bb * H, 1, ML_DH)
    m = jnp.stack([m_ref[:, :, h:h + 1] for h in range(H)], axis=1).reshape(bb * H, 1, 1)
    nw = jnp.stack([nw_ref[0:1, h * ML_DH:(h + 1) * ML_DH] for h in range(H)])
    nw = jnp.broadcast_to(nw[None], (bb, H, 1, ML_DH)).reshape(bb * H, 1, ML_DH)
    for ci in range(nc):
        F_c, q_c, k_c, v_c = chunk(F, ci), chunk(q, ci), chunk(k, ci), chunk(v, ci)
        inter = F_c + m
        m_i = jnp.maximum(inter, chunk(d_max, ci))
        wD = jnp.exp(chunk(Dm, ci) - m_i)
        wI = jnp.exp(inter - m_i)
        s = chunk(s_raw, ci) * wD
        num = wI * _bdot(q_c, C) + _bdot(s, v_c)
        qn = wI * jnp.sum(q_c * n, -1, keepdims=True) + jnp.sum(s, -1, keepdims=True)
        hh = num / jnp.maximum(jnp.abs(qn), jnp.exp(-m_i))
        m_new = m_i[:, L - 1:L, :]
        fl = chunk(f_last, ci)
        wk = jnp.exp(fl - F_c + chunk(ig, ci) - m_new)
        dc = jnp.exp(fl + m - m_new)
        kw = k_c * wk
        C = dc * C + _bdot_tn(kw, v_c)
        n = dc * n + jnp.sum(kw, 1, keepdims=True)
        m = m_new

        hh = chunk(og, ci) * hh
        mu = jnp.mean(hh, -1, keepdims=True)
        hc = hh - mu
        var = jnp.mean(hc * hc, -1, keepdims=True)
        y = (hc * lax.rsqrt(var + LN_EPS) * nw).reshape(bb, H, L, ML_DH)
        for bi in range(bb):
            for h in range(H):
                yc_ref[bi, ci * L:(ci + 1) * L, h * ML_DH:(h + 1) * ML_DH] = y[bi, h]
    c_ref[...] = C.reshape(bb, H, ML_DH, ML_DH)
    n4 = n.reshape(bb, H, 1, ML_DH)
    m4 = m.reshape(bb, H, 1, 1)
    for h in range(H):
        n_ref[:, h:h + 1, :] = n4[:, h]
        m_ref[:, :, h:h + 1] = m4[:, h]


def _mlstm_call(m4, gates, c0, n0, m0, norm_w, *, bb, nc, L):
    B, T, _ = m4.shape
    tt = nc * L
    grid = (B // bb, T // tt)
    tile = lambda c: pl.BlockSpec((bb, tt, c), lambda b, t: (b, t, 0))
    cst = pl.BlockSpec((bb, ML_HEADS, ML_DH, ML_DH), lambda b, t: (b, 0, 0, 0))
    nst = pl.BlockSpec((bb, ML_HEADS, ML_DH), lambda b, t: (b, 0, 0))
    mst = pl.BlockSpec((bb, 1, ML_HEADS), lambda b, t: (b, 0, 0))
    return pl.pallas_call(
        functools.partial(_mlstm_kernel, bb=bb, nc=nc, L=L),
        grid=grid,
        in_specs=[tile(4 * D_ML), tile(LANE), cst, nst, mst,
                  pl.BlockSpec((1, D_ML), lambda b, t: (0, 0))],
        out_specs=(tile(D_ML), cst, nst, mst),
        out_shape=(jax.ShapeDtypeStruct((B, T, D_ML), f32),
                   jax.ShapeDtypeStruct(c0.shape, f32),
                   jax.ShapeDtypeStruct(n0.shape, f32),
                   jax.ShapeDtypeStruct(m0.shape, f32)),
        compiler_params=pltpu.CompilerParams(dimension_semantics=("arbitrary", "arbitrary"),
                                             vmem_limit_bytes=VMEM_LIMIT),
        name="mlstm",
    )(m4, gates, c0, n0, m0, norm_w)


def _out_ffn_kernel(x_ref, ya_ref, yb_ref, yc_ref, hf_ref, wo_ref, ln1_ref, wup_ref, fcw_ref,
                    wdn_ref, ln2_ref, xo_ref, tf_ref, sf_ref, *, bb, tt):
    t = pl.program_id(1)

    @pl.when(t == 0)
    def _():
        sf_ref[:, 0:HIST, :] = hf_ref[...]

    rows = bb * tt
    x = x_ref[...].reshape(rows, D_MODEL)
    mix = (jnp.dot(ya_ref[...].reshape(rows, D_CONV).astype(bf16), wo_ref[0:D_CONV, :],
                   preferred_element_type=f32)
           + jnp.dot(yb_ref[...].reshape(rows, D_GDN).astype(bf16), wo_ref[D_CONV:D_CONV + D_GDN, :],
                     preferred_element_type=f32)
           + jnp.dot(yc_ref[...].reshape(rows, D_ML).astype(bf16), wo_ref[D_CONV + D_GDN:, :],
                     preferred_element_type=f32))
    x1 = _layer_norm(ALPHA * x + mix, ln1_ref[0:1, :], ln1_ref[1:2, :])
    x1b = x1.astype(bf16)

    half = D_FF // 2
    acc = None
    for ch in range(2):
        lo = ch * half
        gate = jnp.dot(x1b, wup_ref[:, lo:lo + half], preferred_element_type=f32)
        val = jnp.dot(x1b, wup_ref[:, D_FF + lo:D_FF + lo + half], preferred_element_type=f32)
        sf_ref[:, HIST:, lo:lo + half] = gate.reshape(bb, tt, half)
        conv = fcw_ref[0:1, lo:lo + half] * sf_ref[:, pl.ds(HIST - 2, tt), lo:lo + half]
        for j in range(1, 3):
            conv = conv + fcw_ref[j:j + 1, lo:lo + half] * sf_ref[:, pl.ds(HIST - 2 + j, tt), lo:lo + half]
        hmid = jax.nn.silu(conv).reshape(rows, half) * val
        part = jnp.dot(hmid.astype(bf16), wdn_ref[lo:lo + half, :], preferred_element_type=f32)
        acc = part if acc is None else acc + part
    tail = sf_ref[:, tt:tt + HIST, :]
    tf_ref[...] = tail
    sf_ref[:, 0:HIST, :] = tail
    x2 = _layer_norm(ALPHA * x1 + acc, ln2_ref[0:1, :], ln2_ref[1:2, :])
    xo_ref[...] = x2.reshape(bb, tt, D_MODEL)


def _out_ffn_call(x, ya, yb, yc, hist_f, w_o, ln1, w_up, ffn_conv_w, w_down, ln2, *, bb, tt):
    B, T, _ = x.shape
    grid = (B // bb, T // tt)
    tile = lambda c: pl.BlockSpec((bb, tt, c), lambda b, t: (b, t, 0))
    head = lambda c: pl.BlockSpec((bb, HIST, c), lambda b, t: (b, 0, 0))
    full = lambda a: pl.BlockSpec(a.shape, lambda b, t: (0,) * a.ndim,
                                  pipeline_mode=pl.Buffered(1))
    return pl.pallas_call(
        functools.partial(_out_ffn_kernel, bb=bb, tt=tt),
        grid=grid,
        in_specs=[tile(D_MODEL), tile(D_CONV), tile(D_GDN), tile(D_ML), head(D_FF),
                  full(w_o), full(ln1), full(w_up), full(ffn_conv_w), full(w_down), full(ln2)],
        out_specs=(tile(D_MODEL), head(D_FF)),
        out_shape=(jax.ShapeDtypeStruct((B, T, D_MODEL), f32),
                   jax.ShapeDtypeStruct((B, HIST, D_FF), f32)),
        scratch_shapes=[pltpu.VMEM((bb, tt + HIST, D_FF), f32)],
        compiler_params=pltpu.CompilerParams(dimension_semantics=("arbitrary", "arbitrary"),
                                             vmem_limit_bytes=VMEM_LIMIT),
        name="out_ffn",
    )(x, ya, yb, yc, hist_f, w_o, ln1, w_up, ffn_conv_w, w_down, ln2)


def _pad_hist(buf):
    return jnp.pad(buf, ((0, 0), (HIST - buf.shape[1], 0), (0, 0)))


def _prep_layer(l, w_in, conv_w, gdn_conv_w, gdn_a_log, gdn_dt_bias, gdn_norm_w,
                ml_i_bias, ml_f_bias, ml_norm_w, w_o, ln1_g, ln1_b,
                w_up, ffn_conv_w, w_down, ln2_g, ln2_b):
    w = w_in[l]
    o = 0
    seg = {}
    for name, n in (("a", 3 * D_CONV), ("g", 4 * D_GDN), ("ga", GDN_HEADS), ("gb", GDN_HEADS),
                    ("m", 4 * D_ML), ("mi", ML_HEADS), ("mf", ML_HEADS)):
        seg[name] = w[:, o:o + n]
        o += n
    gate_cols = jnp.concatenate([seg["ga"], seg["gb"], seg["mi"], seg["mf"]], axis=1)
    gate_cols = jnp.pad(gate_cols, ((0, 0), (0, LANE - gate_cols.shape[1])))
    w_packed = jnp.concatenate([seg["a"], seg["g"], seg["m"], gate_cols], axis=1).astype(bf16)
    zeros4 = jnp.zeros((4,), f32)
    bias_row = jnp.pad(jnp.concatenate([gdn_dt_bias[l], zeros4, ml_i_bias[l], ml_f_bias[l]]), (0, LANE - 16))
    alog_row = jnp.pad(gdn_a_log[l], (0, LANE - GDN_HEADS))
    gate_p = jnp.zeros((8, LANE), f32).at[0].set(bias_row).at[1].set(alog_row)
    return dict(
        w_in=w_packed, conv_w=conv_w[l], gdn_conv_w=gdn_conv_w[l], gate_p=gate_p,
        gdn_norm_w=gdn_norm_w[l].reshape(1, GDN_D), ml_norm_w=ml_norm_w[l].reshape(1, D_ML),
        w_o=w_o[l].astype(bf16), ln1=jnp.stack([ln1_g[l], ln1_b[l]]),
        w_up=w_up[l].astype(bf16), ffn_conv_w=ffn_conv_w[l], w_down=w_down[l].astype(bf16),
        ln2=jnp.stack([ln2_g[l], ln2_b[l]]),
    )


def _layer(x, conv_buf, gdn_buf, gdn_s, ml_c, ml_n, ml_m, ffn_buf, p, *, bb, tt, nc, L, rbb):
    B = x.shape[0]
    ya, qkv, gz, m4, gates, tail_a, tail_g = _proj_call(
        x, _pad_hist(conv_buf), _pad_hist(gdn_buf), p["w_in"], p["conv_w"], p["gdn_conv_w"],
        p["gate_p"], bb=bb, tt=tt)
    yb, s_new = _gdn_call(qkv, gates, gz, gdn_s, p["gdn_norm_w"], bb=rbb, nc=nc, L=L)
    yc, c_new, n_new, m_new = _mlstm_call(m4, gates, ml_c, ml_n, ml_m.reshape(B, 1, ML_HEADS),
                                          p["ml_norm_w"], bb=rbb, nc=nc, L=L)
    x_out, tail_f = _out_ffn_call(x, ya, yb, yc, _pad_hist(ffn_buf), p["w_o"], p["ln1"], p["w_up"],
                                  p["ffn_conv_w"], p["w_down"], p["ln2"], bb=bb, tt=tt)
    return x_out, (tail_a[:, HIST - 2:], tail_g[:, HIST - 3:], s_new, c_new, n_new,
                   m_new.reshape(B, ML_HEADS), tail_f[:, HIST - 2:])


def _trunk(x, states, params, **cfg):
    new = []
    for l in range(DEPTH):
        x, st = _layer(x, *[s[l] for s in states], params[l], **cfg)
        new.append(st)
    return x, [jnp.stack([s[i] for s in new]) for i in range(len(new[0]))]


def kernel(x_prompt, x_sample, state_conv_mix, state_gdn_conv, state_gdn, state_mlstm_c,
           state_mlstm_n, state_mlstm_m, state_ffn_conv,
           w_in, conv_w, gdn_conv_w, gdn_a_log, gdn_dt_bias, gdn_norm_w,
           ml_i_bias, ml_f_bias, ml_norm_w, w_o, ln1_g, ln1_b,
           w_up, ffn_conv_w, w_down, ln2_g, ln2_b):
    weights = (w_in, conv_w, gdn_conv_w, gdn_a_log, gdn_dt_bias, gdn_norm_w,
               ml_i_bias, ml_f_bias, ml_norm_w, w_o, ln1_g, ln1_b,
               w_up, ffn_conv_w, w_down, ln2_g, ln2_b)
    params = [_prep_layer(l, *weights) for l in range(DEPTH)]
    Bp, Tp, _ = x_prompt.shape
    Bs, Ts, _ = x_sample.shape
    zeros = lambda *s: jnp.zeros((DEPTH, Bp) + s, f32)
    p_states = (zeros(2, D_CONV), zeros(3, D_QKV), zeros(GDN_HEADS, GDN_D, GDN_D),
                zeros(ML_HEADS, ML_DH, ML_DH), zeros(ML_HEADS, ML_DH), zeros(ML_HEADS),
                zeros(2, D_FF))
    Lp = min(CHUNK, Tp)
    y_prompt, p_new = _trunk(x_prompt, p_states, params, bb=1, tt=256, nc=1, L=Lp, rbb=Bp)
    s_states = (state_conv_mix, state_gdn_conv, state_gdn, state_mlstm_c, state_mlstm_n,
                state_mlstm_m, state_ffn_conv)
    Ls = min(CHUNK, Ts)
    y_sample, s_new = _trunk(x_sample, s_states, params, bb=32, tt=Ts, nc=Ts // Ls, L=Ls, rbb=16)
    return (y_prompt, y_sample, *p_new, *s_new)
```

```python
import functools

import jax
import jax.numpy as jnp
from jax import lax
from jax.experimental import pallas as pl
from jax.experimental.pallas import tpu as pltpu

f32 = jnp.float32
bf16 = jnp.bfloat16

D_MODEL = 1024
DEPTH = 2
D_CONV = 256
GDN_HEADS = 4
GDN_D = 128
D_GDN = GDN_HEADS * GDN_D
D_QKV = 3 * D_GDN
ML_HEADS = 4
ML_DH = 64
D_ML = ML_HEADS * ML_DH
D_FF = 2816
CHUNK = 64
ALPHA = (2.0 * DEPTH) ** 0.25
LN_EPS = 1e-5
NORM_EPS = 1e-6

LANE = 128
HIST = 8
COL_A = 0
COL_G = 3 * D_CONV
COL_M = COL_G + 4 * D_GDN
COL_GATE = COL_M + 4 * D_ML
D_IN_PACKED = COL_GATE + LANE
G_LANE, B_LANE, I_LANE, F_LANE = 0, 4, 8, 12
VMEM_LIMIT = 56 * 1024 * 1024
_PARAMS = pltpu.CompilerParams(dimension_semantics=("arbitrary", "arbitrary"),
                               vmem_limit_bytes=VMEM_LIMIT)


def _softplus(z):
    return jnp.maximum(z, 0.0) + jnp.log1p(jnp.exp(-jnp.abs(z)))


def _layer_norm(x, g, b):
    mu = jnp.mean(x, -1, keepdims=True)
    xc = x - mu
    var = jnp.mean(xc * xc, -1, keepdims=True)
    return xc * lax.rsqrt(var + LN_EPS) * g + b


def _tile_spec(bb, tt, c):
    return pl.BlockSpec((bb, tt, c), lambda b, t: (b, t, 0))


def _state_spec(l, bb, tail):
    return pl.BlockSpec((None, bb) + tail, lambda b, t: (l, b) + (0,) * len(tail))


def _weight_spec(l, tail):
    return pl.BlockSpec((None,) + tail, lambda b, t: (l,) + (0,) * len(tail),
                        pipeline_mode=pl.Buffered(1))


def _stacked_call(kernel, *, l, grid, inputs, in_specs, states_in, state_specs, prev_states,
                  out_shape, out_specs, n_tile_out, scratch_shapes, name):
    args = list(inputs)
    specs = list(in_specs)
    if states_in is not None:
        args += list(states_in)
        specs += list(state_specs)
    aliases = {}
    if prev_states is not None:
        for i, p in enumerate(prev_states):
            aliases[len(args)] = n_tile_out + i
            args.append(p)
            specs.append(pl.BlockSpec(memory_space=pl.ANY))
    return pl.pallas_call(
        kernel, grid=grid, in_specs=specs, out_specs=out_specs, out_shape=out_shape,
        scratch_shapes=scratch_shapes, input_output_aliases=aliases,
        compiler_params=_PARAMS, name=name)(*args)


def _proj_kernel(*refs, bb, tt, has_state, n_prev):
    x_ref, w_ref, cw_ref, gcw_ref, gp_ref = refs[:5]
    refs = refs[5:]
    if has_state:
        ha_ref, hg_ref = refs[:2]
        refs = refs[2:]
    refs = refs[n_prev:]
    ya_ref, qkv_ref, gz_ref, m4_ref, gates_ref, ta_ref, tg_ref, sa_ref, sg_ref = refs
    t = pl.program_id(1)

    @pl.when(t == 0)
    def _():
        if has_state:
            sa_ref[:, HIST - 2:HIST, :] = ha_ref[...]
            sg_ref[:, HIST - 3:HIST, :] = hg_ref[...]
        else:
            sa_ref[:, 0:HIST, :] = jnp.zeros((bb, HIST, D_CONV), f32)
            sg_ref[:, 0:HIST, :] = jnp.zeros((bb, HIST, D_QKV), f32)

    rows = bb * tt
    xb = x_ref[...].reshape(rows, D_MODEL).astype(bf16)

    def proj(lo, hi):
        return jnp.dot(xb, w_ref[:, lo:hi], preferred_element_type=f32)

    pa = proj(COL_A, COL_G)
    a_b = pa[:, 0:D_CONV]
    sa_ref[:, HIST:, :] = (pa[:, D_CONV:2 * D_CONV] * pa[:, 2 * D_CONV:3 * D_CONV]).reshape(bb, tt, D_CONV)
    conv = cw_ref[0:1, :] * sa_ref[:, pl.ds(HIST - 2, tt), :]
    for j in range(1, 3):
        conv = conv + cw_ref[j:j + 1, :] * sa_ref[:, pl.ds(HIST - 2 + j, tt), :]
    ya_ref[...] = a_b.reshape(bb, tt, D_CONV) * conv
    ta_ref[...] = sa_ref[:, tt + HIST - 2:tt + HIST, :]
    sa_ref[:, 0:HIST, :] = sa_ref[:, tt:tt + HIST, :]

    pg = proj(COL_G, COL_M)
    sg_ref[:, HIST:, :] = pg[:, 0:D_QKV].reshape(bb, tt, D_QKV)
    gz_ref[...] = jax.nn.silu(pg[:, D_QKV:]).reshape(bb, tt, D_GDN)
    for cb in range(D_QKV // LANE):
        cs = slice(cb * LANE, (cb + 1) * LANE)
        c = gcw_ref[0:1, cs] * sg_ref[:, pl.ds(HIST - 3, tt), cs]
        for j in range(1, 4):
            c = c + gcw_ref[j:j + 1, cs] * sg_ref[:, pl.ds(HIST - 3 + j, tt), cs]
        c = jax.nn.silu(c)
        if cb < 2 * GDN_HEADS:
            c = c * lax.rsqrt(jnp.sum(c * c, -1, keepdims=True) + NORM_EPS)
            if cb < GDN_HEADS:
                c = c * (GDN_D ** -0.5)
        qkv_ref[:, :, cs] = c
    tg_ref[...] = sg_ref[:, tt + HIST - 3:tt + HIST, :]
    sg_ref[:, 0:HIST, :] = sg_ref[:, tt:tt + HIST, :]

    pm = proj(COL_M, COL_GATE)
    m4_ref[:, :, 0:D_ML] = pm[:, 0:D_ML].reshape(bb, tt, D_ML)
    m4_ref[:, :, D_ML:2 * D_ML] = (pm[:, D_ML:2 * D_ML] * (ML_DH ** -0.5)).reshape(bb, tt, D_ML)
    m4_ref[:, :, 2 * D_ML:3 * D_ML] = pm[:, 2 * D_ML:3 * D_ML].reshape(bb, tt, D_ML)
    m4_ref[:, :, 3 * D_ML:4 * D_ML] = jax.nn.sigmoid(pm[:, 3 * D_ML:4 * D_ML]).reshape(bb, tt, D_ML)

    z = proj(COL_GATE, D_IN_PACKED) + gp_ref[0:1, :]
    lane = lax.broadcasted_iota(jnp.int32, z.shape, 1)
    g = -jnp.exp(gp_ref[1:2, :]) * _softplus(z)
    gates = jnp.where(lane < B_LANE, g,
                      jnp.where(lane < I_LANE, jax.nn.sigmoid(z),
                                jnp.where(lane < F_LANE, z,
                                          jnp.where(lane < F_LANE + ML_HEADS, -_softplus(-z), 0.0))))
    gates_ref[...] = gates.reshape(bb, tt, LANE)


def _proj_call(l, x, p, states_in, prev_states, *, bb, tt):
    B, T, _ = x.shape
    tile = functools.partial(_tile_spec, bb, tt)
    st_a, st_g = (2, D_CONV), (3, D_QKV)
    return _stacked_call(
        functools.partial(_proj_kernel, bb=bb, tt=tt, has_state=states_in is not None,
                          n_prev=0 if prev_states is None else 2),
        l=l, grid=(B // bb, T // tt),
        inputs=[x, p["w_in"], p["conv_w"], p["gdn_conv_w"], p["gate_p"]],
        in_specs=[tile(D_MODEL), _weight_spec(l, (D_MODEL, D_IN_PACKED)), _weight_spec(l, (3, D_CONV)),
                  _weight_spec(l, (4, D_QKV)), _weight_spec(l, (8, LANE))],
        states_in=states_in, state_specs=[_state_spec(l, bb, st_a), _state_spec(l, bb, st_g)],
        prev_states=prev_states,
        out_shape=(jax.ShapeDtypeStruct((B, T, D_CONV), f32),
                   jax.ShapeDtypeStruct((B, T, D_QKV), f32),
                   jax.ShapeDtypeStruct((B, T, D_GDN), f32),
                   jax.ShapeDtypeStruct((B, T, 4 * D_ML), f32),
                   jax.ShapeDtypeStruct((B, T, LANE), f32),
                   jax.ShapeDtypeStruct((DEPTH, B) + st_a, f32),
                   jax.ShapeDtypeStruct((DEPTH, B) + st_g, f32)),
        out_specs=(tile(D_CONV), tile(D_QKV), tile(D_GDN), tile(4 * D_ML), tile(LANE),
                   _state_spec(l, bb, st_a), _state_spec(l, bb, st_g)),
        n_tile_out=5,
        scratch_shapes=[pltpu.VMEM((bb, tt + HIST, D_CONV), f32),
                        pltpu.VMEM((bb, tt + HIST, D_QKV), f32)],
        name="in_proj")


def _bdot(a, b):
    return jnp.einsum('pik,pkj->pij', a.astype(bf16), b.astype(bf16), preferred_element_type=f32)


def _bdot_nt(a, b):
    return jnp.einsum('pik,pjk->pij', a.astype(bf16), b.astype(bf16), preferred_element_type=f32)


def _bdot_tn(a, b):
    return jnp.einsum('pki,pkj->pij', a.astype(bf16), b.astype(bf16), preferred_element_type=f32)


def _chunk_masks(L):
    r = lax.broadcasted_iota(jnp.int32, (L, L), 0)
    c = lax.broadcasted_iota(jnp.int32, (L, L), 1)
    return c <= r, c < r


def _gate_columns(gates_ref, tril, *, bb, nc, L, heads):
    gt = jnp.stack([gates_ref[bi, ci * L:(ci + 1) * L, :] for bi in range(bb) for ci in range(nc)])
    hi = gt.astype(bf16)
    r1 = gt - hi.astype(f32)
    mid = r1.astype(bf16)
    lo = (r1 - mid.astype(f32)).astype(bf16)
    trb = jnp.broadcast_to(tril, (bb * nc, L, L))
    cs = (jnp.einsum('pij,pjk->pik', trb, hi, preferred_element_type=f32)
          + jnp.einsum('pij,pjk->pik', trb, mid, preferred_element_type=f32)
          + jnp.einsum('pij,pjk->pik', trb, lo, preferred_element_type=f32))

    def col(x, lane0):
        return jnp.stack([x[:, :, lane0 + h:lane0 + h + 1] for h in range(heads)],
                         axis=1).reshape(bb * nc * heads, L, 1)

    def row(xt, lane0):
        return jnp.stack([xt[:, lane0 + h:lane0 + h + 1, :] for h in range(heads)],
                         axis=1).reshape(bb * nc * heads, 1, L)

    return gt, cs, col, row


def _gdn_kernel(*refs, bb, nc, L, has_state, n_prev):
    qkv_ref, gates_ref, gz_ref, nw_ref = refs[:4]
    refs = refs[4:]
    if has_state:
        s0_ref = refs[0]
        refs = refs[1:]
    yb_ref, s_ref = refs[n_prev:]
    t = pl.program_id(1)

    @pl.when(t == 0)
    def _():
        s_ref[...] = s0_ref[...] if has_state else jnp.zeros(s_ref.shape, f32)

    H = GDN_HEADS
    causal, strict = _chunk_masks(L)
    tril = causal.astype(bf16)
    steps = L.bit_length() - 2

    def heads_of(ref, col0):
        return jnp.stack([ref[bi, ci * L:(ci + 1) * L, col0 + h * GDN_D:col0 + (h + 1) * GDN_D]
                          for bi in range(bb) for ci in range(nc) for h in range(H)])

    gt, cs, col, row = _gate_columns(gates_ref, tril, bb=bb, nc=nc, L=L, heads=H)
    cs_t = jnp.swapaxes(cs, 1, 2)
    G = col(cs, G_LANE)
    g_row = row(cs_t, G_LANE)
    g_last = G[:, L - 1:L, :]
    beta = col(gt, B_LANE)
    q = heads_of(qkv_ref, 0)
    k = heads_of(qkv_ref, D_GDN)
    v = heads_of(qkv_ref, 2 * D_GDN)

    decay = jnp.where(causal, jnp.exp(jnp.where(causal, G - g_row, 0.0)), 0.0)
    kb = k * beta
    kk = _bdot_nt(kb, k) * decay
    qk = _bdot_nt(q, k) * decay
    P = jnp.where(strict, kk, 0.0)
    R = -P
    for _ in range(steps):
        P = _bdot(P, P)
        R = R + P + _bdot(R, P)
    eG = jnp.exp(G)
    rhs = jnp.concatenate([v * beta, kb * eG], axis=-1)
    w = rhs + _bdot(R, rhs)
    qg = q * eG
    kdec = k * jnp.exp(g_last - G)
    e_last = jnp.exp(g_last)

    def chunk(x, ci):
        return x.reshape((bb, nc, H) + x.shape[1:])[:, ci].reshape((bb * H,) + x.shape[1:])

    nw = nw_ref[...]
    S = s_ref[...].reshape(bb * H, GDN_D, GDN_D)
    for ci in range(nc):
        rs = slice(ci * L, (ci + 1) * L)
        w_c = chunk(w, ci)
        xs = _bdot(jnp.concatenate([w_c[:, :, GDN_D:], chunk(qg, ci)], axis=1), S)
        u = w_c[:, :, 0:GDN_D] - xs[:, 0:L]
        o = xs[:, L:] + _bdot(chunk(qk, ci), u)
        S = S * chunk(e_last, ci) + _bdot_tn(chunk(kdec, ci), u)
        o = o * lax.rsqrt(jnp.mean(o * o, -1, keepdims=True) + NORM_EPS) * nw
        o = o.reshape(bb, H, L, GDN_D)
        for bi in range(bb):
            for h in range(H):
                hs = slice(h * GDN_D, (h + 1) * GDN_D)
                yb_ref[bi, rs, hs] = o[bi, h] * gz_ref[bi, rs, hs]
    s_ref[...] = S.reshape(bb, H, GDN_D, GDN_D)


def _gdn_call(l, qkv, gates, gz, p, states_in, prev_states, *, bb, nc, L):
    B, T, _ = qkv.shape
    tt = nc * L
    tile = functools.partial(_tile_spec, bb, tt)
    st = (GDN_HEADS, GDN_D, GDN_D)
    return _stacked_call(
        functools.partial(_gdn_kernel, bb=bb, nc=nc, L=L, has_state=states_in is not None,
                          n_prev=0 if prev_states is None else 1),
        l=l, grid=(B // bb, T // tt),
        inputs=[qkv, gates, gz, p["gdn_norm_w"]],
        in_specs=[tile(D_QKV), tile(LANE), tile(D_GDN), _weight_spec(l, (1, GDN_D))],
        states_in=states_in, state_specs=[_state_spec(l, bb, st)], prev_states=prev_states,
        out_shape=(jax.ShapeDtypeStruct((B, T, D_GDN), f32),
                   jax.ShapeDtypeStruct((DEPTH, B) + st, f32)),
        out_specs=(tile(D_GDN), _state_spec(l, bb, st)),
        n_tile_out=1, scratch_shapes=[], name="gdn")


def _mlstm_kernel(*refs, bb, nc, L, has_state, n_prev):
    m4_ref, gates_ref, nw_ref = refs[:3]
    refs = refs[3:]
    if has_state:
        c0_ref, n0_ref, m0_ref = refs[:3]
        refs = refs[3:]
    yc_ref, c_ref, n_ref, m_ref = refs[n_prev:]
    t = pl.program_id(1)

    @pl.when(t == 0)
    def _():
        if has_state:
            c_ref[...] = c0_ref[...]
            n_ref[...] = n0_ref[...]
            m_ref[...] = m0_ref[...]
        else:
            c_ref[...] = jnp.zeros(c_ref.shape, f32)
            n_ref[...] = jnp.zeros(n_ref.shape, f32)
            m_ref[...] = jnp.zeros(m_ref.shape, f32)

    H = ML_HEADS
    causal, _ = _chunk_masks(L)
    tril = causal.astype(bf16)

    def heads_of(col0):
        return jnp.stack([m4_ref[bi, ci * L:(ci + 1) * L, col0 + h * ML_DH:col0 + (h + 1) * ML_DH]
                          for bi in range(bb) for ci in range(nc) for h in range(H)])

    gt, cs, col, row = _gate_columns(gates_ref, tril, bb=bb, nc=nc, L=L, heads=H)
    cs_t = jnp.swapaxes(cs, 1, 2)
    gt_t = jnp.swapaxes(gt, 1, 2)
    F = col(cs, F_LANE)
    f_row = row(cs_t, F_LANE)
    f_last = F[:, L - 1:L, :]
    ig = col(gt, I_LANE)
    ig_row = row(gt_t, I_LANE)
    q = heads_of(0)
    k = heads_of(D_ML)
    v = heads_of(2 * D_ML)
    og = heads_of(3 * D_ML)
    Dm = jnp.where(causal, F - f_row + ig_row, -jnp.inf)
    d_max = jnp.max(Dm, -1, keepdims=True)
    s_raw = _bdot_nt(q, k)

    def chunk(x, ci):
        return x.reshape((bb, nc, H) + x.shape[1:])[:, ci].reshape((bb * H,) + x.shape[1:])

    C = c_ref[...].reshape(bb * H, ML_DH, ML_DH)
    n = jnp.stack([n_ref[:, h:h + 1, :] for h in range(H)], axis=1).reshape(bb * H, 1, ML_DH)
    m = jnp.stack([m_ref[:, :, h:h + 1] for h in range(H)], axis=1).reshape(bb * H, 1, 1)
    nw = jnp.stack([nw_ref[0:1, h * ML_DH:(h + 1) * ML_DH] for h in range(H)])
    nw = jnp.broadcast_to(nw[None], (bb, H, 1, ML_DH)).reshape(bb * H, 1, ML_DH)
    for ci in range(nc):
        F_c, q_c, k_c, v_c = chunk(F, ci), chunk(q, ci), chunk(k, ci), chunk(v, ci)
        inter = F_c + m
        m_i = jnp.maximum(inter, chunk(d_max, ci))
        wD = jnp.exp(chunk(Dm, ci) - m_i)
        wI = jnp.exp(inter - m_i)
        s = chunk(s_raw, ci) * wD
        num = wI * _bdot(q_c, C) + _bdot(s, v_c)
        qn = wI * jnp.sum(q_c * n, -1, keepdims=True) + jnp.sum(s, -1, keepdims=True)
        hh = num / jnp.maximum(jnp.abs(qn), jnp.exp(-m_i))
        m_new = m_i[:, L - 1:L, :]
        fl = chunk(f_last, ci)
        wk = jnp.exp(fl - F_c + chunk(ig, ci) - m_new)
        dc = jnp.exp(fl + m - m_new)
        kw = k_c * wk
        C = dc * C + _bdot_tn(kw, v_c)
        n = dc * n + jnp.sum(kw, 1, keepdims=True)
        m = m_new

        hh = chunk(og, ci) * hh
        mu = jnp.mean(hh, -1, keepdims=True)
        hc = hh - mu
        var = jnp.mean(hc * hc, -1, keepdims=True)
        y = (hc * lax.rsqrt(var + LN_EPS) * nw).reshape(bb, H, L, ML_DH)
        for bi in range(bb):
            for h in range(H):
                yc_ref[bi, ci * L:(ci + 1) * L, h * ML_DH:(h + 1) * ML_DH] = y[bi, h]
    c_ref[...] = C.reshape(bb, H, ML_DH, ML_DH)
    n4 = n.reshape(bb, H, 1, ML_DH)
    m4 = m.reshape(bb, H, 1, 1)
    for h in range(H):
        n_ref[:, h:h + 1, :] = n4[:, h]
        m_ref[:, :, h:h + 1] = m4[:, h]


def _mlstm_call(l, m4, gates, p, states_in, prev_states, *, bb, nc, L):
    B, T, _ = m4.shape
    tt = nc * L
    tile = functools.partial(_tile_spec, bb, tt)
    st_c, st_n, st_m = (ML_HEADS, ML_DH, ML_DH), (ML_HEADS, ML_DH), (1, ML_HEADS)
    return _stacked_call(
        functools.partial(_mlstm_kernel, bb=bb, nc=nc, L=L, has_state=states_in is not None,
                          n_prev=0 if prev_states is None else 3),
        l=l, grid=(B // bb, T // tt),
        inputs=[m4, gates, p["ml_norm_w"]],
        in_specs=[tile(4 * D_ML), tile(LANE), _weight_spec(l, (1, D_ML))],
        states_in=states_in,
        state_specs=[_state_spec(l, bb, st_c), _state_spec(l, bb, st_n), _state_spec(l, bb, st_m)],
        prev_states=prev_states,
        out_shape=(jax.ShapeDtypeStruct((B, T, D_ML), f32),
                   jax.ShapeDtypeStruct((DEPTH, B) + st_c, f32),
                   jax.ShapeDtypeStruct((DEPTH, B) + st_n, f32),
                   jax.ShapeDtypeStruct((DEPTH, B) + st_m, f32)),
        out_specs=(tile(D_ML), _state_spec(l, bb, st_c), _state_spec(l, bb, st_n),
                   _state_spec(l, bb, st_m)),
        n_tile_out=1, scratch_shapes=[], name="mlstm")


def _out_ffn_kernel(*refs, bb, tt, has_state, n_prev):
    (x_ref, ya_ref, yb_ref, yc_ref, wo_ref, ln1_ref, wup_ref, fcw_ref, wdn_ref, ln2_ref) = refs[:10]
    refs = refs[10:]
    if has_state:
        hf_ref = refs[0]
        refs = refs[1:]
    xo_ref, tf_ref, sf_ref = refs[n_prev:]
    t = pl.program_id(1)

    @pl.when(t == 0)
    def _():
        if has_state:
            sf_ref[:, HIST - 2:HIST, :] = hf_ref[...]
        else:
            sf_ref[:, 0:HIST, :] = jnp.zeros((bb, HIST, D_FF), f32)

    rows = bb * tt
    x = x_ref[...].reshape(rows, D_MODEL)
    mix = (jnp.dot(ya_ref[...].reshape(rows, D_CONV).astype(bf16), wo_ref[0:D_CONV, :],
                   preferred_element_type=f32)
           + jnp.dot(yb_ref[...].reshape(rows, D_GDN).astype(bf16), wo_ref[D_CONV:D_CONV + D_GDN, :],
                     preferred_element_type=f32)
           + jnp.dot(yc_ref[...].reshape(rows, D_ML).astype(bf16), wo_ref[D_CONV + D_GDN:, :],
                     preferred_element_type=f32))
    x1 = _layer_norm(ALPHA * x + mix, ln1_ref[0:1, :], ln1_ref[1:2, :])
    x1b = x1.astype(bf16)

    half = D_FF // 2
    acc = None
    for ch in range(2):
        lo = ch * half
        gate = jnp.dot(x1b, wup_ref[:, lo:lo + half], preferred_element_type=f32)
        val = jnp.dot(x1b, wup_ref[:, D_FF + lo:D_FF + lo + half], preferred_element_type=f32)
        sf_ref[:, HIST:, lo:lo + half] = gate.reshape(bb, tt, half)
        conv = fcw_ref[0:1, lo:lo + half] * sf_ref[:, pl.ds(HIST - 2, tt), lo:lo + half]
        for j in range(1, 3):
            conv = conv + fcw_ref[j:j + 1, lo:lo + half] * sf_ref[:, pl.ds(HIST - 2 + j, tt), lo:lo + half]
        hmid = jax.nn.silu(conv).reshape(rows, half) * val
        part = jnp.dot(hmid.astype(bf16), wdn_ref[lo:lo + half, :], preferred_element_type=f32)
        acc = part if acc is None else acc + part
    tf_ref[...] = sf_ref[:, tt + HIST - 2:tt + HIST, :]
    sf_ref[:, 0:HIST, :] = sf_ref[:, tt:tt + HIST, :]
    x2 = _layer_norm(ALPHA * x1 + acc, ln2_ref[0:1, :], ln2_ref[1:2, :])
    xo_ref[...] = x2.reshape(bb, tt, D_MODEL)


def _out_ffn_call(l, x, ya, yb, yc, p, states_in, prev_states, *, bb, tt):
    B, T, _ = x.shape
    tile = functools.partial(_tile_spec, bb, tt)
    st = (2, D_FF)
    return _stacked_call(
        functools.partial(_out_ffn_kernel, bb=bb, tt=tt, has_state=states_in is not None,
                          n_prev=0 if prev_states is None else 1),
        l=l, grid=(B // bb, T // tt),
        inputs=[x, ya, yb, yc, p["w_o"], p["ln1"], p["w_up"], p["ffn_conv_w"], p["w_down"], p["ln2"]],
        in_specs=[tile(D_MODEL), tile(D_CONV), tile(D_GDN), tile(D_ML),
                  _weight_spec(l, (D_MODEL, D_MODEL)), _weight_spec(l, (2, D_MODEL)),
                  _weight_spec(l, (D_MODEL, 2 * D_FF)), _weight_spec(l, (3, D_FF)),
                  _weight_spec(l, (D_FF, D_MODEL)), _weight_spec(l, (2, D_MODEL))],
        states_in=states_in, state_specs=[_state_spec(l, bb, st)], prev_states=prev_states,
        out_shape=(jax.ShapeDtypeStruct((B, T, D_MODEL), f32),
                   jax.ShapeDtypeStruct((DEPTH, B) + st, f32)),
        out_specs=(tile(D_MODEL), _state_spec(l, bb, st)),
        n_tile_out=1, scratch_shapes=[pltpu.VMEM((bb, tt + HIST, D_FF), f32)], name="out_ffn")


def _prep_params(w_in, conv_w, gdn_conv_w, gdn_a_log, gdn_dt_bias, gdn_norm_w,
                 ml_i_bias, ml_f_bias, ml_norm_w, w_o, ln1_g, ln1_b,
                 w_up, ffn_conv_w, w_down, ln2_g, ln2_b):
    o = 0
    seg = {}
    for name, n in (("a", 3 * D_CONV), ("g", 4 * D_GDN), ("ga", GDN_HEADS), ("gb", GDN_HEADS),
                    ("m", 4 * D_ML), ("mi", ML_HEADS), ("mf", ML_HEADS)):
        seg[name] = w_in[:, :, o:o + n]
        o += n
    gate_cols = jnp.concatenate([seg["ga"], seg["gb"], seg["mi"], seg["mf"]], axis=2)
    gate_cols = jnp.pad(gate_cols, ((0, 0), (0, 0), (0, LANE - gate_cols.shape[2])))
    w_packed = jnp.concatenate([seg["a"], seg["g"], seg["m"], gate_cols], axis=2).astype(bf16)
    zeros4 = jnp.zeros((DEPTH, 4), f32)
    bias_row = jnp.pad(jnp.concatenate([gdn_dt_bias, zeros4, ml_i_bias, ml_f_bias], axis=1),
                       ((0, 0), (0, LANE - 16)))
    alog_row = jnp.pad(gdn_a_log, ((0, 0), (0, LANE - GDN_HEADS)))
    gate_p = jnp.zeros((DEPTH, 8, LANE), f32).at[:, 0].set(bias_row).at[:, 1].set(alog_row)
    return dict(
        w_in=w_packed, conv_w=conv_w, gdn_conv_w=gdn_conv_w, gate_p=gate_p,
        gdn_norm_w=gdn_norm_w.reshape(DEPTH, 1, GDN_D), ml_norm_w=ml_norm_w.reshape(DEPTH, 1, D_ML),
        w_o=w_o.astype(bf16), ln1=jnp.stack([ln1_g, ln1_b], axis=1),
        w_up=w_up.astype(bf16), ffn_conv_w=ffn_conv_w, w_down=w_down.astype(bf16),
        ln2=jnp.stack([ln2_g, ln2_b], axis=1),
    )


def _trunk(x, states, p, *, bb, tt, nc, L, rbb):
    conv_new = gdn_new = ml_new = ffn_new = None
    for l in range(DEPTH):
        sel = lambda idx: None if states is None else [states[i] for i in idx]
        ya, qkv, gz, m4, gates, *conv_new = _proj_call(l, x, p, sel((0, 1)), conv_new, bb=bb, tt=tt)
        yb, *gdn_new = _gdn_call(l, qkv, gates, gz, p, sel((2,)), gdn_new, bb=rbb, nc=nc, L=L)
        yc, *ml_new = _mlstm_call(l, m4, gates, p, sel((3, 4, 5)), ml_new, bb=rbb, nc=nc, L=L)
        x, *ffn_new = _out_ffn_call(l, x, ya, yb, yc, p, sel((6,)), ffn_new, bb=bb, tt=tt)
    B = x.shape[0]
    return x, (conv_new[0], conv_new[1], gdn_new[0], ml_new[0], ml_new[1],
               ml_new[2].reshape(DEPTH, B, ML_HEADS), ffn_new[0])


def kernel(x_prompt, x_sample, state_conv_mix, state_gdn_conv, state_gdn, state_mlstm_c,
           state_mlstm_n, state_mlstm_m, state_ffn_conv,
           w_in, conv_w, gdn_conv_w, gdn_a_log, gdn_dt_bias, gdn_norm_w,
           ml_i_bias, ml_f_bias, ml_norm_w, w_o, ln1_g, ln1_b,
           w_up, ffn_conv_w, w_down, ln2_g, ln2_b):
    p = _prep_params(w_in, conv_w, gdn_conv_w, gdn_a_log, gdn_dt_bias, gdn_norm_w,
                     ml_i_bias, ml_f_bias, ml_norm_w, w_o, ln1_g, ln1_b,
                     w_up, ffn_conv_w, w_down, ln2_g, ln2_b)
    Bp, Tp, _ = x_prompt.shape
    Bs, Ts, _ = x_sample.shape
    Lp = min(CHUNK, Tp)
    y_prompt, p_new = _trunk(x_prompt, None, p, bb=1, tt=256, nc=1, L=Lp, rbb=Bp)
    s_states = (state_conv_mix, state_gdn_conv, state_gdn, state_mlstm_c, state_mlstm_n,
                state_mlstm_m.reshape(DEPTH, Bs, 1, ML_HEADS), state_ffn_conv)
    Ls = min(CHUNK, Ts)
    y_sample, s_new = _trunk(x_sample, s_states, p, bb=32, tt=Ts, nc=Ts // Ls, L=Ls, rbb=16)
    return (y_prompt, y_sample, *p_new, *s_new)
```

```python
import functools

import jax
import jax.numpy as jnp
from jax import lax
from jax.experimental import pallas as pl
from jax.experimental.pallas import tpu as pltpu

f32 = jnp.float32
bf16 = jnp.bfloat16

D_MODEL = 1024
DEPTH = 2
D_CONV = 256
GDN_HEADS = 4
GDN_D = 128
D_GDN = GDN_HEADS * GDN_D
D_QKV = 3 * D_GDN
ML_HEADS = 4
ML_DH = 64
D_ML = ML_HEADS * ML_DH
D_FF = 2816
CHUNK = 64
ALPHA = (2.0 * DEPTH) ** 0.25
LN_EPS = 1e-5
NORM_EPS = 1e-6

LANE = 128
HIST = 8
COL_A = 0
COL_G = 3 * D_CONV
COL_M = COL_G + 4 * D_GDN
COL_GATE = COL_M + 4 * D_ML
D_IN_PACKED = COL_GATE + LANE
G_LANE, B_LANE, I_LANE, F_LANE = 0, 4, 8, 12
MXU_DIM = 256
FF_CHUNKS = ((0, 6 * MXU_DIM), (6 * MXU_DIM, D_FF))
VMEM_LIMIT = 56 * 1024 * 1024
_PARAMS = pltpu.CompilerParams(dimension_semantics=("arbitrary", "arbitrary"),
                               vmem_limit_bytes=VMEM_LIMIT)


def _softplus(z):
    return jnp.maximum(z, 0.0) + jnp.log1p(jnp.exp(-jnp.abs(z)))


def _layer_norm(x, g, b):
    mu = jnp.mean(x, -1, keepdims=True)
    xc = x - mu
    var = jnp.mean(xc * xc, -1, keepdims=True)
    return xc * lax.rsqrt(var + LN_EPS) * g + b


def _tile_spec(bb, tt, c):
    return pl.BlockSpec((bb, tt, c), lambda b, t: (b, t, 0))


def _state_spec(l, bb, tail):
    return pl.BlockSpec((None, bb) + tail, lambda b, t: (l, b) + (0,) * len(tail))


def _weight_spec(l, tail):
    return pl.BlockSpec((None,) + tail, lambda b, t: (l,) + (0,) * len(tail),
                        pipeline_mode=pl.Buffered(1))


def _stacked_call(kernel, *, grid, inputs, in_specs, states_in, state_specs, prev_states,
                  out_shape, out_specs, n_tile_out, scratch_shapes, name):
    args = list(inputs)
    specs = list(in_specs)
    if states_in is not None:
        args += list(states_in)
        specs += list(state_specs)
    aliases = {}
    if prev_states is not None:
        for i, p in enumerate(prev_states):
            aliases[len(args)] = n_tile_out + i
            args.append(p)
            specs.append(pl.BlockSpec(memory_space=pl.ANY))
    return pl.pallas_call(
        kernel, grid=grid, in_specs=specs, out_specs=out_specs, out_shape=out_shape,
        scratch_shapes=scratch_shapes, input_output_aliases=aliases,
        compiler_params=_PARAMS, name=name)(*args)


def _proj_kernel(*refs, bb, tt, has_state, n_prev):
    x_ref, w_ref, cw_ref, gcw_ref, gp_ref = refs[:5]
    refs = refs[5:]
    if has_state:
        ha_ref, hg_ref = refs[:2]
        refs = refs[2:]
    refs = refs[n_prev:]
    ya_ref, qkv_ref, gz_ref, m4_ref, gates_ref, ta_ref, tg_ref, sa_ref, sg_ref = refs
    t = pl.program_id(1)

    @pl.when(t == 0)
    def _():
        if has_state:
            sa_ref[:, HIST - 2:HIST, :] = ha_ref[...]
            sg_ref[:, HIST - 3:HIST, :] = hg_ref[...]
        else:
            sa_ref[:, 0:HIST, :] = jnp.zeros((bb, HIST, D_CONV), f32)
            sg_ref[:, 0:HIST, :] = jnp.zeros((bb, HIST, D_QKV), f32)

    rows = bb * tt
    xb = x_ref[...].reshape(rows, D_MODEL).astype(bf16)

    def proj(lo, hi):
        return jnp.dot(xb, w_ref[:, lo:hi], preferred_element_type=f32)

    pa = proj(COL_A, COL_G)
    a_b = pa[:, 0:D_CONV]
    sa_ref[:, HIST:, :] = (pa[:, D_CONV:2 * D_CONV] * pa[:, 2 * D_CONV:3 * D_CONV]).reshape(bb, tt, D_CONV)
    conv = cw_ref[0:1, :] * sa_ref[:, pl.ds(HIST - 2, tt), :]
    for j in range(1, 3):
        conv = conv + cw_ref[j:j + 1, :] * sa_ref[:, pl.ds(HIST - 2 + j, tt), :]
    ya_ref[...] = a_b.reshape(bb, tt, D_CONV) * conv
    ta_ref[...] = sa_ref[:, tt + HIST - 2:tt + HIST, :]
    sa_ref[:, 0:HIST, :] = sa_ref[:, tt:tt + HIST, :]

    pg = proj(COL_G, COL_M)
    sg_ref[:, HIST:, :] = pg[:, 0:D_QKV].reshape(bb, tt, D_QKV)
    gz_ref[...] = jax.nn.silu(pg[:, D_QKV:]).reshape(bb, tt, D_GDN)
    for cb in range(D_QKV // LANE):
        cs = slice(cb * LANE, (cb + 1) * LANE)
        c = gcw_ref[0:1, cs] * sg_ref[:, pl.ds(HIST - 3, tt), cs]
        for j in range(1, 4):
            c = c + gcw_ref[j:j + 1, cs] * sg_ref[:, pl.ds(HIST - 3 + j, tt), cs]
        c = jax.nn.silu(c)
        if cb < 2 * GDN_HEADS:
            c = c * lax.rsqrt(jnp.sum(c * c, -1, keepdims=True) + NORM_EPS)
            if cb < GDN_HEADS:
                c = c * (GDN_D ** -0.5)
        qkv_ref[:, :, cs] = c
    tg_ref[...] = sg_ref[:, tt + HIST - 3:tt + HIST, :]
    sg_ref[:, 0:HIST, :] = sg_ref[:, tt:tt + HIST, :]

    pm = proj(COL_M, COL_GATE)
    m4_ref[:, :, 0:D_ML] = pm[:, 0:D_ML].reshape(bb, tt, D_ML)
    m4_ref[:, :, D_ML:2 * D_ML] = (pm[:, D_ML:2 * D_ML] * (ML_DH ** -0.5)).reshape(bb, tt, D_ML)
    m4_ref[:, :, 2 * D_ML:3 * D_ML] = pm[:, 2 * D_ML:3 * D_ML].reshape(bb, tt, D_ML)
    m4_ref[:, :, 3 * D_ML:4 * D_ML] = jax.nn.sigmoid(pm[:, 3 * D_ML:4 * D_ML]).reshape(bb, tt, D_ML)

    z = proj(COL_GATE, D_IN_PACKED) + gp_ref[0:1, :]
    lane = lax.broadcasted_iota(jnp.int32, z.shape, 1)
    g = -jnp.exp(gp_ref[1:2, :]) * _softplus(z)
    gates = jnp.where(lane < B_LANE, g,
                      jnp.where(lane < I_LANE, jax.nn.sigmoid(z),
                                jnp.where(lane < F_LANE, z,
                                          jnp.where(lane < F_LANE + ML_HEADS, -_softplus(-z), 0.0))))
    gates_ref[...] = gates.reshape(bb, tt, LANE)


def _proj_call(l, x, p, states_in, prev_states, *, bb, tt):
    B, T, _ = x.shape
    tile = functools.partial(_tile_spec, bb, tt)
    st_a, st_g = (2, D_CONV), (3, D_QKV)
    return _stacked_call(
        functools.partial(_proj_kernel, bb=bb, tt=tt, has_state=states_in is not None,
                          n_prev=0 if prev_states is None else 2),
        grid=(B // bb, T // tt),
        inputs=[x, p["w_in"], p["conv_w"], p["gdn_conv_w"], p["gate_p"]],
        in_specs=[tile(D_MODEL), _weight_spec(l, (D_MODEL, D_IN_PACKED)), _weight_spec(l, (3, D_CONV)),
                  _weight_spec(l, (4, D_QKV)), _weight_spec(l, (8, LANE))],
        states_in=states_in, state_specs=[_state_spec(l, bb, st_a), _state_spec(l, bb, st_g)],
        prev_states=prev_states,
        out_shape=(jax.ShapeDtypeStruct((B, T, D_CONV), f32),
                   jax.ShapeDtypeStruct((B, T, D_QKV), f32),
                   jax.ShapeDtypeStruct((B, T, D_GDN), f32),
                   jax.ShapeDtypeStruct((B, T, 4 * D_ML), f32),
                   jax.ShapeDtypeStruct((B, T, LANE), f32),
                   jax.ShapeDtypeStruct((DEPTH, B) + st_a, f32),
                   jax.ShapeDtypeStruct((DEPTH, B) + st_g, f32)),
        out_specs=(tile(D_CONV), tile(D_QKV), tile(D_GDN), tile(4 * D_ML), tile(LANE),
                   _state_spec(l, bb, st_a), _state_spec(l, bb, st_g)),
        n_tile_out=5,
        scratch_shapes=[pltpu.VMEM((bb, tt + HIST, D_CONV), f32),
                        pltpu.VMEM((bb, tt + HIST, D_QKV), f32)],
        name="in_proj")


def _bdot(a, b):
    return jnp.einsum('pik,pkj->pij', a.astype(bf16), b.astype(bf16), preferred_element_type=f32)


def _bdot_nt(a, b):
    return jnp.einsum('pik,pjk->pij', a.astype(bf16), b.astype(bf16), preferred_element_type=f32)


def _bdot_tn(a, b):
    return jnp.einsum('pki,pkj->pij', a.astype(bf16), b.astype(bf16), preferred_element_type=f32)


def _chunk_masks(L):
    r = lax.broadcasted_iota(jnp.int32, (L, L), 0)
    c = lax.broadcasted_iota(jnp.int32, (L, L), 1)
    return c <= r, c < r


def _gate_columns(gates_ref, tril, *, bb, nc, L, heads):
    gt = jnp.stack([gates_ref[bi, ci * L:(ci + 1) * L, :] for bi in range(bb) for ci in range(nc)])
    hi = gt.astype(bf16)
    r1 = gt - hi.astype(f32)
    mid = r1.astype(bf16)
    lo = (r1 - mid.astype(f32)).astype(bf16)
    trb = jnp.broadcast_to(tril, (bb * nc, L, L))
    cs = (jnp.einsum('pij,pjk->pik', trb, hi, preferred_element_type=f32)
          + jnp.einsum('pij,pjk->pik', trb, mid, preferred_element_type=f32)
          + jnp.einsum('pij,pjk->pik', trb, lo, preferred_element_type=f32))

    def col(x, lane0):
        return jnp.stack([x[:, :, lane0 + h:lane0 + h + 1] for h in range(heads)],
                         axis=1).reshape(bb * nc * heads, L, 1)

    def row(xt, lane0):
        return jnp.stack([xt[:, lane0 + h:lane0 + h + 1, :] for h in range(heads)],
                         axis=1).reshape(bb * nc * heads, 1, L)

    return gt, cs, col, row


def _gdn_kernel(*refs, bb, nc, L, has_state, n_prev):
    qkv_ref, gates_ref, gz_ref, nw_ref = refs[:4]
    refs = refs[4:]
    if has_state:
        s0_ref = refs[0]
        refs = refs[1:]
    yb_ref, s_ref = refs[n_prev:]
    t = pl.program_id(1)

    @pl.when(t == 0)
    def _():
        s_ref[...] = s0_ref[...] if has_state else jnp.zeros(s_ref.shape, f32)

    H = GDN_HEADS
    causal, strict = _chunk_masks(L)
    tril = causal.astype(bf16)
    steps = L.bit_length() - 2

    def heads_of(ref, col0):
        return jnp.stack([ref[bi, ci * L:(ci + 1) * L, col0 + h * GDN_D:col0 + (h + 1) * GDN_D]
                          for bi in range(bb) for ci in range(nc) for h in range(H)])

    gt, cs, col, row = _gate_columns(gates_ref, tril, bb=bb, nc=nc, L=L, heads=H)
    cs_t = jnp.swapaxes(cs, 1, 2)
    G = col(cs, G_LANE)
    g_row = row(cs_t, G_LANE)
    g_last = G[:, L - 1:L, :]
    beta = col(gt, B_LANE)
    q = heads_of(qkv_ref, 0)
    k = heads_of(qkv_ref, D_GDN)
    v = heads_of(qkv_ref, 2 * D_GDN)

    decay = jnp.where(causal, jnp.exp(jnp.where(causal, G - g_row, 0.0)), 0.0)
    kb = k * beta
    kk = _bdot_nt(kb, k) * decay
    qk = _bdot_nt(q, k) * decay
    P = jnp.where(strict, kk, 0.0)
    R = -P
    for _ in range(steps):
        P = _bdot(P, P)
        R = R + P + _bdot(R, P)
    eG = jnp.exp(G)
    rhs = jnp.concatenate([v * beta, kb * eG], axis=-1)
    w = rhs + _bdot(R, rhs)
    qg = q * eG
    kdec = k * jnp.exp(g_last - G)
    e_last = jnp.exp(g_last)

    def chunk(x, ci):
        return x.reshape((bb, nc, H) + x.shape[1:])[:, ci].reshape((bb * H,) + x.shape[1:])

    nw = nw_ref[...]
    S = s_ref[...].reshape(bb * H, GDN_D, GDN_D)
    for ci in range(nc):
        rs = slice(ci * L, (ci + 1) * L)
        w_c = chunk(w, ci)
        xs = _bdot(jnp.concatenate([w_c[:, :, GDN_D:], chunk(qg, ci)], axis=1), S)
        u = w_c[:, :, 0:GDN_D] - xs[:, 0:L]
        o = xs[:, L:] + _bdot(chunk(qk, ci), u)
        S = S * chunk(e_last, ci) + _bdot_tn(chunk(kdec, ci), u)
        o = o * lax.rsqrt(jnp.mean(o * o, -1, keepdims=True) + NORM_EPS) * nw
        o = o.reshape(bb, H, L, GDN_D)
        for bi in range(bb):
            for h in range(H):
                hs = slice(h * GDN_D, (h + 1) * GDN_D)
                yb_ref[bi, rs, hs] = o[bi, h] * gz_ref[bi, rs, hs]
    s_ref[...] = S.reshape(bb, H, GDN_D, GDN_D)


def _gdn_call(l, qkv, gates, gz, p, states_in, prev_states, *, bb, nc, L):
    B, T, _ = qkv.shape
    tt = nc * L
    tile = functools.partial(_tile_spec, bb, tt)
    st = (GDN_HEADS, GDN_D, GDN_D)
    return _stacked_call(
        functools.partial(_gdn_kernel, bb=bb, nc=nc, L=L, has_state=states_in is not None,
                          n_prev=0 if prev_states is None else 1),
        grid=(B // bb, T // tt),
        inputs=[qkv, gates, gz, p["gdn_norm_w"]],
        in_specs=[tile(D_QKV), tile(LANE), tile(D_GDN), _weight_spec(l, (1, GDN_D))],
        states_in=states_in, state_specs=[_state_spec(l, bb, st)], prev_states=prev_states,
        out_shape=(jax.ShapeDtypeStruct((B, T, D_GDN), f32),
                   jax.ShapeDtypeStruct((DEPTH, B) + st, f32)),
        out_specs=(tile(D_GDN), _state_spec(l, bb, st)),
        n_tile_out=1, scratch_shapes=[], name="gdn")


def _split3(x):
    hi = x.astype(bf16)
    r1 = x - hi.astype(f32)
    mid = r1.astype(bf16)
    lo = (r1 - mid.astype(f32)).astype(bf16)
    return hi, mid, lo


def _mlstm_chunks(m4_ref, gates_ref, nw_ref, yc_ref, cn_ref, mp_ref, *, bb, nc, L):
    DH = ML_DH
    NP = ML_HEADS // 2
    nbc, npair = bb * nc, bb * nc * NP
    lane = lax.broadcasted_iota(jnp.int32, (1, LANE), 1)
    hi_half = lane >= DH
    row = lax.broadcasted_iota(jnp.int32, (L, LANE), 0)
    col = lax.broadcasted_iota(jnp.int32, (L, LANE), 1) % DH
    causal2 = col <= row
    eye2 = (col == row).astype(f32)
    r2 = lax.broadcasted_iota(jnp.int32, (LANE, LANE), 0) // DH
    c2 = lax.broadcasted_iota(jnp.int32, (LANE, LANE), 1) // DH
    blk = r2 == c2
    blk2 = jnp.concatenate([blk, blk], axis=1)
    ones_bd = blk.astype(bf16)

    def tiles(col0):
        return jnp.stack([m4_ref[bi, ci * L:(ci + 1) * L, col0 + pp * LANE:col0 + (pp + 1) * LANE]
                          for bi in range(bb) for ci in range(nc) for pp in range(NP)])

    def pair_cols(x, lane0):
        a = jnp.stack([x[:, :, lane0 + 2 * pp:lane0 + 2 * pp + 1] for pp in range(NP)], axis=1)
        b = jnp.stack([x[:, :, lane0 + 2 * pp + 1:lane0 + 2 * pp + 2] for pp in range(NP)], axis=1)
        return jnp.where(hi_half, b, a).reshape(npair, L, LANE)

    tril = (lax.broadcasted_iota(jnp.int32, (L, L), 1) <= lax.broadcasted_iota(jnp.int32, (L, L), 0))
    gt = jnp.stack([gates_ref[bi, ci * L:(ci + 1) * L, :] for bi in range(bb) for ci in range(nc)])
    trb = jnp.broadcast_to(tril.astype(bf16), (nbc, L, L))
    cs = sum(jnp.einsum('pij,pjk->pik', trb, t, preferred_element_type=f32) for t in _split3(gt))
    a_t = pltpu.roll(gt, F_LANE - I_LANE, axis=2) - cs
    A_t = a_t
    rows3 = lax.broadcasted_iota(jnp.int32, a_t.shape, 1)
    sh = 1
    while sh < L:
        A_t = jnp.maximum(A_t, jnp.where(rows3 >= sh, pltpu.roll(A_t, sh, axis=1), -jnp.inf))
        sh *= 2
    F = pair_cols(cs, F_LANE)
    a_p = pair_cols(a_t, F_LANE)
    A_p = pair_cols(A_t, F_LANE)
    onesb = jnp.ones((npair, L, L), bf16)
    a_row = sum(jnp.einsum('pij,pjk->pik', onesb, t, preferred_element_type=f32)
                for t in _split3(a_p * eye2))
    Dm = jnp.where(causal2, F + a_row, -jnp.inf)
    Q = tiles(0)
    K = tiles(D_ML)
    V = tiles(2 * D_ML)
    OG = tiles(3 * D_ML)
    if L < DH:
        zpad = jnp.zeros((npair, DH - L, LANE), f32)
        Kp, Vp = jnp.concatenate([K, zpad], axis=1), jnp.concatenate([V, zpad], axis=1)
    else:
        Kp, Vp = K, V
    k_bd = jnp.concatenate([jnp.where(hi_half, 0.0, Kp), jnp.where(hi_half, Kp, 0.0)], axis=1).astype(bf16)
    v_bd = jnp.concatenate([jnp.where(hi_half, 0.0, Vp), jnp.where(hi_half, Vp, 0.0)], axis=1).astype(bf16)
    vo = jnp.concatenate([v_bd, jnp.broadcast_to(ones_bd, (npair, LANE, LANE))], axis=-1)
    s_raw = jnp.einsum('pik,pjk->pij', Q.astype(bf16), k_bd, preferred_element_type=f32)
    v_one = jnp.concatenate([V, jnp.ones((npair, L, LANE), f32)], axis=-1).astype(bf16)

    def chunk(x, ci):
        return x.reshape((bb, nc, NP) + x.shape[1:])[:, ci].reshape((bb * NP,) + x.shape[1:])

    nw = jnp.stack([nw_ref[0:1, pp * LANE:(pp + 1) * LANE] for pp in range(NP)])
    nw = jnp.broadcast_to(nw[None], (bb, NP, 1, LANE)).reshape(bb * NP, 1, LANE)
    mean_w = (blk.astype(f32) * (1.0 / DH)).astype(bf16)

    def head_mean(x):
        hi = x.astype(bf16)
        lo = (x - hi.astype(f32)).astype(bf16)
        mw = jnp.broadcast_to(mean_w, (x.shape[0], LANE, LANE))
        return (jnp.einsum('pik,pkj->pij', hi, mw, preferred_element_type=f32)
                + jnp.einsum('pik,pkj->pij', lo, mw, preferred_element_type=f32))

    CN = cn_ref[...]
    m = mp_ref[:, 0:1, :]
    for ci in range(nc):
        F_c, A_c, a_c = chunk(F, ci), chunk(A_p, ci), chunk(a_p, ci)
        M = F_c + jnp.maximum(m, A_c)
        wD = jnp.exp(chunk(Dm, ci) - M)
        wI = jnp.exp(F_c + m - M)
        sc = chunk(s_raw, ci) * wD
        numq = (jnp.concatenate([wI, wI], axis=-1) * _bdot(chunk(Q, ci), CN)
                + jnp.einsum('pik,pkj->pij', sc.astype(bf16), chunk(vo, ci), preferred_element_type=f32))
        hh = numq[:, :, 0:LANE] / jnp.maximum(jnp.abs(numq[:, :, LANE:]), jnp.exp(-M))
        m_new = M[:, L - 1:L, :]
        f_last = F_c[:, L - 1:L, :]
        wk = jnp.exp(f_last - F_c + (a_c + F_c) - m_new)
        dc = jnp.exp(f_last + m - m_new)
        kw = chunk(K, ci) * wk
        upd = jnp.einsum('pki,pkj->pij', kw.astype(bf16), chunk(v_one, ci), preferred_element_type=f32)
        CN = jnp.concatenate([dc, dc], axis=-1) * CN + jnp.where(blk2, upd, 0.0)
        m = m_new
        hh = chunk(OG, ci) * hh
        hc = hh - head_mean(hh)
        var = head_mean(hc * hc)
        y = (hc * lax.rsqrt(var + LN_EPS) * nw).reshape(bb, NP, L, LANE)
        for bi in range(bb):
            for pp in range(NP):
                yc_ref[bi, ci * L:(ci + 1) * L, pp * LANE:(pp + 1) * LANE] = y[bi, pp]
    cn_ref[...] = CN
    mp_ref[:, 0:1, :] = m


def _mlstm_state_in(c0_ref, n0_ref, m0_ref, cn_ref, mp_ref, *, bb):
    DH, NP = ML_DH, ML_HEADS // 2
    cn_ref[...] = jnp.zeros(cn_ref.shape, f32)
    if c0_ref is None:
        mp_ref[...] = jnp.zeros(mp_ref.shape, f32)
        return
    lane = lax.broadcasted_iota(jnp.int32, (1, LANE), 1)
    for bi in range(bb):
        for pp in range(NP):
            for hl in range(2):
                h = 2 * pp + hl
                rs, ls = slice(hl * DH, (hl + 1) * DH), slice(hl * DH, (hl + 1) * DH)
                cn_ref[bi * NP + pp, rs, ls] = c0_ref[bi, h]
                n_col = jnp.broadcast_to(n0_ref[bi, h:h + 1, :], (DH, DH)).T
                cn_ref[bi * NP + pp, rs, LANE + hl * DH:LANE + (hl + 1) * DH] = n_col
            m_a = m0_ref[bi, 0:1, 2 * pp:2 * pp + 1]
            m_b = m0_ref[bi, 0:1, 2 * pp + 1:2 * pp + 2]
            mp_ref[bi * NP + pp, 0:1, :] = jnp.where(lane >= DH, m_b, m_a)


def _mlstm_state_out(cn_ref, mp_ref, c_ref, n_ref, m_ref, *, bb):
    DH, NP = ML_DH, ML_HEADS // 2
    for bi in range(bb):
        for pp in range(NP):
            for hl in range(2):
                h = 2 * pp + hl
                rs = slice(hl * DH, (hl + 1) * DH)
                c_ref[bi, h] = cn_ref[bi * NP + pp, rs, hl * DH:(hl + 1) * DH]
                n_blk = cn_ref[bi * NP + pp, rs, LANE + hl * DH:LANE + (hl + 1) * DH]
                n_ref[bi, h:h + 1, :] = n_blk.T[0:1, :]
                m_ref[bi, 0:1, h:h + 1] = mp_ref[bi * NP + pp, 0:1, hl * DH:hl * DH + 1]


def _mlstm_kernel(*refs, bb, nc, L, has_state, n_prev):
    m4_ref, gates_ref, nw_ref = refs[:3]
    refs = refs[3:]
    c0_ref = n0_ref = m0_ref = None
    if has_state:
        c0_ref, n0_ref, m0_ref = refs[:3]
        refs = refs[3:]
    yc_ref, c_ref, n_ref, m_ref, cn_ref, mp_ref = refs[n_prev:]
    t = pl.program_id(1)

    @pl.when(t == 0)
    def _():
        _mlstm_state_in(c0_ref, n0_ref, m0_ref, cn_ref, mp_ref, bb=bb)

    _mlstm_chunks(m4_ref, gates_ref, nw_ref, yc_ref, cn_ref, mp_ref, bb=bb, nc=nc, L=L)

    @pl.when(t == pl.num_programs(1) - 1)
    def _():
        _mlstm_state_out(cn_ref, mp_ref, c_ref, n_ref, m_ref, bb=bb)


def _mlstm_call(l, m4, gates, p, states_in, prev_states, *, bb, nc, L):
    B, T, _ = m4.shape
    tt = nc * L
    tile = functools.partial(_tile_spec, bb, tt)
    st_c, st_n, st_m = (ML_HEADS, ML_DH, ML_DH), (ML_HEADS, ML_DH), (1, ML_HEADS)
    return _stacked_call(
        functools.partial(_mlstm_kernel, bb=bb, nc=nc, L=L, has_state=states_in is not None,
                          n_prev=0 if prev_states is None else 3),
        grid=(B // bb, T // tt),
        inputs=[m4, gates, p["ml_norm_w"]],
        in_specs=[tile(4 * D_ML), tile(LANE), _weight_spec(l, (1, D_ML))],
        states_in=states_in,
        state_specs=[_state_spec(l, bb, st_c), _state_spec(l, bb, st_n), _state_spec(l, bb, st_m)],
        prev_states=prev_states,
        out_shape=(jax.ShapeDtypeStruct((B, T, D_ML), f32),
                   jax.ShapeDtypeStruct((DEPTH, B) + st_c, f32),
                   jax.ShapeDtypeStruct((DEPTH, B) + st_n, f32),
                   jax.ShapeDtypeStruct((DEPTH, B) + st_m, f32)),
        out_specs=(tile(D_ML), _state_spec(l, bb, st_c), _state_spec(l, bb, st_n),
                   _state_spec(l, bb, st_m)),
        n_tile_out=1,
        scratch_shapes=[pltpu.VMEM((bb * (ML_HEADS // 2), LANE, 2 * LANE), f32),
                        pltpu.VMEM((bb * (ML_HEADS // 2), 8, LANE), f32)],
        name="mlstm")


def _out_ffn_kernel(*refs, bb, tt, has_state, n_prev):
    (x_ref, ya_ref, yb_ref, yc_ref, wo_ref, ln1_ref, wup_ref, fcw_ref, wdn_ref, ln2_ref) = refs[:10]
    refs = refs[10:]
    if has_state:
        hf_ref = refs[0]
        refs = refs[1:]
    xo_ref, tf_ref, sf_ref = refs[n_prev:]
    t = pl.program_id(1)

    @pl.when(t == 0)
    def _():
        if has_state:
            sf_ref[:, HIST - 2:HIST, :] = hf_ref[...]
        else:
            sf_ref[:, 0:HIST, :] = jnp.zeros((bb, HIST, D_FF), f32)

    rows = bb * tt
    x = x_ref[...].reshape(rows, D_MODEL)
    mix = (jnp.dot(ya_ref[...].reshape(rows, D_CONV).astype(bf16), wo_ref[0:D_CONV, :],
                   preferred_element_type=f32)
           + jnp.dot(yb_ref[...].reshape(rows, D_GDN).astype(bf16), wo_ref[D_CONV:D_CONV + D_GDN, :],
                     preferred_element_type=f32)
           + jnp.dot(yc_ref[...].reshape(rows, D_ML).astype(bf16), wo_ref[D_CONV + D_GDN:, :],
                     preferred_element_type=f32))
    x1 = _layer_norm(ALPHA * x + mix, ln1_ref[0:1, :], ln1_ref[1:2, :])
    x1b = x1.astype(bf16)

    acc = None
    for lo, hi in FF_CHUNKS:
        gate = jnp.dot(x1b, wup_ref[:, lo:hi], preferred_element_type=f32)
        val = jnp.dot(x1b, wup_ref[:, D_FF + lo:D_FF + hi], preferred_element_type=f32)
        sf_ref[:, HIST:, lo:hi] = gate.reshape(bb, tt, hi - lo)
        conv = fcw_ref[0:1, lo:hi] * sf_ref[:, pl.ds(HIST - 2, tt), lo:hi]
        for j in range(1, 3):
            conv = conv + fcw_ref[j:j + 1, lo:hi] * sf_ref[:, pl.ds(HIST - 2 + j, tt), lo:hi]
        hmid = jax.nn.silu(conv).reshape(rows, hi - lo) * val
        part = jnp.dot(hmid.astype(bf16), wdn_ref[lo:hi, :], preferred_element_type=f32)
        acc = part if acc is None else acc + part
    tf_ref[...] = sf_ref[:, tt + HIST - 2:tt + HIST, :]
    sf_ref[:, 0:HIST, :] = sf_ref[:, tt:tt + HIST, :]
    x2 = _layer_norm(ALPHA * x1 + acc, ln2_ref[0:1, :], ln2_ref[1:2, :])
    xo_ref[...] = x2.reshape(bb, tt, D_MODEL)


def _out_ffn_call(l, x, ya, yb, yc, p, states_in, prev_states, *, bb, tt):
    B, T, _ = x.shape
    tile = functools.partial(_tile_spec, bb, tt)
    st = (2, D_FF)
    return _stacked_call(
        functools.partial(_out_ffn_kernel, bb=bb, tt=tt, has_state=states_in is not None,
                          n_prev=0 if prev_states is None else 1),
        grid=(B // bb, T // tt),
        inputs=[x, ya, yb, yc, p["w_o"], p["ln1"], p["w_up"], p["ffn_conv_w"], p["w_down"], p["ln2"]],
        in_specs=[tile(D_MODEL), tile(D_CONV), tile(D_GDN), tile(D_ML),
                  _weight_spec(l, (D_MODEL, D_MODEL)), _weight_spec(l, (2, D_MODEL)),
                  _weight_spec(l, (D_MODEL, 2 * D_FF)), _weight_spec(l, (3, D_FF)),
                  _weight_spec(l, (D_FF, D_MODEL)), _weight_spec(l, (2, D_MODEL))],
        states_in=states_in, state_specs=[_state_spec(l, bb, st)], prev_states=prev_states,
        out_shape=(jax.ShapeDtypeStruct((B, T, D_MODEL), f32),
                   jax.ShapeDtypeStruct((DEPTH, B) + st, f32)),
        out_specs=(tile(D_MODEL), _state_spec(l, bb, st)),
        n_tile_out=1, scratch_shapes=[pltpu.VMEM((bb, tt + HIST, D_FF), f32)], name="out_ffn")


def _prep_params(w_in, conv_w, gdn_conv_w, gdn_a_log, gdn_dt_bias, gdn_norm_w,
                 ml_i_bias, ml_f_bias, ml_norm_w, w_o, ln1_g, ln1_b,
                 w_up, ffn_conv_w, w_down, ln2_g, ln2_b):
    o = 0
    seg = {}
    for name, n in (("a", 3 * D_CONV), ("g", 4 * D_GDN), ("ga", GDN_HEADS), ("gb", GDN_HEADS),
                    ("m", 4 * D_ML), ("mi", ML_HEADS), ("mf", ML_HEADS)):
        seg[name] = w_in[:, :, o:o + n]
        o += n
    gate_cols = jnp.concatenate([seg["ga"], seg["gb"], seg["mi"], seg["mf"]], axis=2)
    gate_cols = jnp.pad(gate_cols, ((0, 0), (0, 0), (0, LANE - gate_cols.shape[2])))
    w_packed = jnp.concatenate([seg["a"], seg["g"], seg["m"], gate_cols], axis=2).astype(bf16)
    zeros4 = jnp.zeros((DEPTH, 4), f32)
    bias_row = jnp.pad(jnp.concatenate([gdn_dt_bias, zeros4, ml_i_bias, ml_f_bias], axis=1),
                       ((0, 0), (0, LANE - 16)))
    alog_row = jnp.pad(gdn_a_log, ((0, 0), (0, LANE - GDN_HEADS)))
    gate_p = jnp.zeros((DEPTH, 8, LANE), f32).at[:, 0].set(bias_row).at[:, 1].set(alog_row)
    return dict(
        w_in=w_packed, conv_w=conv_w, gdn_conv_w=gdn_conv_w, gate_p=gate_p,
        gdn_norm_w=gdn_norm_w.reshape(DEPTH, 1, GDN_D), ml_norm_w=ml_norm_w.reshape(DEPTH, 1, D_ML),
        w_o=w_o.astype(bf16), ln1=jnp.stack([ln1_g, ln1_b], axis=1),
        w_up=w_up.astype(bf16), ffn_conv_w=ffn_conv_w, w_down=w_down.astype(bf16),
        ln2=jnp.stack([ln2_g, ln2_b], axis=1),
    )


def _tiling(B, T):
    L = min(CHUNK, T)
    if T >= 512:
        return dict(bb=1, tt=256, fbb=1, ftt=512, L=L, rbb=min(B, 8), gnc=1, mnc=2)
    return dict(bb=256 // T, tt=T, fbb=256 // T, ftt=T, L=L, rbb=min(B, 16), gnc=T // L, mnc=T // L)


def _trunk(x, states, p):
    B, T, _ = x.shape
    c = _tiling(B, T)
    conv_new = gdn_new = ml_new = ffn_new = None
    for l in range(DEPTH):
        sel = lambda idx: None if states is None else [states[i] for i in idx]
        ya, qkv, gz, m4, gates, *conv_new = _proj_call(l, x, p, sel((0, 1)), conv_new,
                                                       bb=c["bb"], tt=c["tt"])
        yb, *gdn_new = _gdn_call(l, qkv, gates, gz, p, sel((2,)), gdn_new,
                                 bb=c["rbb"], nc=c["gnc"], L=c["L"])
        yc, *ml_new = _mlstm_call(l, m4, gates, p, sel((3, 4, 5)), ml_new,
                                  bb=c["rbb"], nc=c["mnc"], L=c["L"])
        x, *ffn_new = _out_ffn_call(l, x, ya, yb, yc, p, sel((6,)), ffn_new,
                                    bb=c["fbb"], tt=c["ftt"])
    return x, (conv_new[0], conv_new[1], gdn_new[0], ml_new[0], ml_new[1],
               ml_new[2].reshape(DEPTH, B, ML_HEADS), ffn_new[0])


def kernel(x_prompt, x_sample, state_conv_mix, state_gdn_conv, state_gdn, state_mlstm_c,
           state_mlstm_n, state_mlstm_m, state_ffn_conv,
           w_in, conv_w, gdn_conv_w, gdn_a_log, gdn_dt_bias, gdn_norm_w,
           ml_i_bias, ml_f_bias, ml_norm_w, w_o, ln1_g, ln1_b,
           w_up, ffn_conv_w, w_down, ln2_g, ln2_b):
    p = _prep_params(w_in, conv_w, gdn_conv_w, gdn_a_log, gdn_dt_bias, gdn_norm_w,
                     ml_i_bias, ml_f_bias, ml_norm_w, w_o, ln1_g, ln1_b,
                     w_up, ffn_conv_w, w_down, ln2_g, ln2_b)
    Bs = x_sample.shape[0]
    y_prompt, p_new = _trunk(x_prompt, None, p)
    s_states = (state_conv_mix, state_gdn_conv, state_gdn, state_mlstm_c, state_mlstm_n,
                state_mlstm_m.reshape(DEPTH, Bs, 1, ML_HEADS), state_ffn_conv)
    y_sample, s_new = _trunk(x_sample, s_states, p)
    return (y_prompt, y_sample, *p_new, *s_new)
```

```python
import functools

import jax
import jax.numpy as jnp
from jax import lax
from jax.experimental import pallas as pl
from jax.experimental.pallas import tpu as pltpu

f32 = jnp.float32
bf16 = jnp.bfloat16

D_MODEL = 1024
DEPTH = 2
D_CONV = 256
GDN_HEADS = 4
GDN_D = 128
D_GDN = GDN_HEADS * GDN_D
D_QKV = 3 * D_GDN
ML_HEADS = 4
ML_DH = 64
D_ML = ML_HEADS * ML_DH
D_FF = 2816
CHUNK = 64
ALPHA = (2.0 * DEPTH) ** 0.25
LN_EPS = 1e-5
NORM_EPS = 1e-6

LANE = 128
HIST = 8
COL_A = 0
COL_G = 3 * D_CONV
COL_GA = COL_G + 4 * D_GDN
COL_MQ = COL_GA + 2 * GDN_HEADS
COL_MI = COL_MQ + 4 * D_ML
D_IN = COL_MI + 2 * ML_HEADS
D_TAIL = 4 * D_ML + LANE
G_LANE, B_LANE, I_LANE, F_LANE = 0, 4, 8, 12
MXU_DIM = 256
FF_CHUNKS = ((0, 6 * MXU_DIM), (6 * MXU_DIM, D_FF))
VMEM_LIMIT = 56 * 1024 * 1024
_PARAMS = pltpu.CompilerParams(dimension_semantics=("arbitrary", "arbitrary"),
                               vmem_limit_bytes=VMEM_LIMIT)


def _softplus(z):
    return jnp.maximum(z, 0.0) + jnp.log1p(jnp.exp(-jnp.abs(z)))


def _layer_norm(x, g, b):
    mu = jnp.mean(x, -1, keepdims=True)
    xc = x - mu
    var = jnp.mean(xc * xc, -1, keepdims=True)
    return xc * lax.rsqrt(var + LN_EPS) * g + b


def _tile_spec(bb, tt, c):
    return pl.BlockSpec((bb, tt, c), lambda b, t: (b, t, 0))


def _state_spec(l, bb, tail, every_layer=False):
    if every_layer:
        return pl.BlockSpec((DEPTH, bb) + tail, lambda b, t: (0, b) + (0,) * len(tail))
    return pl.BlockSpec((None, bb) + tail, lambda b, t: (l, b) + (0,) * len(tail))


def _own_layer(refs, l, first):
    if not first:
        return refs

    @pl.when(pl.program_id(1) == 0)
    def _():
        for r in refs:
            for j in range(DEPTH):
                if j != l:
                    r[j] = jnp.zeros(r.shape[1:], f32)

    return [r.at[l] for r in refs]


def _weight_spec(l, tail):
    return pl.BlockSpec((None,) + tail, lambda b, t: (l,) + (0,) * len(tail),
                        pipeline_mode=pl.Buffered(1))


def _stacked_call(kernel, *, l, B, bb, grid, inputs, in_specs, states_in, prev_states, state_tails,
                  tile_shapes, tile_specs, scratch_shapes, name, **static):
    first = prev_states is None
    args = list(inputs)
    specs = list(in_specs)
    if states_in is not None:
        args += list(states_in)
        specs += [_state_spec(l, bb, tail) for tail in state_tails]
    aliases = {}
    if not first:
        for i, p in enumerate(prev_states):
            aliases[len(args)] = len(tile_shapes) + i
            args.append(p)
            specs.append(pl.BlockSpec(memory_space=pl.ANY))
    return pl.pallas_call(
        functools.partial(kernel, l=l, first=first, has_state=states_in is not None,
                          n_prev=0 if first else len(prev_states), bb=bb, **static),
        grid=grid, in_specs=specs,
        out_specs=tuple(tile_specs) + tuple(_state_spec(l, bb, tail, every_layer=first)
                                            for tail in state_tails),
        out_shape=tuple(tile_shapes) + tuple(jax.ShapeDtypeStruct((DEPTH, B) + tail, f32)
                                             for tail in state_tails),
        scratch_shapes=scratch_shapes, input_output_aliases=aliases,
        compiler_params=_PARAMS, name=name)(*args)


def _proj_kernel(*refs, l, first, bb, tt, has_state, n_prev):
    x_ref, w_ref, wt_ref, cw_ref, gcw_ref, gp_ref = refs[:6]
    refs = refs[6:]
    if has_state:
        ha_ref, hg_ref = refs[:2]
        refs = refs[2:]
    ya_ref, qkv_ref, gz_ref, m4_ref, gates_ref, ta_ref, tg_ref, sa_ref, sg_ref = refs[n_prev:]
    ta_ref, tg_ref = _own_layer([ta_ref, tg_ref], l, first)
    t = pl.program_id(1)

    @pl.when(t == 0)
    def _():
        if has_state:
            sa_ref[:, HIST - 2:HIST, :] = ha_ref[...]
            sg_ref[:, HIST - 3:HIST, :] = hg_ref[...]
        else:
            sa_ref[:, 0:HIST, :] = jnp.zeros((bb, HIST, D_CONV), f32)
            sg_ref[:, 0:HIST, :] = jnp.zeros((bb, HIST, D_QKV), f32)

    rows = bb * tt
    xb = x_ref[...].reshape(rows, D_MODEL).astype(bf16)

    def post_a(pa):
        a_b = pa[:, 0:D_CONV]
        sa_ref[:, HIST:, :] = (pa[:, D_CONV:2 * D_CONV] * pa[:, 2 * D_CONV:3 * D_CONV]).reshape(bb, tt, D_CONV)
        conv = cw_ref[0:1, :] * sa_ref[:, pl.ds(HIST - 2, tt), :]
        for j in range(1, 3):
            conv = conv + cw_ref[j:j + 1, :] * sa_ref[:, pl.ds(HIST - 2 + j, tt), :]
        ya_ref[...] = a_b.reshape(bb, tt, D_CONV) * conv
        ta_ref[...] = sa_ref[:, tt + HIST - 2:tt + HIST, :]
        sa_ref[:, 0:HIST, :] = sa_ref[:, tt:tt + HIST, :]

    def post_qkv(c0, pg):
        w = pg.shape[1]
        sg_ref[:, HIST:, c0:c0 + w] = pg.reshape(bb, tt, w)
        for cb in range(c0 // LANE, (c0 + w) // LANE):
            cs = slice(cb * LANE, (cb + 1) * LANE)
            c = gcw_ref[0:1, cs] * sg_ref[:, pl.ds(HIST - 3, tt), cs]
            for j in range(1, 4):
                c = c + gcw_ref[j:j + 1, cs] * sg_ref[:, pl.ds(HIST - 3 + j, tt), cs]
            c = jax.nn.silu(c)
            if cb < 2 * GDN_HEADS:
                c = c * lax.rsqrt(jnp.sum(c * c, -1, keepdims=True) + NORM_EPS)
                if cb < GDN_HEADS:
                    c = c * (GDN_D ** -0.5)
            qkv_ref[:, :, cs] = c
        tg_ref[:, :, c0:c0 + w] = sg_ref[:, tt + HIST - 3:tt + HIST, c0:c0 + w]
        sg_ref[:, 0:HIST, c0:c0 + w] = sg_ref[:, tt:tt + HIST, c0:c0 + w]

    def post_gz(c0, pz):
        gz_ref[:, :, c0:c0 + pz.shape[1]] = jax.nn.silu(pz).reshape(bb, tt, pz.shape[1])

    def post_m(part, pm):
        if part == 1:
            pm = pm * (ML_DH ** -0.5)
        elif part == 3:
            pm = jax.nn.sigmoid(pm)
        m4_ref[:, :, part * D_ML:(part + 1) * D_ML] = pm.reshape(bb, tt, D_ML)

    def post_gates(z):
        z = z + gp_ref[0:1, :]
        lane = lax.broadcasted_iota(jnp.int32, z.shape, 1)
        g = -jnp.exp(gp_ref[1:2, :]) * _softplus(z)
        gates = jnp.where(lane < B_LANE, g,
                          jnp.where(lane < I_LANE, jax.nn.sigmoid(z),
                                    jnp.where(lane < F_LANE, z,
                                              jnp.where(lane < F_LANE + ML_HEADS, -_softplus(-z), 0.0))))
        gates_ref[...] = gates.reshape(bb, tt, LANE)

    W = MXU_DIM
    qkv = [(w_ref, COL_G + c, COL_G + c + W, functools.partial(post_qkv, c)) for c in range(0, D_QKV, W)]
    gz = [(w_ref, COL_G + D_QKV + c, COL_G + D_QKV + c + W, functools.partial(post_gz, c))
          for c in range(0, D_GDN, W)]
    mm = [(wt_ref, i * D_ML, (i + 1) * D_ML, functools.partial(post_m, i)) for i in range(4)]
    light = [(w_ref, COL_A, COL_G, post_a)] + mm + gz + [(wt_ref, 4 * D_ML, 4 * D_ML + LANE, post_gates)]
    tasks = []
    for i in range(max(len(qkv), len(light))):
        tasks += qkv[i:i + 1] + light[i:i + 1]
    pending = None
    for ref, lo, hi, post in tasks:
        res = jnp.dot(xb, ref[:, lo:hi], preferred_element_type=f32)
        if pending is not None:
            pending[0](pending[1])
        pending = (post, res)
    pending[0](pending[1])


def _proj_call(l, x, p, states_in, prev_states, *, bb, tt):
    B, T, _ = x.shape
    tile = functools.partial(_tile_spec, bb, tt)
    widths = (D_CONV, D_QKV, D_GDN, 4 * D_ML, LANE)
    return _stacked_call(
        _proj_kernel, l=l, B=B, bb=bb, tt=tt, grid=(B // bb, T // tt),
        inputs=[x, p["w_in"], p["w_tail"], p["conv_w"], p["gdn_conv_w"], p["gate_p"]],
        in_specs=[tile(D_MODEL), _weight_spec(l, (D_MODEL, D_IN)), _weight_spec(l, (D_MODEL, D_TAIL)),
                  _weight_spec(l, (3, D_CONV)), _weight_spec(l, (4, D_QKV)), _weight_spec(l, (8, LANE))],
        states_in=states_in, prev_states=prev_states, state_tails=[(2, D_CONV), (3, D_QKV)],
        tile_shapes=[jax.ShapeDtypeStruct((B, T, w), f32) for w in widths],
        tile_specs=[tile(w) for w in widths],
        scratch_shapes=[pltpu.VMEM((bb, tt + HIST, D_CONV), f32),
                        pltpu.VMEM((bb, tt + HIST, D_QKV), f32)],
        name="in_proj")


def _bdot(a, b):
    return jnp.einsum('pik,pkj->pij', a.astype(bf16), b.astype(bf16), preferred_element_type=f32)


def _bdot_nt(a, b):
    return jnp.einsum('pik,pjk->pij', a.astype(bf16), b.astype(bf16), preferred_element_type=f32)


def _bdot_tn(a, b):
    return jnp.einsum('pki,pkj->pij', a.astype(bf16), b.astype(bf16), preferred_element_type=f32)


def _split3(x):
    hi = x.astype(bf16)
    r1 = x - hi.astype(f32)
    mid = r1.astype(bf16)
    lo = (r1 - mid.astype(f32)).astype(bf16)
    return hi, mid, lo


def _gate_tiles(gates_ref, *, bb, nc, L):
    gt = jnp.stack([gates_ref[bi, ci * L:(ci + 1) * L, :] for bi in range(bb) for ci in range(nc)])
    tril = lax.broadcasted_iota(jnp.int32, (L, L), 1) <= lax.broadcasted_iota(jnp.int32, (L, L), 0)
    trb = jnp.broadcast_to(tril.astype(bf16), (bb * nc, L, L))
    cs = sum(jnp.einsum('pij,pjk->pik', trb, t, preferred_element_type=f32) for t in _split3(gt))
    return gt, cs


def _rows_from_cols(x, eye2):
    L = x.shape[1]
    onesb = jnp.ones((x.shape[0], L, L), bf16)
    return sum(jnp.einsum('pij,pjk->pik', onesb, t, preferred_element_type=f32)
               for t in _split3(x * eye2))


def _gdn_kernel(*refs, l, first, bb, nc, L, has_state, n_prev):
    qkv_ref, gates_ref, gz_ref, nw_ref = refs[:4]
    refs = refs[4:]
    if has_state:
        s0_ref = refs[0]
        refs = refs[1:]
    yb_ref, s_ref = refs[n_prev:]
    (s_ref,) = _own_layer([s_ref], l, first)
    t = pl.program_id(1)

    @pl.when(t == 0)
    def _():
        s_ref[...] = s0_ref[...] if has_state else jnp.zeros(s_ref.shape, f32)

    NP = GDN_HEADS // 2
    W2 = 2 * L
    nbc, npair = bb * nc, bb * nc * NP
    hi2 = lax.broadcasted_iota(jnp.int32, (1, W2), 1) >= L
    row = lax.broadcasted_iota(jnp.int32, (L, W2), 0)
    col = lax.broadcasted_iota(jnp.int32, (L, W2), 1) % L
    causal2, strict2 = col <= row, col < row
    eye2 = (col == row).astype(f32)
    steps = L.bit_length() - 2

    def head_tiles(col0, hl):
        return jnp.stack([qkv_ref[bi, ci * L:(ci + 1) * L,
                                  col0 + (2 * pp + hl) * GDN_D:col0 + (2 * pp + hl + 1) * GDN_D]
                          for bi in range(bb) for ci in range(nc) for pp in range(NP)])

    def pair_col(x, lane0, hl):
        return jnp.stack([x[:, :, lane0 + 2 * pp + hl:lane0 + 2 * pp + hl + 1] for pp in range(NP)],
                         axis=1).reshape(npair, L, 1)

    gt, cs = _gate_tiles(gates_ref, bb=bb, nc=nc, L=L)
    G = [pair_col(cs, G_LANE, hl) for hl in range(2)]
    beta = [pair_col(gt, B_LANE, hl) for hl in range(2)]
    g_col = jnp.where(hi2, G[1], G[0])
    g_row = _rows_from_cols(g_col, eye2)
    decay = jnp.where(causal2, jnp.exp(jnp.where(causal2, g_col - g_row, 0.0)), 0.0)
    q = [head_tiles(0, hl) for hl in range(2)]
    k = [head_tiles(D_GDN, hl) for hl in range(2)]
    v = [head_tiles(2 * D_GDN, hl) for hl in range(2)]
    kb = [k[hl] * beta[hl] for hl in range(2)]
    zd = jnp.zeros((npair, L, GDN_D), f32)
    lhs = jnp.concatenate([jnp.concatenate(kb, axis=-1), jnp.concatenate(q, axis=-1)], axis=1)
    k_bd = jnp.concatenate([jnp.concatenate([k[0], zd], axis=-1),
                            jnp.concatenate([zd, k[1]], axis=-1)], axis=1)
    kkqk = _bdot_nt(lhs, k_bd)
    kk = kkqk[:, 0:L] * decay
    qk = kkqk[:, L:] * decay
    def block_diag(x):
        return jnp.concatenate([jnp.where(hi2, 0.0, x), jnp.where(hi2, x, 0.0)], axis=1)

    N = jnp.where(strict2, kk, 0.0)
    R = -N
    P = _bdot(N, block_diag(N))
    for step in range(steps):
        if step < steps - 1:
            x = _bdot(jnp.concatenate([P, R], axis=1), block_diag(P))
            R = R + P + x[:, L:]
            P = x[:, 0:L]
        else:
            R = R + P + _bdot(R, block_diag(P))
    eG = [jnp.exp(G[hl]) for hl in range(2)]
    rhs = [jnp.concatenate([v[hl] * beta[hl], kb[hl] * eG[hl]], axis=-1) for hl in range(2)]
    z2 = jnp.zeros((npair, L, 2 * GDN_D), f32)
    w = [rhs[0] + _bdot(R, jnp.concatenate([rhs[0], z2], axis=1)),
         rhs[1] + _bdot(R, jnp.concatenate([z2, rhs[1]], axis=1))]
    qg = [q[hl] * eG[hl] for hl in range(2)]
    g_last = [G[hl][:, L - 1:L, :] for hl in range(2)]
    kdec = [k[hl] * jnp.exp(g_last[hl] - G[hl]) for hl in range(2)]
    e_last = [jnp.exp(g_last[hl]) for hl in range(2)]

    def chunk(xs, ci):
        return jnp.concatenate([x.reshape((bb, nc, NP) + x.shape[1:])[:, ci]
                                .reshape((bb * NP,) + x.shape[1:]) for x in xs], axis=0)

    order = [(bi, 2 * pp + hl) for hl in range(2) for bi in range(bb) for pp in range(NP)]
    nw = nw_ref[...]
    S = jnp.stack([s_ref[bi, h] for bi, h in order])
    zu = jnp.zeros((bb * NP, L, GDN_D), f32)
    for ci in range(nc):
        rs = slice(ci * L, (ci + 1) * L)
        w_c = chunk(w, ci)
        xs = _bdot(jnp.concatenate([w_c[:, :, GDN_D:], chunk(qg, ci)], axis=1), S)
        u = w_c[:, :, 0:GDN_D] - xs[:, 0:L]
        u_bd = jnp.concatenate([jnp.concatenate([u[0:bb * NP], zu], axis=1),
                                jnp.concatenate([zu, u[bb * NP:]], axis=1)], axis=0)
        o = xs[:, L:] + _bdot(chunk([qk, qk], ci), u_bd)
        S = S * chunk(e_last, ci) + _bdot_tn(chunk(kdec, ci), u)
        o = o * lax.rsqrt(jnp.mean(o * o, -1, keepdims=True) + NORM_EPS) * nw
        for i, (bi, h) in enumerate(order):
            hs = slice(h * GDN_D, (h + 1) * GDN_D)
            yb_ref[bi, rs, hs] = o[i] * gz_ref[bi, rs, hs]
    for i, (bi, h) in enumerate(order):
        s_ref[bi, h] = S[i]


def _gdn_call(l, qkv, gates, gz, p, states_in, prev_states, *, bb, nc, L):
    B, T, _ = qkv.shape
    tt = nc * L
    tile = functools.partial(_tile_spec, bb, tt)
    return _stacked_call(
        _gdn_kernel, l=l, B=B, bb=bb, nc=nc, L=L, grid=(B // bb, T // tt),
        inputs=[qkv, gates, gz, p["gdn_norm_w"]],
        in_specs=[tile(D_QKV), tile(LANE), tile(D_GDN), _weight_spec(l, (1, GDN_D))],
        states_in=states_in, prev_states=prev_states, state_tails=[(GDN_HEADS, GDN_D, GDN_D)],
        tile_shapes=[jax.ShapeDtypeStruct((B, T, D_GDN), f32)], tile_specs=[tile(D_GDN)],
        scratch_shapes=[], name="gdn")


def _mlstm_chunks(m4_ref, gates_ref, nw_ref, yc_ref, cn_ref, mp_ref, *, bb, nc, L):
    DH = ML_DH
    NP = ML_HEADS // 2
    nbc, npair = bb * nc, bb * nc * NP
    lane = lax.broadcasted_iota(jnp.int32, (1, LANE), 1)
    hi_half = lane >= DH
    row = lax.broadcasted_iota(jnp.int32, (L, LANE), 0)
    col = lax.broadcasted_iota(jnp.int32, (L, LANE), 1) % DH
    causal2 = col <= row
    eye2 = (col == row).astype(f32)
    r2 = lax.broadcasted_iota(jnp.int32, (LANE, LANE), 0) // DH
    c2 = lax.broadcasted_iota(jnp.int32, (LANE, LANE), 1) // DH
    blk = r2 == c2
    blk2 = jnp.concatenate([blk, blk], axis=1)
    ones_bd = blk.astype(bf16)

    def tiles(col0):
        return jnp.stack([m4_ref[bi, ci * L:(ci + 1) * L, col0 + pp * LANE:col0 + (pp + 1) * LANE]
                          for bi in range(bb) for ci in range(nc) for pp in range(NP)])

    def pair_cols(x, lane0):
        a = jnp.stack([x[:, :, lane0 + 2 * pp:lane0 + 2 * pp + 1] for pp in range(NP)], axis=1)
        b = jnp.stack([x[:, :, lane0 + 2 * pp + 1:lane0 + 2 * pp + 2] for pp in range(NP)], axis=1)
        return jnp.where(hi_half, b, a).reshape(npair, L, LANE)

    gt, cs = _gate_tiles(gates_ref, bb=bb, nc=nc, L=L)
    a_t = pltpu.roll(gt, F_LANE - I_LANE, axis=2) - cs
    A_t = a_t
    rows3 = lax.broadcasted_iota(jnp.int32, a_t.shape, 1)
    sh = 1
    while sh < L:
        A_t = jnp.maximum(A_t, jnp.where(rows3 >= sh, pltpu.roll(A_t, sh, axis=1), -jnp.inf))
        sh *= 2
    F = pair_cols(cs, F_LANE)
    a_p = pair_cols(a_t, F_LANE)
    A_p = pair_cols(A_t, F_LANE)
    Dm = jnp.where(causal2, F + _rows_from_cols(a_p, eye2), -jnp.inf)
    Q = tiles(0)
    K = tiles(D_ML)
    V = tiles(2 * D_ML)
    OG = tiles(3 * D_ML)
    if L < DH:
        zpad = jnp.zeros((npair, DH - L, LANE), f32)
        Kp, Vp = jnp.concatenate([K, zpad], axis=1), jnp.concatenate([V, zpad], axis=1)
    else:
        Kp, Vp = K, V
    k_bd = jnp.concatenate([jnp.where(hi_half, 0.0, Kp), jnp.where(hi_half, Kp, 0.0)], axis=1).astype(bf16)
    v_bd = jnp.concatenate([jnp.where(hi_half, 0.0, Vp), jnp.where(hi_half, Vp, 0.0)], axis=1).astype(bf16)
    vo = jnp.concatenate([v_bd, jnp.broadcast_to(ones_bd, (npair, LANE, LANE))], axis=-1)
    s_raw = jnp.einsum('pik,pjk->pij', Q.astype(bf16), k_bd, preferred_element_type=f32)
    v_one = jnp.concatenate([V, jnp.ones((npair, L, LANE), f32)], axis=-1).astype(bf16)

    def chunk(x, ci):
        return x.reshape((bb, nc, NP) + x.shape[1:])[:, ci].reshape((bb * NP,) + x.shape[1:])

    nw = jnp.stack([nw_ref[0:1, pp * LANE:(pp + 1) * LANE] for pp in range(NP)])
    nw = jnp.broadcast_to(nw[None], (bb, NP, 1, LANE)).reshape(bb * NP, 1, LANE)
    mean_w = (blk.astype(f32) * (1.0 / DH)).astype(bf16)

    def head_mean(x):
        hi = x.astype(bf16)
        lo = (x - hi.astype(f32)).astype(bf16)
        mw = jnp.broadcast_to(mean_w, (x.shape[0], LANE, LANE))
        return (jnp.einsum('pik,pkj->pij', hi, mw, preferred_element_type=f32)
                + jnp.einsum('pik,pkj->pij', lo, mw, preferred_element_type=f32))

    CN = cn_ref[...]
    m = mp_ref[:, 0:1, :]
    for ci in range(nc):
        F_c, A_c, a_c = chunk(F, ci), chunk(A_p, ci), chunk(a_p, ci)
        M = F_c + jnp.maximum(m, A_c)
        wD = jnp.exp(chunk(Dm, ci) - M)
        wI = jnp.exp(F_c + m - M)
        sc = chunk(s_raw, ci) * wD
        numq = (jnp.concatenate([wI, wI], axis=-1) * _bdot(chunk(Q, ci), CN)
                + jnp.einsum('pik,pkj->pij', sc.astype(bf16), chunk(vo, ci), preferred_element_type=f32))
        hh = numq[:, :, 0:LANE] / jnp.maximum(jnp.abs(numq[:, :, LANE:]), jnp.exp(-M))
        m_new = M[:, L - 1:L, :]
        f_last = F_c[:, L - 1:L, :]
        wk = jnp.exp(f_last - F_c + (a_c + F_c) - m_new)
        dc = jnp.exp(f_last + m - m_new)
        kw = chunk(K, ci) * wk
        upd = jnp.einsum('pki,pkj->pij', kw.astype(bf16), chunk(v_one, ci), preferred_element_type=f32)
        CN = jnp.concatenate([dc, dc], axis=-1) * CN + jnp.where(blk2, upd, 0.0)
        m = m_new
        hh = chunk(OG, ci) * hh
        hc = hh - head_mean(hh)
        var = head_mean(hc * hc)
        y = (hc * lax.rsqrt(var + LN_EPS) * nw).reshape(bb, NP, L, LANE)
        for bi in range(bb):
            for pp in range(NP):
                yc_ref[bi, ci * L:(ci + 1) * L, pp * LANE:(pp + 1) * LANE] = y[bi, pp]
    cn_ref[...] = CN
    mp_ref[:, 0:1, :] = m


def _mlstm_state_in(c0_ref, n0_ref, m0_ref, cn_ref, mp_ref, *, bb):
    DH, NP = ML_DH, ML_HEADS // 2
    cn_ref[...] = jnp.zeros(cn_ref.shape, f32)
    if c0_ref is None:
        mp_ref[...] = jnp.zeros(mp_ref.shape, f32)
        return
    lane = lax.broadcasted_iota(jnp.int32, (1, LANE), 1)
    for bi in range(bb):
        for pp in range(NP):
            for hl in range(2):
                h = 2 * pp + hl
                rs, ls = slice(hl * DH, (hl + 1) * DH), slice(hl * DH, (hl + 1) * DH)
                cn_ref[bi * NP + pp, rs, ls] = c0_ref[bi, h]
                n_col = jnp.broadcast_to(n0_ref[bi, h:h + 1, :], (DH, DH)).T
                cn_ref[bi * NP + pp, rs, LANE + hl * DH:LANE + (hl + 1) * DH] = n_col
            m_a = m0_ref[bi, 0:1, 2 * pp:2 * pp + 1]
            m_b = m0_ref[bi, 0:1, 2 * pp + 1:2 * pp + 2]
            mp_ref[bi * NP + pp, 0:1, :] = jnp.where(lane >= DH, m_b, m_a)


def _mlstm_state_out(cn_ref, mp_ref, c_ref, n_ref, m_ref, *, bb):
    DH, NP = ML_DH, ML_HEADS // 2
    for bi in range(bb):
        for pp in range(NP):
            for hl in range(2):
                h = 2 * pp + hl
                rs = slice(hl * DH, (hl + 1) * DH)
                c_ref[bi, h] = cn_ref[bi * NP + pp, rs, hl * DH:(hl + 1) * DH]
                n_blk = cn_ref[bi * NP + pp, rs, LANE + hl * DH:LANE + (hl + 1) * DH]
                n_ref[bi, h:h + 1, :] = n_blk.T[0:1, :]
                m_ref[bi, 0:1, h:h + 1] = mp_ref[bi * NP + pp, 0:1, hl * DH:hl * DH + 1]


def _mlstm_kernel(*refs, l, first, bb, nc, L, has_state, n_prev):
    m4_ref, gates_ref, nw_ref = refs[:3]
    refs = refs[3:]
    c0_ref = n0_ref = m0_ref = None
    if has_state:
        c0_ref, n0_ref, m0_ref = refs[:3]
        refs = refs[3:]
    yc_ref, c_ref, n_ref, m_ref, cn_ref, mp_ref = refs[n_prev:]
    c_ref, n_ref, m_ref = _own_layer([c_ref, n_ref, m_ref], l, first)
    t = pl.program_id(1)

    @pl.when(t == 0)
    def _():
        _mlstm_state_in(c0_ref, n0_ref, m0_ref, cn_ref, mp_ref, bb=bb)

    _mlstm_chunks(m4_ref, gates_ref, nw_ref, yc_ref, cn_ref, mp_ref, bb=bb, nc=nc, L=L)

    @pl.when(t == pl.num_programs(1) - 1)
    def _():
        _mlstm_state_out(cn_ref, mp_ref, c_ref, n_ref, m_ref, bb=bb)


def _mlstm_call(l, m4, gates, p, states_in, prev_states, *, bb, nc, L):
    B, T, _ = m4.shape
    tt = nc * L
    tile = functools.partial(_tile_spec, bb, tt)
    npair = bb * (ML_HEADS // 2)
    return _stacked_call(
        _mlstm_kernel, l=l, B=B, bb=bb, nc=nc, L=L, grid=(B // bb, T // tt),
        inputs=[m4, gates, p["ml_norm_w"]],
        in_specs=[tile(4 * D_ML), tile(LANE), _weight_spec(l, (1, D_ML))],
        states_in=states_in, prev_states=prev_states,
        state_tails=[(ML_HEADS, ML_DH, ML_DH), (ML_HEADS, ML_DH), (1, ML_HEADS)],
        tile_shapes=[jax.ShapeDtypeStruct((B, T, D_ML), f32)], tile_specs=[tile(D_ML)],
        scratch_shapes=[pltpu.VMEM((npair, LANE, 2 * LANE), f32), pltpu.VMEM((npair, 8, LANE), f32)],
        name="mlstm")


def _out_ffn_kernel(*refs, l, first, bb, tt, has_state, n_prev):
    (x_ref, ya_ref, yb_ref, yc_ref, wo_ref, ln1_ref, wup_ref, fcw_ref, wdn_ref, ln2_ref) = refs[:10]
    refs = refs[10:]
    if has_state:
        hf_ref = refs[0]
        refs = refs[1:]
    xo_ref, tf_ref, sf_ref = refs[n_prev:]
    (tf_ref,) = _own_layer([tf_ref], l, first)
    t = pl.program_id(1)

    @pl.when(t == 0)
    def _():
        if has_state:
            sf_ref[:, HIST - 2:HIST, :] = hf_ref[...]
        else:
            sf_ref[:, 0:HIST, :] = jnp.zeros((bb, HIST, D_FF), f32)

    rows = bb * tt
    x = x_ref[...].reshape(rows, D_MODEL)
    mix = (jnp.dot(ya_ref[...].reshape(rows, D_CONV).astype(bf16), wo_ref[0:D_CONV, :],
                   preferred_element_type=f32)
           + jnp.dot(yb_ref[...].reshape(rows, D_GDN).astype(bf16), wo_ref[D_CONV:D_CONV + D_GDN, :],
                     preferred_element_type=f32)
           + jnp.dot(yc_ref[...].reshape(rows, D_ML).astype(bf16), wo_ref[D_CONV + D_GDN:, :],
                     preferred_element_type=f32))
    x1 = _layer_norm(ALPHA * x + mix, ln1_ref[0:1, :], ln1_ref[1:2, :])
    x1b = x1.astype(bf16)

    acc = None
    for lo, hi in FF_CHUNKS:
        gate = jnp.dot(x1b, wup_ref[:, lo:hi], preferred_element_type=f32)
        val = jnp.dot(x1b, wup_ref[:, D_FF + lo:D_FF + hi], preferred_element_type=f32)
        sf_ref[:, HIST:, lo:hi] = gate.reshape(bb, tt, hi - lo)
        conv = fcw_ref[0:1, lo:hi] * sf_ref[:, pl.ds(HIST - 2, tt), lo:hi]
        for j in range(1, 3):
            conv = conv + fcw_ref[j:j + 1, lo:hi] * sf_ref[:, pl.ds(HIST - 2 + j, tt), lo:hi]
        hmid = jax.nn.silu(conv).reshape(rows, hi - lo) * val
        part = jnp.dot(hmid.astype(bf16), wdn_ref[lo:hi, :], preferred_element_type=f32)
        acc = part if acc is None else acc + part
    tf_ref[...] = sf_ref[:, tt + HIST - 2:tt + HIST, :]
    sf_ref[:, 0:HIST, :] = sf_ref[:, tt:tt + HIST, :]
    x2 = _layer_norm(ALPHA * x1 + acc, ln2_ref[0:1, :], ln2_ref[1:2, :])
    xo_ref[...] = x2.reshape(bb, tt, D_MODEL)


def _out_ffn_call(l, x, ya, yb, yc, p, states_in, prev_states, *, bb, tt):
    B, T, _ = x.shape
    tile = functools.partial(_tile_spec, bb, tt)
    return _stacked_call(
        _out_ffn_kernel, l=l, B=B, bb=bb, tt=tt, grid=(B // bb, T // tt),
        inputs=[x, ya, yb, yc, p["w_o"], p["ln1"], p["w_up"], p["ffn_conv_w"], p["w_down"], p["ln2"]],
        in_specs=[tile(D_MODEL), tile(D_CONV), tile(D_GDN), tile(D_ML),
                  _weight_spec(l, (D_MODEL, D_MODEL)), _weight_spec(l, (2, D_MODEL)),
                  _weight_spec(l, (D_MODEL, 2 * D_FF)), _weight_spec(l, (3, D_FF)),
                  _weight_spec(l, (D_FF, D_MODEL)), _weight_spec(l, (2, D_MODEL))],
        states_in=states_in, prev_states=prev_states, state_tails=[(2, D_FF)],
        tile_shapes=[jax.ShapeDtypeStruct((B, T, D_MODEL), f32)], tile_specs=[tile(D_MODEL)],
        scratch_shapes=[pltpu.VMEM((bb, tt + HIST, D_FF), f32)], name="out_ffn")


def _prep_params(w_in, conv_w, gdn_conv_w, gdn_a_log, gdn_dt_bias, gdn_norm_w,
                 ml_i_bias, ml_f_bias, ml_norm_w, w_o, ln1_g, ln1_b,
                 w_up, ffn_conv_w, w_down, ln2_g, ln2_b):
    gate_cols = jnp.concatenate([w_in[:, :, COL_GA:COL_MQ], w_in[:, :, COL_MI:D_IN]], axis=2)
    gate_cols = jnp.pad(gate_cols, ((0, 0), (0, 0), (0, LANE - gate_cols.shape[2])))
    w_tail = jnp.concatenate([w_in[:, :, COL_MQ:COL_MI], gate_cols], axis=2).astype(bf16)
    zeros4 = jnp.zeros((DEPTH, 4), f32)
    bias_row = jnp.pad(jnp.concatenate([gdn_dt_bias, zeros4, ml_i_bias, ml_f_bias], axis=1),
                       ((0, 0), (0, LANE - 16)))
    alog_row = jnp.pad(gdn_a_log, ((0, 0), (0, LANE - GDN_HEADS)))
    gate_p = jnp.zeros((DEPTH, 8, LANE), f32).at[:, 0].set(bias_row).at[:, 1].set(alog_row)
    return dict(
        w_in=w_in.astype(bf16), w_tail=w_tail, conv_w=conv_w, gdn_conv_w=gdn_conv_w, gate_p=gate_p,
        gdn_norm_w=gdn_norm_w.reshape(DEPTH, 1, GDN_D), ml_norm_w=ml_norm_w.reshape(DEPTH, 1, D_ML),
        w_o=w_o.astype(bf16), ln1=jnp.stack([ln1_g, ln1_b], axis=1),
        w_up=w_up.astype(bf16), ffn_conv_w=ffn_conv_w, w_down=w_down.astype(bf16),
        ln2=jnp.stack([ln2_g, ln2_b], axis=1),
    )


def _tiling(B, T):
    L = min(CHUNK, T)
    if T >= 512:
        return dict(bb=1, tt=256, fbb=1, ftt=512, L=L, rbb=min(B, 8), gnc=1, mnc=2)
    return dict(bb=256 // T, tt=T, fbb=256 // T, ftt=T, L=L, rbb=min(B, 16), gnc=T // L, mnc=T // L)


def _trunk(x, states, p):
    B, T, _ = x.shape
    c = _tiling(B, T)
    conv_new = gdn_new = ml_new = ffn_new = None
    for l in range(DEPTH):
        sel = lambda idx: None if states is None else [states[i] for i in idx]
        ya, qkv, gz, m4, gates, *conv_new = _proj_call(l, x, p, sel((0, 1)), conv_new,
                                                       bb=c["bb"], tt=c["tt"])
        yb, *gdn_new = _gdn_call(l, qkv, gates, gz, p, sel((2,)), gdn_new,
                                 bb=c["rbb"], nc=c["gnc"], L=c["L"])
        yc, *ml_new = _mlstm_call(l, m4, gates, p, sel((3, 4, 5)), ml_new,
                                  bb=c["rbb"], nc=c["mnc"], L=c["L"])
        x, *ffn_new = _out_ffn_call(l, x, ya, yb, yc, p, sel((6,)), ffn_new,
                                    bb=c["fbb"], tt=c["ftt"])
    return x, (conv_new[0], conv_new[1], gdn_new[0], ml_new[0], ml_new[1],
               ml_new[2].reshape(DEPTH, B, ML_HEADS), ffn_new[0])


def kernel(x_prompt, x_sample, state_conv_mix, state_gdn_conv, state_gdn, state_mlstm_c,
           state_mlstm_n, state_mlstm_m, state_ffn_conv,
           w_in, conv_w, gdn_conv_w, gdn_a_log, gdn_dt_bias, gdn_norm_w,
           ml_i_bias, ml_f_bias, ml_norm_w, w_o, ln1_g, ln1_b,
           w_up, ffn_conv_w, w_down, ln2_g, ln2_b):
    p = _prep_params(w_in, conv_w, gdn_conv_w, gdn_a_log, gdn_dt_bias, gdn_norm_w,
                     ml_i_bias, ml_f_bias, ml_norm_w, w_o, ln1_g, ln1_b,
                     w_up, ffn_conv_w, w_down, ln2_g, ln2_b)
    Bs = x_sample.shape[0]
    y_prompt, p_new = _trunk(x_prompt, None, p)
    s_states = (state_conv_mix, state_gdn_conv, state_gdn, state_mlstm_c, state_mlstm_n,
                state_mlstm_m.reshape(DEPTH, Bs, 1, ML_HEADS), state_ffn_conv)
    y_sample, s_new = _trunk(x_sample, s_states, p)
    return (y_prompt, y_sample, *p_new, *s_new)
```

```python
import functools
import itertools

import jax
import jax.numpy as jnp
from jax import lax
from jax.experimental import pallas as pl
from jax.experimental.pallas import tpu as pltpu

f32 = jnp.float32
bf16 = jnp.bfloat16

D_MODEL = 1024
DEPTH = 2
D_CONV = 256
GDN_HEADS = 4
GDN_D = 128
D_GDN = GDN_HEADS * GDN_D
D_QKV = 3 * D_GDN
ML_HEADS = 4
ML_DH = 64
D_ML = ML_HEADS * ML_DH
D_FF = 2816
CHUNK = 64
ALPHA = (2.0 * DEPTH) ** 0.25
LN_EPS = 1e-5
NORM_EPS = 1e-6

LANE = 128
HIST = 8
COL_A = 0
COL_G = 3 * D_CONV
COL_GA = COL_G + 4 * D_GDN
COL_MQ = COL_GA + 2 * GDN_HEADS
COL_MI = COL_MQ + 4 * D_ML
D_IN = COL_MI + 2 * ML_HEADS
D_TAIL = 4 * D_ML + LANE
G_LANE, B_LANE, I_LANE, F_LANE = 0, 4, 8, 12
MXU_DIM = 256
FF_CHUNKS = ((0, 6 * MXU_DIM), (6 * MXU_DIM, D_FF))
VMEM_LIMIT = 56 * 1024 * 1024
_PARAMS = pltpu.CompilerParams(dimension_semantics=("arbitrary", "arbitrary"),
                               vmem_limit_bytes=VMEM_LIMIT)


def _softplus(z):
    return jnp.maximum(z, 0.0) + jnp.log1p(jnp.exp(-jnp.abs(z)))


def _silu(x):
    h = 0.5 * x
    return h + h * jnp.tanh(h)


def _causal_dwconv(xh, w):
    width = w.shape[0]
    y = w[width - 1:width, :] * xh[:, HIST:, :]
    for j in range(width - 1):
        y = y + w[j:j + 1, :] * pltpu.roll(xh, width - 1 - j, axis=1)[:, HIST:, :]
    return y


def _layer_norm(x, g, b):
    mu = jnp.mean(x, -1, keepdims=True)
    xc = x - mu
    var = jnp.mean(xc * xc, -1, keepdims=True)
    return xc * lax.rsqrt(var + LN_EPS) * g + b


def _tile_spec(bb, tt, c):
    return pl.BlockSpec((bb, tt, c), lambda b, t: (b, t, 0))


def _state_spec(l, bb, tail, every_layer=False):
    if every_layer:
        return pl.BlockSpec((DEPTH, bb) + tail, lambda b, t: (0, b) + (0,) * len(tail))
    return pl.BlockSpec((None, bb) + tail, lambda b, t: (l, b) + (0,) * len(tail))


def _own_layer(refs, l, first):
    if not first:
        return refs

    @pl.when(pl.program_id(1) == 0)
    def _():
        for r in refs:
            for j in range(DEPTH):
                if j != l:
                    r[j] = jnp.zeros(r.shape[1:], f32)

    return [r.at[l] for r in refs]


def _weight_spec(l, tail):
    return pl.BlockSpec((None,) + tail, lambda b, t: (l,) + (0,) * len(tail),
                        pipeline_mode=pl.Buffered(1))


def _stacked_call(kernel, *, l, B, bb, grid, inputs, in_specs, states_in, prev_states, state_tails,
                  tile_shapes, tile_specs, scratch_shapes, name, **static):
    first = prev_states is None
    args = list(inputs)
    specs = list(in_specs)
    if states_in is not None:
        args += list(states_in)
        specs += [_state_spec(l, bb, tail) for tail in state_tails]
    aliases = {}
    if not first:
        for i, p in enumerate(prev_states):
            aliases[len(args)] = len(tile_shapes) + i
            args.append(p)
            specs.append(pl.BlockSpec(memory_space=pl.ANY))
    return pl.pallas_call(
        functools.partial(kernel, l=l, first=first, has_state=states_in is not None,
                          n_prev=0 if first else len(prev_states), bb=bb, **static),
        grid=grid, in_specs=specs,
        out_specs=tuple(tile_specs) + tuple(_state_spec(l, bb, tail, every_layer=first)
                                            for tail in state_tails),
        out_shape=tuple(tile_shapes) + tuple(jax.ShapeDtypeStruct((DEPTH, B) + tail, f32)
                                             for tail in state_tails),
        scratch_shapes=scratch_shapes, input_output_aliases=aliases,
        compiler_params=_PARAMS, name=name)(*args)


def _proj_kernel(*refs, l, first, bb, tt, has_state, n_prev):
    x_ref, w_ref, wt_ref, cw_ref, gcw_ref, gp_ref = refs[:6]
    refs = refs[6:]
    if has_state:
        ha_ref, hg_ref = refs[:2]
        refs = refs[2:]
    ya_ref, qkv_ref, gz_ref, m4_ref, gates_ref, ta_ref, tg_ref, sa_ref, sg_ref = refs[n_prev:]
    ta_ref, tg_ref = _own_layer([ta_ref, tg_ref], l, first)
    t = pl.program_id(1)

    @pl.when(t == 0)
    def _():
        if has_state:
            sa_ref[:, HIST - 2:HIST, :] = ha_ref[...]
            sg_ref[:, HIST - 3:HIST, :] = hg_ref[...]
        else:
            sa_ref[:, 0:HIST, :] = jnp.zeros((bb, HIST, D_CONV), f32)
            sg_ref[:, 0:HIST, :] = jnp.zeros((bb, HIST, D_QKV), f32)

    rows = bb * tt
    xb = x_ref[...].reshape(rows, D_MODEL).astype(bf16)

    def post_a(pa):
        a_b = pa[:, 0:D_CONV]
        sa_ref[:, HIST:, :] = (pa[:, D_CONV:2 * D_CONV] * pa[:, 2 * D_CONV:3 * D_CONV]).reshape(bb, tt, D_CONV)
        conv = _causal_dwconv(sa_ref[...], cw_ref[...])
        ya_ref[...] = a_b.reshape(bb, tt, D_CONV) * conv
        ta_ref[...] = sa_ref[:, tt + HIST - 2:tt + HIST, :]
        sa_ref[:, 0:HIST, :] = sa_ref[:, tt:tt + HIST, :]

    def post_qkv(c0, pg):
        w = pg.shape[1]
        sg_ref[:, HIST:, c0:c0 + w] = pg.reshape(bb, tt, w)
        for cb in range(c0 // LANE, (c0 + w) // LANE):
            cs = slice(cb * LANE, (cb + 1) * LANE)
            c = _silu(_causal_dwconv(sg_ref[:, :, cs], gcw_ref[:, cs]))
            if cb < 2 * GDN_HEADS:
                c = c * lax.rsqrt(jnp.sum(c * c, -1, keepdims=True) + NORM_EPS)
                if cb < GDN_HEADS:
                    c = c * (GDN_D ** -0.5)
            qkv_ref[:, :, cs] = c
        tg_ref[:, :, c0:c0 + w] = sg_ref[:, tt + HIST - 3:tt + HIST, c0:c0 + w]
        sg_ref[:, 0:HIST, c0:c0 + w] = sg_ref[:, tt:tt + HIST, c0:c0 + w]

    def post_gz(c0, pz):
        gz_ref[:, :, c0:c0 + pz.shape[1]] = _silu(pz).reshape(bb, tt, pz.shape[1])

    def post_m(part, pm):
        if part == 1:
            pm = pm * (ML_DH ** -0.5)
        elif part == 3:
            pm = jax.nn.sigmoid(pm)
        m4_ref[:, :, part * D_ML:(part + 1) * D_ML] = pm.reshape(bb, tt, D_ML)

    def post_gates(z):
        z = z + gp_ref[0:1, :]
        lane = lax.broadcasted_iota(jnp.int32, z.shape, 1)
        g = -jnp.exp(gp_ref[1:2, :]) * _softplus(z)
        gates = jnp.where(lane < B_LANE, g,
                          jnp.where(lane < I_LANE, jax.nn.sigmoid(z),
                                    jnp.where(lane < F_LANE, z,
                                              jnp.where(lane < F_LANE + ML_HEADS, -_softplus(-z), 0.0))))
        gates_ref[...] = gates.reshape(bb, tt, LANE)

    W = MXU_DIM
    qkv = [(w_ref, COL_G + c, COL_G + c + W, functools.partial(post_qkv, c)) for c in range(0, D_QKV, W)]
    gz = [(w_ref, COL_G + D_QKV + c, COL_G + D_QKV + c + W, functools.partial(post_gz, c))
          for c in range(0, D_GDN, W)]
    mm = [(wt_ref, i * D_ML, (i + 1) * D_ML, functools.partial(post_m, i)) for i in range(4)]
    light = [(w_ref, COL_A, COL_G, post_a)] + mm + gz + [(wt_ref, 4 * D_ML, 4 * D_ML + LANE, post_gates)]
    tasks = []
    for i in range(max(len(qkv), len(light))):
        tasks += qkv[i:i + 1] + light[i:i + 1]
    pending = None
    for ref, lo, hi, post in tasks:
        res = jnp.dot(xb, ref[:, lo:hi], preferred_element_type=f32)
        if pending is not None:
            pending[0](pending[1])
        pending = (post, res)
    pending[0](pending[1])


def _proj_call(l, x, p, states_in, prev_states, *, bb, tt):
    B, T, _ = x.shape
    tile = functools.partial(_tile_spec, bb, tt)
    widths = (D_CONV, D_QKV, D_GDN, 4 * D_ML, LANE)
    return _stacked_call(
        _proj_kernel, l=l, B=B, bb=bb, tt=tt, grid=(B // bb, T // tt),
        inputs=[x, p["w_in"], p["w_tail"], p["conv_w"], p["gdn_conv_w"], p["gate_p"]],
        in_specs=[tile(D_MODEL), _weight_spec(l, (D_MODEL, D_IN)), _weight_spec(l, (D_MODEL, D_TAIL)),
                  _weight_spec(l, (3, D_CONV)), _weight_spec(l, (4, D_QKV)), _weight_spec(l, (8, LANE))],
        states_in=states_in, prev_states=prev_states, state_tails=[(2, D_CONV), (3, D_QKV)],
        tile_shapes=[jax.ShapeDtypeStruct((B, T, w), f32) for w in widths],
        tile_specs=[tile(w) for w in widths],
        scratch_shapes=[pltpu.VMEM((bb, tt + HIST, D_CONV), f32),
                        pltpu.VMEM((bb, tt + HIST, D_QKV), f32)],
        name="in_proj")


def _bdot(a, b):
    return jnp.einsum('pik,pkj->pij', a.astype(bf16), b.astype(bf16), preferred_element_type=f32)


def _bdot_nt(a, b):
    return jnp.einsum('pik,pjk->pij', a.astype(bf16), b.astype(bf16), preferred_element_type=f32)


def _bdot_tn(a, b):
    return jnp.einsum('pki,pkj->pij', a.astype(bf16), b.astype(bf16), preferred_element_type=f32)


def _split3(x):
    hi = x.astype(bf16)
    r1 = x - hi.astype(f32)
    mid = r1.astype(bf16)
    lo = (r1 - mid.astype(f32)).astype(bf16)
    return hi, mid, lo


def _gate_tiles(gates_ref, *, bb, nc, L):
    gt = jnp.stack([gates_ref[bi, ci * L:(ci + 1) * L, :] for bi in range(bb) for ci in range(nc)])
    tril = lax.broadcasted_iota(jnp.int32, (L, L), 1) <= lax.broadcasted_iota(jnp.int32, (L, L), 0)
    trb = jnp.broadcast_to(tril.astype(bf16), (bb * nc, L, L))
    cs = sum(jnp.einsum('pij,pjk->pik', trb, t, preferred_element_type=f32) for t in _split3(gt))
    return gt, cs


def _rows_from_cols(x, eye2):
    L = x.shape[1]
    onesb = jnp.ones((x.shape[0], L, L), bf16)
    return sum(jnp.einsum('pij,pjk->pik', onesb, t, preferred_element_type=f32)
               for t in _split3(x * eye2))


def _gdn_kernel(*refs, l, first, bb, nc, L, has_state, n_prev):
    qkv_ref, gates_ref, gz_ref, nw_ref = refs[:4]
    refs = refs[4:]
    if has_state:
        s0_ref = refs[0]
        refs = refs[1:]
    yb_ref, s_ref = refs[n_prev:]
    (s_ref,) = _own_layer([s_ref], l, first)
    t = pl.program_id(1)

    @pl.when(t == 0)
    def _():
        s_ref[...] = s0_ref[...] if has_state else jnp.zeros(s_ref.shape, f32)

    NP = GDN_HEADS // 2
    W2 = 2 * L
    nbc, npair = bb * nc, bb * nc * NP
    hi2 = lax.broadcasted_iota(jnp.int32, (1, W2), 1) >= L
    row = lax.broadcasted_iota(jnp.int32, (L, W2), 0)
    col = lax.broadcasted_iota(jnp.int32, (L, W2), 1) % L
    causal2, strict2 = col <= row, col < row
    eye2 = (col == row).astype(f32)
    steps = L.bit_length() - 2

    def head_tiles(col0, hl):
        return jnp.stack([qkv_ref[bi, ci * L:(ci + 1) * L,
                                  col0 + (2 * pp + hl) * GDN_D:col0 + (2 * pp + hl + 1) * GDN_D]
                          for bi in range(bb) for ci in range(nc) for pp in range(NP)])

    def pair_col(x, lane0, hl):
        return jnp.stack([x[:, :, lane0 + 2 * pp + hl:lane0 + 2 * pp + hl + 1] for pp in range(NP)],
                         axis=1).reshape(npair, L, 1)

    gt, cs = _gate_tiles(gates_ref, bb=bb, nc=nc, L=L)
    G = [pair_col(cs, G_LANE, hl) for hl in range(2)]
    beta = [pair_col(gt, B_LANE, hl) for hl in range(2)]
    g_col = jnp.where(hi2, G[1], G[0])
    g_row = _rows_from_cols(g_col, eye2)
    decay = jnp.where(causal2, jnp.exp(jnp.where(causal2, g_col - g_row, 0.0)), 0.0)
    q = [head_tiles(0, hl) for hl in range(2)]
    k = [head_tiles(D_GDN, hl) for hl in range(2)]
    v = [head_tiles(2 * D_GDN, hl) for hl in range(2)]
    kb = [k[hl] * beta[hl] for hl in range(2)]
    zd = jnp.zeros((npair, L, GDN_D), f32)
    lhs = jnp.concatenate([jnp.concatenate(kb, axis=-1), jnp.concatenate(q, axis=-1)], axis=1)
    k_bd = jnp.concatenate([jnp.concatenate([k[0], zd], axis=-1),
                            jnp.concatenate([zd, k[1]], axis=-1)], axis=1)
    kkqk = _bdot_nt(lhs, k_bd)
    kk = kkqk[:, 0:L] * decay
    qk = kkqk[:, L:] * decay
    def block_diag(x):
        return jnp.concatenate([jnp.where(hi2, 0.0, x), jnp.where(hi2, x, 0.0)], axis=1)

    N = jnp.where(strict2, kk, 0.0)
    R = -N
    P = _bdot(N, block_diag(N))
    for step in range(steps):
        if step < steps - 1:
            x = _bdot(jnp.concatenate([P, R], axis=1), block_diag(P))
            R = R + P + x[:, L:]
            P = x[:, 0:L]
        else:
            R = R + P + _bdot(R, block_diag(P))
    eG = [jnp.exp(G[hl]) for hl in range(2)]
    rhs = [jnp.concatenate([v[hl] * beta[hl], kb[hl] * eG[hl]], axis=-1) for hl in range(2)]
    z2 = jnp.zeros((npair, L, 2 * GDN_D), f32)
    w = [rhs[0] + _bdot(R, jnp.concatenate([rhs[0], z2], axis=1)),
         rhs[1] + _bdot(R, jnp.concatenate([z2, rhs[1]], axis=1))]
    qg = [q[hl] * eG[hl] for hl in range(2)]
    g_last = [G[hl][:, L - 1:L, :] for hl in range(2)]
    kdec = [k[hl] * jnp.exp(g_last[hl] - G[hl]) for hl in range(2)]
    e_last = [jnp.exp(g_last[hl]) for hl in range(2)]

    def chunk(xs, ci):
        return jnp.concatenate([x.reshape((bb, nc, NP) + x.shape[1:])[:, ci]
                                .reshape((bb * NP,) + x.shape[1:]) for x in xs], axis=0)

    order = [(bi, 2 * pp + hl) for hl in range(2) for bi in range(bb) for pp in range(NP)]
    nw = nw_ref[...]
    S = jnp.stack([s_ref[bi, h] for bi, h in order])
    zu = jnp.zeros((bb * NP, L, GDN_D), f32)
    for ci in range(nc):
        rs = slice(ci * L, (ci + 1) * L)
        w_c = chunk(w, ci)
        xs = _bdot(jnp.concatenate([w_c[:, :, GDN_D:], chunk(qg, ci)], axis=1), S)
        u = w_c[:, :, 0:GDN_D] - xs[:, 0:L]
        u_bd = jnp.concatenate([jnp.concatenate([u[0:bb * NP], zu], axis=1),
                                jnp.concatenate([zu, u[bb * NP:]], axis=1)], axis=0)
        o = xs[:, L:] + _bdot(chunk([qk, qk], ci), u_bd)
        S = S * chunk(e_last, ci) + _bdot_tn(chunk(kdec, ci), u)
        o = o * lax.rsqrt(jnp.mean(o * o, -1, keepdims=True) + NORM_EPS) * nw
        for i, (bi, h) in enumerate(order):
            hs = slice(h * GDN_D, (h + 1) * GDN_D)
            yb_ref[bi, rs, hs] = o[i] * gz_ref[bi, rs, hs]
    for i, (bi, h) in enumerate(order):
        s_ref[bi, h] = S[i]


def _gdn_call(l, qkv, gates, gz, p, states_in, prev_states, *, bb, nc, L):
    B, T, _ = qkv.shape
    tt = nc * L
    tile = functools.partial(_tile_spec, bb, tt)
    return _stacked_call(
        _gdn_kernel, l=l, B=B, bb=bb, nc=nc, L=L, grid=(B // bb, T // tt),
        inputs=[qkv, gates, gz, p["gdn_norm_w"]],
        in_specs=[tile(D_QKV), tile(LANE), tile(D_GDN), _weight_spec(l, (1, GDN_D))],
        states_in=states_in, prev_states=prev_states, state_tails=[(GDN_HEADS, GDN_D, GDN_D)],
        tile_shapes=[jax.ShapeDtypeStruct((B, T, D_GDN), f32)], tile_specs=[tile(D_GDN)],
        scratch_shapes=[], name="gdn")


def _mlstm_chunks(m4_ref, gates_ref, nw_ref, yc_ref, cn_ref, mp_ref, *, bb, nc, L):
    DH = ML_DH
    NP = ML_HEADS // 2
    nbc, npair = bb * nc, bb * nc * NP
    lane = lax.broadcasted_iota(jnp.int32, (1, LANE), 1)
    hi_half = lane >= DH
    row = lax.broadcasted_iota(jnp.int32, (L, LANE), 0)
    col = lax.broadcasted_iota(jnp.int32, (L, LANE), 1) % DH
    causal2 = col <= row
    eye2 = (col == row).astype(f32)
    r2 = lax.broadcasted_iota(jnp.int32, (LANE, LANE), 0) // DH
    c2 = lax.broadcasted_iota(jnp.int32, (LANE, LANE), 1) // DH
    blk = r2 == c2
    blk2 = jnp.concatenate([blk, blk], axis=1)
    ones_bd = blk.astype(bf16)

    def tiles(col0):
        return jnp.stack([m4_ref[bi, ci * L:(ci + 1) * L, col0 + pp * LANE:col0 + (pp + 1) * LANE]
                          for bi in range(bb) for ci in range(nc) for pp in range(NP)])

    def pair_cols(x, lane0):
        a = jnp.stack([x[:, :, lane0 + 2 * pp:lane0 + 2 * pp + 1] for pp in range(NP)], axis=1)
        b = jnp.stack([x[:, :, lane0 + 2 * pp + 1:lane0 + 2 * pp + 2] for pp in range(NP)], axis=1)
        return jnp.where(hi_half, b, a).reshape(npair, L, LANE)

    gt, cs = _gate_tiles(gates_ref, bb=bb, nc=nc, L=L)
    a_t = pltpu.roll(gt, F_LANE - I_LANE, axis=2) - cs
    A_t = a_t
    rows3 = lax.broadcasted_iota(jnp.int32, a_t.shape, 1)
    sh = 1
    while sh < L:
        A_t = jnp.maximum(A_t, jnp.where(rows3 >= sh, pltpu.roll(A_t, sh, axis=1), -jnp.inf))
        sh *= 2
    F = pair_cols(cs, F_LANE)
    a_p = pair_cols(a_t, F_LANE)
    A_p = pair_cols(A_t, F_LANE)
    Dm = jnp.where(causal2, F + _rows_from_cols(a_p, eye2), -jnp.inf)
    Q = tiles(0)
    K = tiles(D_ML)
    V = tiles(2 * D_ML)
    OG = tiles(3 * D_ML)
    if L < DH:
        zpad = jnp.zeros((npair, DH - L, LANE), f32)
        Kp, Vp = jnp.concatenate([K, zpad], axis=1), jnp.concatenate([V, zpad], axis=1)
    else:
        Kp, Vp = K, V
    k_bd = jnp.concatenate([jnp.where(hi_half, 0.0, Kp), jnp.where(hi_half, Kp, 0.0)], axis=1).astype(bf16)
    v_bd = jnp.concatenate([jnp.where(hi_half, 0.0, Vp), jnp.where(hi_half, Vp, 0.0)], axis=1).astype(bf16)
    vo = jnp.concatenate([v_bd, jnp.broadcast_to(ones_bd, (npair, LANE, LANE))], axis=-1)
    s_raw = jnp.einsum('pik,pjk->pij', Q.astype(bf16), k_bd, preferred_element_type=f32)
    v_one = jnp.concatenate([V, jnp.ones((npair, L, LANE), f32)], axis=-1).astype(bf16)

    def chunk(x, ci):
        return x.reshape((bb, nc, NP) + x.shape[1:])[:, ci].reshape((bb * NP,) + x.shape[1:])

    nw = jnp.stack([nw_ref[0:1, pp * LANE:(pp + 1) * LANE] for pp in range(NP)])
    nw = jnp.broadcast_to(nw[None], (bb, NP, 1, LANE)).reshape(bb * NP, 1, LANE)
    mean_w = (blk.astype(f32) * (1.0 / DH)).astype(bf16)

    def head_mean(x):
        hi = x.astype(bf16)
        lo = (x - hi.astype(f32)).astype(bf16)
        mw = jnp.broadcast_to(mean_w, (x.shape[0], LANE, LANE))
        return (jnp.einsum('pik,pkj->pij', hi, mw, preferred_element_type=f32)
                + jnp.einsum('pik,pkj->pij', lo, mw, preferred_element_type=f32))

    CN = cn_ref[...]
    m = mp_ref[:, 0:1, :]
    for ci in range(nc):
        F_c, A_c, a_c = chunk(F, ci), chunk(A_p, ci), chunk(a_p, ci)
        M = F_c + jnp.maximum(m, A_c)
        wD = jnp.exp(chunk(Dm, ci) - M)
        wI = jnp.exp(F_c + m - M)
        sc = chunk(s_raw, ci) * wD
        numq = (jnp.concatenate([wI, wI], axis=-1) * _bdot(chunk(Q, ci), CN)
                + jnp.einsum('pik,pkj->pij', sc.astype(bf16), chunk(vo, ci), preferred_element_type=f32))
        hh = numq[:, :, 0:LANE] / jnp.maximum(jnp.abs(numq[:, :, LANE:]), jnp.exp(-M))
        m_new = M[:, L - 1:L, :]
        f_last = F_c[:, L - 1:L, :]
        wk = jnp.exp(f_last - F_c + (a_c + F_c) - m_new)
        dc = jnp.exp(f_last + m - m_new)
        kw = chunk(K, ci) * wk
        upd = jnp.einsum('pki,pkj->pij', kw.astype(bf16), chunk(v_one, ci), preferred_element_type=f32)
        CN = jnp.concatenate([dc, dc], axis=-1) * CN + jnp.where(blk2, upd, 0.0)
        m = m_new
        hh = chunk(OG, ci) * hh
        hc = hh - head_mean(hh)
        var = head_mean(hc * hc)
        y = (hc * lax.rsqrt(var + LN_EPS) * nw).reshape(bb, NP, L, LANE)
        for bi in range(bb):
            for pp in range(NP):
                yc_ref[bi, ci * L:(ci + 1) * L, pp * LANE:(pp + 1) * LANE] = y[bi, pp]
    cn_ref[...] = CN
    mp_ref[:, 0:1, :] = m


def _mlstm_state_in(c0_ref, n0_ref, m0_ref, cn_ref, mp_ref, *, bb):
    DH, NP = ML_DH, ML_HEADS // 2
    cn_ref[...] = jnp.zeros(cn_ref.shape, f32)
    if c0_ref is None:
        mp_ref[...] = jnp.zeros(mp_ref.shape, f32)
        return
    lane = lax.broadcasted_iota(jnp.int32, (1, LANE), 1)
    for bi in range(bb):
        for pp in range(NP):
            for hl in range(2):
                h = 2 * pp + hl
                rs, ls = slice(hl * DH, (hl + 1) * DH), slice(hl * DH, (hl + 1) * DH)
                cn_ref[bi * NP + pp, rs, ls] = c0_ref[bi, h]
                n_col = jnp.broadcast_to(n0_ref[bi, h:h + 1, :], (DH, DH)).T
                cn_ref[bi * NP + pp, rs, LANE + hl * DH:LANE + (hl + 1) * DH] = n_col
            m_a = m0_ref[bi, 0:1, 2 * pp:2 * pp + 1]
            m_b = m0_ref[bi, 0:1, 2 * pp + 1:2 * pp + 2]
            mp_ref[bi * NP + pp, 0:1, :] = jnp.where(lane >= DH, m_b, m_a)


def _mlstm_state_out(cn_ref, mp_ref, c_ref, n_ref, m_ref, *, bb):
    DH, NP = ML_DH, ML_HEADS // 2
    for bi in range(bb):
        for pp in range(NP):
            for hl in range(2):
                h = 2 * pp + hl
                rs = slice(hl * DH, (hl + 1) * DH)
                c_ref[bi, h] = cn_ref[bi * NP + pp, rs, hl * DH:(hl + 1) * DH]
                n_blk = cn_ref[bi * NP + pp, rs, LANE + hl * DH:LANE + (hl + 1) * DH]
                n_ref[bi, h:h + 1, :] = n_blk.T[0:1, :]
                m_ref[bi, 0:1, h:h + 1] = mp_ref[bi * NP + pp, 0:1, hl * DH:hl * DH + 1]


def _mlstm_kernel(*refs, l, first, bb, nc, L, has_state, n_prev):
    m4_ref, gates_ref, nw_ref = refs[:3]
    refs = refs[3:]
    c0_ref = n0_ref = m0_ref = None
    if has_state:
        c0_ref, n0_ref, m0_ref = refs[:3]
        refs = refs[3:]
    yc_ref, c_ref, n_ref, m_ref, cn_ref, mp_ref = refs[n_prev:]
    c_ref, n_ref, m_ref = _own_layer([c_ref, n_ref, m_ref], l, first)
    t = pl.program_id(1)

    @pl.when(t == 0)
    def _():
        _mlstm_state_in(c0_ref, n0_ref, m0_ref, cn_ref, mp_ref, bb=bb)

    _mlstm_chunks(m4_ref, gates_ref, nw_ref, yc_ref, cn_ref, mp_ref, bb=bb, nc=nc, L=L)

    @pl.when(t == pl.num_programs(1) - 1)
    def _():
        _mlstm_state_out(cn_ref, mp_ref, c_ref, n_ref, m_ref, bb=bb)


def _mlstm_call(l, m4, gates, p, states_in, prev_states, *, bb, nc, L):
    B, T, _ = m4.shape
    tt = nc * L
    tile = functools.partial(_tile_spec, bb, tt)
    npair = bb * (ML_HEADS // 2)
    return _stacked_call(
        _mlstm_kernel, l=l, B=B, bb=bb, nc=nc, L=L, grid=(B // bb, T // tt),
        inputs=[m4, gates, p["ml_norm_w"]],
        in_specs=[tile(4 * D_ML), tile(LANE), _weight_spec(l, (1, D_ML))],
        states_in=states_in, prev_states=prev_states,
        state_tails=[(ML_HEADS, ML_DH, ML_DH), (ML_HEADS, ML_DH), (1, ML_HEADS)],
        tile_shapes=[jax.ShapeDtypeStruct((B, T, D_ML), f32)], tile_specs=[tile(D_ML)],
        scratch_shapes=[pltpu.VMEM((npair, LANE, 2 * LANE), f32), pltpu.VMEM((npair, 8, LANE), f32)],
        name="mlstm")


def _out_ffn_kernel(*refs, l, first, bb, tt, has_state, n_prev):
    (x_ref, ya_ref, yb_ref, yc_ref, wo_ref, ln1_ref, wup_ref, fcw_ref, wdn_ref, ln2_ref) = refs[:10]
    refs = refs[10:]
    if has_state:
        hf_ref = refs[0]
        refs = refs[1:]
    xo_ref, tf_ref, sf_ref = refs[n_prev:]
    (tf_ref,) = _own_layer([tf_ref], l, first)
    t = pl.program_id(1)

    @pl.when(t == 0)
    def _():
        if has_state:
            sf_ref[:, HIST - 2:HIST, :] = hf_ref[...]
        else:
            sf_ref[:, 0:HIST, :] = jnp.zeros((bb, HIST, D_FF), f32)

    def stages(r0, th):
        rows = bb * th
        rs = slice(r0, r0 + th)
        x = x_ref[:, rs, :].reshape(rows, D_MODEL)
        mix = (jnp.dot(ya_ref[:, rs, :].reshape(rows, D_CONV).astype(bf16), wo_ref[0:D_CONV, :],
                       preferred_element_type=f32)
               + jnp.dot(yb_ref[:, rs, :].reshape(rows, D_GDN).astype(bf16),
                         wo_ref[D_CONV:D_CONV + D_GDN, :], preferred_element_type=f32)
               + jnp.dot(yc_ref[:, rs, :].reshape(rows, D_ML).astype(bf16), wo_ref[D_CONV + D_GDN:, :],
                         preferred_element_type=f32))
        yield
        x1 = _layer_norm(ALPHA * x + mix, ln1_ref[0:1, :], ln1_ref[1:2, :])
        x1b = x1.astype(bf16)
        yield
        acc = None
        for lo, hi in FF_CHUNKS:
            gate = jnp.dot(x1b, wup_ref[:, lo:hi], preferred_element_type=f32)
            val = jnp.dot(x1b, wup_ref[:, D_FF + lo:D_FF + hi], preferred_element_type=f32)
            yield
            sf_ref[:, HIST + r0:HIST + r0 + th, lo:hi] = gate.reshape(bb, th, hi - lo)
            conv = _causal_dwconv(sf_ref[:, r0:r0 + HIST + th, lo:hi], fcw_ref[:, lo:hi])
            hmid = _silu(conv).reshape(rows, hi - lo) * val
            yield
            part = jnp.dot(hmid.astype(bf16), wdn_ref[lo:hi, :], preferred_element_type=f32)
            acc = part if acc is None else acc + part
            yield
        x2 = _layer_norm(ALPHA * x1 + acc, ln2_ref[0:1, :], ln2_ref[1:2, :])
        xo_ref[:, rs, :] = x2.reshape(bb, th, D_MODEL)

    nh = 2 if tt % (2 * MXU_DIM) == 0 else 1
    for _ in itertools.zip_longest(*[stages(i * (tt // nh), tt // nh) for i in range(nh)]):
        pass
    tf_ref[...] = sf_ref[:, tt + HIST - 2:tt + HIST, :]
    sf_ref[:, 0:HIST, :] = sf_ref[:, tt:tt + HIST, :]


def _out_ffn_call(l, x, ya, yb, yc, p, states_in, prev_states, *, bb, tt):
    B, T, _ = x.shape
    tile = functools.partial(_tile_spec, bb, tt)
    return _stacked_call(
        _out_ffn_kernel, l=l, B=B, bb=bb, tt=tt, grid=(B // bb, T // tt),
        inputs=[x, ya, yb, yc, p["w_o"], p["ln1"], p["w_up"], p["ffn_conv_w"], p["w_down"], p["ln2"]],
        in_specs=[tile(D_MODEL), tile(D_CONV), tile(D_GDN), tile(D_ML),
                  _weight_spec(l, (D_MODEL, D_MODEL)), _weight_spec(l, (2, D_MODEL)),
                  _weight_spec(l, (D_MODEL, 2 * D_FF)), _weight_spec(l, (3, D_FF)),
                  _weight_spec(l, (D_FF, D_MODEL)), _weight_spec(l, (2, D_MODEL))],
        states_in=states_in, prev_states=prev_states, state_tails=[(2, D_FF)],
        tile_shapes=[jax.ShapeDtypeStruct((B, T, D_MODEL), f32)], tile_specs=[tile(D_MODEL)],
        scratch_shapes=[pltpu.VMEM((bb, tt + HIST, D_FF), f32)], name="out_ffn")


def _prep_params(w_in, conv_w, gdn_conv_w, gdn_a_log, gdn_dt_bias, gdn_norm_w,
                 ml_i_bias, ml_f_bias, ml_norm_w, w_o, ln1_g, ln1_b,
                 w_up, ffn_conv_w, w_down, ln2_g, ln2_b):
    gate_cols = jnp.concatenate([w_in[:, :, COL_GA:COL_MQ], w_in[:, :, COL_MI:D_IN]], axis=2)
    gate_cols = jnp.pad(gate_cols, ((0, 0), (0, 0), (0, LANE - gate_cols.shape[2])))
    w_tail = jnp.concatenate([w_in[:, :, COL_MQ:COL_MI], gate_cols], axis=2).astype(bf16)
    zeros4 = jnp.zeros((DEPTH, 4), f32)
    bias_row = jnp.pad(jnp.concatenate([gdn_dt_bias, zeros4, ml_i_bias, ml_f_bias], axis=1),
                       ((0, 0), (0, LANE - 16)))
    alog_row = jnp.pad(gdn_a_log, ((0, 0), (0, LANE - GDN_HEADS)))
    gate_p = jnp.zeros((DEPTH, 8, LANE), f32).at[:, 0].set(bias_row).at[:, 1].set(alog_row)
    return dict(
        w_in=w_in.astype(bf16), w_tail=w_tail, conv_w=conv_w, gdn_conv_w=gdn_conv_w, gate_p=gate_p,
        gdn_norm_w=gdn_norm_w.reshape(DEPTH, 1, GDN_D), ml_norm_w=ml_norm_w.reshape(DEPTH, 1, D_ML),
        w_o=w_o.astype(bf16), ln1=jnp.stack([ln1_g, ln1_b], axis=1),
        w_up=w_up.astype(bf16), ffn_conv_w=ffn_conv_w, w_down=w_down.astype(bf16),
        ln2=jnp.stack([ln2_g, ln2_b], axis=1),
    )


def _tiling(B, T):
    L = min(CHUNK, T)
    if T >= 512:
        return dict(bb=1, tt=256, fbb=1, ftt=512, L=L, rbb=min(B, 8), gnc=2, mnc=2)
    return dict(bb=256 // T, tt=T, fbb=256 // T, ftt=T, L=L, rbb=min(B, 16), gnc=T // L, mnc=T // L)


def _trunk(x, states, p):
    B, T, _ = x.shape
    c = _tiling(B, T)
    conv_new = gdn_new = ml_new = ffn_new = None
    for l in range(DEPTH):
        sel = lambda idx: None if states is None else [states[i] for i in idx]
        ya, qkv, gz, m4, gates, *conv_new = _proj_call(l, x, p, sel((0, 1)), conv_new,
                                                       bb=c["bb"], tt=c["tt"])
        yb, *gdn_new = _gdn_call(l, qkv, gates, gz, p, sel((2,)), gdn_new,
                                 bb=c["rbb"], nc=c["gnc"], L=c["L"])
        yc, *ml_new = _mlstm_call(l, m4, gates, p, sel((3, 4, 5)), ml_new,
                                  bb=c["rbb"], nc=c["mnc"], L=c["L"])
        x, *ffn_new = _out_ffn_call(l, x, ya, yb, yc, p, sel((6,)), ffn_new,
                                    bb=c["fbb"], tt=c["ftt"])
    return x, (conv_new[0], conv_new[1], gdn_new[0], ml_new[0], ml_new[1],
               ml_new[2].reshape(DEPTH, B, ML_HEADS), ffn_new[0])


def kernel(x_prompt, x_sample, state_conv_mix, state_gdn_conv, state_gdn, state_mlstm_c,
           state_mlstm_n, state_mlstm_m, state_ffn_conv,
           w_in, conv_w, gdn_conv_w, gdn_a_log, gdn_dt_bias, gdn_norm_w,
           ml_i_bias, ml_f_bias, ml_norm_w, w_o, ln1_g, ln1_b,
           w_up, ffn_conv_w, w_down, ln2_g, ln2_b):
    p = _prep_params(w_in, conv_w, gdn_conv_w, gdn_a_log, gdn_dt_bias, gdn_norm_w,
                     ml_i_bias, ml_f_bias, ml_norm_w, w_o, ln1_g, ln1_b,
                     w_up, ffn_conv_w, w_down, ln2_g, ln2_b)
    Bs = x_sample.shape[0]
    y_prompt, p_new = _trunk(x_prompt, None, p)
    s_states = (state_conv_mix, state_gdn_conv, state_gdn, state_mlstm_c, state_mlstm_n,
                state_mlstm_m.reshape(DEPTH, Bs, 1, ML_HEADS), state_ffn_conv)
    y_sample, s_new = _trunk(x_sample, s_states, p)
    return (y_prompt, y_sample, *p_new, *s_new)
```

```python
import functools
import itertools

import jax
import jax.numpy as jnp
from jax import lax
from jax.experimental import pallas as pl
from jax.experimental.pallas import tpu as pltpu

f32 = jnp.float32
bf16 = jnp.bfloat16

D_MODEL = 1024
DEPTH = 2
D_CONV = 256
GDN_HEADS = 4
GDN_D = 128
D_GDN = GDN_HEADS * GDN_D
D_QKV = 3 * D_GDN
ML_HEADS = 4
ML_DH = 64
D_ML = ML_HEADS * ML_DH
D_FF = 2816
CHUNK = 64
ALPHA = (2.0 * DEPTH) ** 0.25
LN_EPS = 1e-5
NORM_EPS = 1e-6

LANE = 128
HIST = 8
COL_A = 0
COL_G = 3 * D_CONV
COL_GA = COL_G + 4 * D_GDN
COL_MQ = COL_GA + 2 * GDN_HEADS
COL_MI = COL_MQ + 4 * D_ML
D_IN = COL_MI + 2 * ML_HEADS
D_TAIL = 4 * D_ML + LANE
G_LANE, B_LANE, I_LANE, F_LANE = 0, 4, 8, 12
MXU_DIM = 256
FF_CHUNKS = ((0, 6 * MXU_DIM), (6 * MXU_DIM, D_FF))
VMEM_LIMIT = 56 * 1024 * 1024
_PARAMS = pltpu.CompilerParams(dimension_semantics=("arbitrary", "arbitrary"),
                               vmem_limit_bytes=VMEM_LIMIT)


def _softplus(z):
    return jnp.maximum(z, 0.0) + jnp.log1p(jnp.exp(-jnp.abs(z)))


def _silu(x):
    h = 0.5 * x
    return h + h * jnp.tanh(h)


def _causal_dwconv(xh, w):
    width = w.shape[0]
    y = w[width - 1:width, :] * xh[:, HIST:, :]
    for j in range(width - 1):
        y = y + w[j:j + 1, :] * pltpu.roll(xh, width - 1 - j, axis=1)[:, HIST:, :]
    return y


def _layer_norm(x, g, b):
    mu = jnp.mean(x, -1, keepdims=True)
    xc = x - mu
    var = jnp.mean(xc * xc, -1, keepdims=True)
    return xc * lax.rsqrt(var + LN_EPS) * g + b


def _tile_spec(bb, tt, c):
    return pl.BlockSpec((bb, tt, c), lambda b, t: (b, t, 0))


def _state_spec(l, bb, tail, every_layer=False):
    if every_layer:
        return pl.BlockSpec((DEPTH, bb) + tail, lambda b, t: (0, b) + (0,) * len(tail))
    return pl.BlockSpec((None, bb) + tail, lambda b, t: (l, b) + (0,) * len(tail))


def _own_layer(refs, l, first):
    if not first:
        return refs

    @pl.when(pl.program_id(1) == 0)
    def _():
        for r in refs:
            for j in range(DEPTH):
                if j != l:
                    r[j] = jnp.zeros(r.shape[1:], f32)

    return [r.at[l] for r in refs]


def _weight_spec(l, tail):
    return pl.BlockSpec((None,) + tail, lambda b, t: (l,) + (0,) * len(tail),
                        pipeline_mode=pl.Buffered(1))


def _stacked_call(kernel, *, l, B, bb, grid, inputs, in_specs, states_in, prev_states, state_tails,
                  tile_shapes, tile_specs, scratch_shapes, name, **static):
    first = prev_states is None
    args = list(inputs)
    specs = list(in_specs)
    if states_in is not None:
        args += list(states_in)
        specs += [_state_spec(l, bb, tail) for tail in state_tails]
    aliases = {}
    if not first:
        for i, p in enumerate(prev_states):
            aliases[len(args)] = len(tile_shapes) + i
            args.append(p)
            specs.append(pl.BlockSpec(memory_space=pl.ANY))
    return pl.pallas_call(
        functools.partial(kernel, l=l, first=first, has_state=states_in is not None,
                          n_prev=0 if first else len(prev_states), bb=bb, **static),
        grid=grid, in_specs=specs,
        out_specs=tuple(tile_specs) + tuple(_state_spec(l, bb, tail, every_layer=first)
                                            for tail in state_tails),
        out_shape=tuple(tile_shapes) + tuple(jax.ShapeDtypeStruct((DEPTH, B) + tail, f32)
                                             for tail in state_tails),
        scratch_shapes=scratch_shapes, input_output_aliases=aliases,
        compiler_params=_PARAMS, name=name)(*args)


def _proj_kernel(*refs, l, first, bb, tt, has_state, n_prev):
    x_ref, w_ref, wt_ref, cw_ref, gcw_ref, gp_ref = refs[:6]
    refs = refs[6:]
    if has_state:
        ha_ref, hg_ref = refs[:2]
        refs = refs[2:]
    ya_ref, qkv_ref, gz_ref, m4_ref, gates_ref, ta_ref, tg_ref, sa_ref, sg_ref = refs[n_prev:]
    ta_ref, tg_ref = _own_layer([ta_ref, tg_ref], l, first)
    t = pl.program_id(1)

    @pl.when(t == 0)
    def _():
        if has_state:
            sa_ref[:, HIST - 2:HIST, :] = ha_ref[...]
            sg_ref[:, HIST - 3:HIST, :] = hg_ref[...]
        else:
            sa_ref[:, 0:HIST, :] = jnp.zeros((bb, HIST, D_CONV), f32)
            sg_ref[:, 0:HIST, :] = jnp.zeros((bb, HIST, D_QKV), f32)

    rows = bb * tt
    xb = x_ref[...].reshape(rows, D_MODEL).astype(bf16)

    def post_a(pa):
        a_b = pa[:, 0:D_CONV]
        sa_ref[:, HIST:, :] = (pa[:, D_CONV:2 * D_CONV] * pa[:, 2 * D_CONV:3 * D_CONV]).reshape(bb, tt, D_CONV)
        conv = _causal_dwconv(sa_ref[...], cw_ref[...])
        ya_ref[...] = a_b.reshape(bb, tt, D_CONV) * conv
        ta_ref[...] = sa_ref[:, tt + HIST - 2:tt + HIST, :]
        sa_ref[:, 0:HIST, :] = sa_ref[:, tt:tt + HIST, :]

    def post_qkv(c0, pg):
        w = pg.shape[1]
        sg_ref[:, HIST:, c0:c0 + w] = pg.reshape(bb, tt, w)
        for cb in range(c0 // LANE, (c0 + w) // LANE):
            cs = slice(cb * LANE, (cb + 1) * LANE)
            c = _silu(_causal_dwconv(sg_ref[:, :, cs], gcw_ref[:, cs]))
            if cb < 2 * GDN_HEADS:
                c = c * lax.rsqrt(jnp.sum(c * c, -1, keepdims=True) + NORM_EPS)
                if cb < GDN_HEADS:
                    c = c * (GDN_D ** -0.5)
            qkv_ref[:, :, cs] = c
        tg_ref[:, :, c0:c0 + w] = sg_ref[:, tt + HIST - 3:tt + HIST, c0:c0 + w]
        sg_ref[:, 0:HIST, c0:c0 + w] = sg_ref[:, tt:tt + HIST, c0:c0 + w]

    def post_gz(c0, pz):
        gz_ref[:, :, c0:c0 + pz.shape[1]] = _silu(pz).reshape(bb, tt, pz.shape[1])

    def post_m(part, pm):
        if part == 1:
            pm = pm * (ML_DH ** -0.5)
        elif part == 3:
            pm = jax.nn.sigmoid(pm)
        m4_ref[:, :, part * D_ML:(part + 1) * D_ML] = pm.reshape(bb, tt, D_ML)

    def post_gates(z):
        z = z + gp_ref[0:1, :]
        lane = lax.broadcasted_iota(jnp.int32, z.shape, 1)
        g = -jnp.exp(gp_ref[1:2, :]) * _softplus(z)
        gates = jnp.where(lane < B_LANE, g,
                          jnp.where(lane < I_LANE, jax.nn.sigmoid(z),
                                    jnp.where(lane < F_LANE, z,
                                              jnp.where(lane < F_LANE + ML_HEADS, -_softplus(-z), 0.0))))
        gates_ref[...] = gates.reshape(bb, tt, LANE)

    W = MXU_DIM
    qkv = [(w_ref, COL_G + c, COL_G + c + W, functools.partial(post_qkv, c)) for c in range(0, D_QKV, W)]
    gz = [(w_ref, COL_G + D_QKV + c, COL_G + D_QKV + c + W, functools.partial(post_gz, c))
          for c in range(0, D_GDN, W)]
    mm = [(wt_ref, i * D_ML, (i + 1) * D_ML, functools.partial(post_m, i)) for i in range(4)]
    light = [(w_ref, COL_A, COL_G, post_a)] + mm + gz + [(wt_ref, 4 * D_ML, 4 * D_ML + LANE, post_gates)]
    tasks = []
    for i in range(max(len(qkv), len(light))):
        tasks += qkv[i:i + 1] + light[i:i + 1]
    pending = None
    for ref, lo, hi, post in tasks:
        res = jnp.dot(xb, ref[:, lo:hi], preferred_element_type=f32)
        if pending is not None:
            pending[0](pending[1])
        pending = (post, res)
    pending[0](pending[1])


def _proj_call(l, x, p, states_in, prev_states, *, bb, tt):
    B, T, _ = x.shape
    tile = functools.partial(_tile_spec, bb, tt)
    widths = (D_CONV, D_QKV, D_GDN, 4 * D_ML, LANE)
    return _stacked_call(
        _proj_kernel, l=l, B=B, bb=bb, tt=tt, grid=(B // bb, T // tt),
        inputs=[x, p["w_in"], p["w_tail"], p["conv_w"], p["gdn_conv_w"], p["gate_p"]],
        in_specs=[tile(D_MODEL), _weight_spec(l, (D_MODEL, D_IN)), _weight_spec(l, (D_MODEL, D_TAIL)),
                  _weight_spec(l, (3, D_CONV)), _weight_spec(l, (4, D_QKV)), _weight_spec(l, (8, LANE))],
        states_in=states_in, prev_states=prev_states, state_tails=[(2, D_CONV), (3, D_QKV)],
        tile_shapes=[jax.ShapeDtypeStruct((B, T, w), f32) for w in widths],
        tile_specs=[tile(w) for w in widths],
        scratch_shapes=[pltpu.VMEM((bb, tt + HIST, D_CONV), f32),
                        pltpu.VMEM((bb, tt + HIST, D_QKV), f32)],
        name="in_proj")


def _bdot(a, b):
    return jnp.einsum('pik,pkj->pij', a.astype(bf16), b.astype(bf16), preferred_element_type=f32)


def _bdot_nt(a, b):
    return jnp.einsum('pik,pjk->pij', a.astype(bf16), b.astype(bf16), preferred_element_type=f32)


def _bdot_tn(a, b):
    return jnp.einsum('pki,pkj->pij', a.astype(bf16), b.astype(bf16), preferred_element_type=f32)


def _split3(x):
    hi = x.astype(bf16)
    r1 = x - hi.astype(f32)
    mid = r1.astype(bf16)
    lo = (r1 - mid.astype(f32)).astype(bf16)
    return hi, mid, lo


def _gate_tiles(gates_ref, *, bb, nc, L):
    gt = jnp.stack([gates_ref[bi, ci * L:(ci + 1) * L, :] for bi in range(bb) for ci in range(nc)])
    tril = lax.broadcasted_iota(jnp.int32, (L, L), 1) <= lax.broadcasted_iota(jnp.int32, (L, L), 0)
    trb = jnp.broadcast_to(tril.astype(bf16), (bb * nc, L, L))
    cs = sum(jnp.einsum('pij,pjk->pik', trb, t, preferred_element_type=f32) for t in _split3(gt))
    return gt, cs


def _rows_from_cols(x, eye2):
    L = x.shape[1]
    onesb = jnp.ones((x.shape[0], L, L), bf16)
    return sum(jnp.einsum('pij,pjk->pik', onesb, t, preferred_element_type=f32)
               for t in _split3(x * eye2))


def _gdn_kernel(*refs, l, first, bb, nc, L, has_state, n_prev):
    qkv_ref, gates_ref, gz_ref, nw_ref = refs[:4]
    refs = refs[4:]
    if has_state:
        s0_ref = refs[0]
        refs = refs[1:]
    yb_ref, s_ref = refs[n_prev:]
    (s_ref,) = _own_layer([s_ref], l, first)
    t = pl.program_id(1)

    @pl.when(t == 0)
    def _():
        s_ref[...] = s0_ref[...] if has_state else jnp.zeros(s_ref.shape, f32)

    NP = GDN_HEADS // 2
    W2 = 2 * L
    nbc, npair = bb * nc, bb * nc * NP
    hi2 = lax.broadcasted_iota(jnp.int32, (1, W2), 1) >= L
    row = lax.broadcasted_iota(jnp.int32, (L, W2), 0)
    col = lax.broadcasted_iota(jnp.int32, (L, W2), 1) % L
    causal2, strict2 = col <= row, col < row
    eye2 = (col == row).astype(f32)
    steps = L.bit_length() - 2

    def head_tiles(col0, hl):
        return jnp.stack([qkv_ref[bi, ci * L:(ci + 1) * L,
                                  col0 + (2 * pp + hl) * GDN_D:col0 + (2 * pp + hl + 1) * GDN_D]
                          for bi in range(bb) for ci in range(nc) for pp in range(NP)])

    def pair_col(x, lane0, hl):
        return jnp.stack([x[:, :, lane0 + 2 * pp + hl:lane0 + 2 * pp + hl + 1] for pp in range(NP)],
                         axis=1).reshape(npair, L, 1)

    gt, cs = _gate_tiles(gates_ref, bb=bb, nc=nc, L=L)
    G = [pair_col(cs, G_LANE, hl) for hl in range(2)]
    beta = [pair_col(gt, B_LANE, hl) for hl in range(2)]
    g_col = jnp.where(hi2, G[1], G[0])
    g_row = _rows_from_cols(g_col, eye2)
    decay = jnp.where(causal2, jnp.exp(jnp.where(causal2, g_col - g_row, 0.0)), 0.0)
    q = [head_tiles(0, hl) for hl in range(2)]
    k = [head_tiles(D_GDN, hl) for hl in range(2)]
    v = [head_tiles(2 * D_GDN, hl) for hl in range(2)]
    kb = [k[hl] * beta[hl] for hl in range(2)]
    zd = jnp.zeros((npair, L, GDN_D), f32)
    lhs = jnp.concatenate([jnp.concatenate(kb, axis=-1), jnp.concatenate(q, axis=-1)], axis=1)
    k_bd = jnp.concatenate([jnp.concatenate([k[0], zd], axis=-1),
                            jnp.concatenate([zd, k[1]], axis=-1)], axis=1)
    kkqk = _bdot_nt(lhs, k_bd)
    kk = kkqk[:, 0:L] * decay
    qk = kkqk[:, L:] * decay
    def block_diag(x):
        return jnp.concatenate([jnp.where(hi2, 0.0, x), jnp.where(hi2, x, 0.0)], axis=1)

    N = jnp.where(strict2, kk, 0.0)
    R = -N
    P = _bdot(N, block_diag(N))
    for step in range(steps):
        if step < steps - 1:
            x = _bdot(jnp.concatenate([P, R], axis=1), block_diag(P))
            R = R + P + x[:, L:]
            P = x[:, 0:L]
        else:
            R = R + P + _bdot(R, block_diag(P))
    eG = [jnp.exp(G[hl]) for hl in range(2)]
    rhs = [jnp.concatenate([v[hl] * beta[hl], kb[hl] * eG[hl]], axis=-1) for hl in range(2)]
    z2 = jnp.zeros((npair, L, 2 * GDN_D), f32)
    w = [rhs[0] + _bdot(R, jnp.concatenate([rhs[0], z2], axis=1)),
         rhs[1] + _bdot(R, jnp.concatenate([z2, rhs[1]], axis=1))]
    qg = [q[hl] * eG[hl] for hl in range(2)]
    g_last = [G[hl][:, L - 1:L, :] for hl in range(2)]
    kdec = [k[hl] * jnp.exp(g_last[hl] - G[hl]) for hl in range(2)]
    e_last = [jnp.exp(g_last[hl]) for hl in range(2)]

    def chunk(xs, ci):
        return jnp.concatenate([x.reshape((bb, nc, NP) + x.shape[1:])[:, ci]
                                .reshape((bb * NP,) + x.shape[1:]) for x in xs], axis=0)

    order = [(bi, 2 * pp + hl) for hl in range(2) for bi in range(bb) for pp in range(NP)]
    nw = nw_ref[...]
    S = jnp.stack([s_ref[bi, h] for bi, h in order])
    zu = jnp.zeros((bb * NP, L, GDN_D), f32)
    for ci in range(nc):
        rs = slice(ci * L, (ci + 1) * L)
        w_c = chunk(w, ci)
        xs = _bdot(jnp.concatenate([w_c[:, :, GDN_D:], chunk(qg, ci)], axis=1), S)
        u = w_c[:, :, 0:GDN_D] - xs[:, 0:L]
        u_bd = jnp.concatenate([jnp.concatenate([u[0:bb * NP], zu], axis=1),
                                jnp.concatenate([zu, u[bb * NP:]], axis=1)], axis=0)
        o = xs[:, L:] + _bdot(chunk([qk, qk], ci), u_bd)
        S = S * chunk(e_last, ci) + _bdot_tn(chunk(kdec, ci), u)
        o = o * lax.rsqrt(jnp.mean(o * o, -1, keepdims=True) + NORM_EPS) * nw
        for i, (bi, h) in enumerate(order):
            hs = slice(h * GDN_D, (h + 1) * GDN_D)
            yb_ref[bi, rs, hs] = o[i] * gz_ref[bi, rs, hs]
    for i, (bi, h) in enumerate(order):
        s_ref[bi, h] = S[i]


def _gdn_call(l, qkv, gates, gz, p, states_in, prev_states, *, bb, nc, L):
    B, T, _ = qkv.shape
    tt = nc * L
    tile = functools.partial(_tile_spec, bb, tt)
    return _stacked_call(
        _gdn_kernel, l=l, B=B, bb=bb, nc=nc, L=L, grid=(B // bb, T // tt),
        inputs=[qkv, gates, gz, p["gdn_norm_w"]],
        in_specs=[tile(D_QKV), tile(LANE), tile(D_GDN), _weight_spec(l, (1, GDN_D))],
        states_in=states_in, prev_states=prev_states, state_tails=[(GDN_HEADS, GDN_D, GDN_D)],
        tile_shapes=[jax.ShapeDtypeStruct((B, T, D_GDN), f32)], tile_specs=[tile(D_GDN)],
        scratch_shapes=[], name="gdn")


def _mlstm_chunks(m4_ref, gates_ref, nw_ref, yc_ref, cn_ref, mp_ref, *, bb, nc, L):
    DH = ML_DH
    NP = ML_HEADS // 2
    nbc, npair = bb * nc, bb * nc * NP
    lane = lax.broadcasted_iota(jnp.int32, (1, LANE), 1)
    hi_half = lane >= DH
    row = lax.broadcasted_iota(jnp.int32, (L, LANE), 0)
    col = lax.broadcasted_iota(jnp.int32, (L, LANE), 1) % DH
    causal2 = col <= row
    eye2 = (col == row).astype(f32)
    r2 = lax.broadcasted_iota(jnp.int32, (LANE, LANE), 0) // DH
    c2 = lax.broadcasted_iota(jnp.int32, (LANE, LANE), 1) // DH
    blk = r2 == c2
    blk2 = jnp.concatenate([blk, blk], axis=1)
    ones_bd = blk.astype(bf16)

    def tiles(col0):
        return jnp.stack([m4_ref[bi, ci * L:(ci + 1) * L, col0 + pp * LANE:col0 + (pp + 1) * LANE]
                          for bi in range(bb) for ci in range(nc) for pp in range(NP)])

    def pair_cols(x, lane0):
        a = jnp.stack([x[:, :, lane0 + 2 * pp:lane0 + 2 * pp + 1] for pp in range(NP)], axis=1)
        b = jnp.stack([x[:, :, lane0 + 2 * pp + 1:lane0 + 2 * pp + 2] for pp in range(NP)], axis=1)
        return jnp.where(hi_half, b, a).reshape(npair, L, LANE)

    gt, cs = _gate_tiles(gates_ref, bb=bb, nc=nc, L=L)
    a_t = pltpu.roll(gt, F_LANE - I_LANE, axis=2) - cs
    A_t = a_t
    rows3 = lax.broadcasted_iota(jnp.int32, a_t.shape, 1)
    sh = 1
    while sh < L:
        A_t = jnp.maximum(A_t, jnp.where(rows3 >= sh, pltpu.roll(A_t, sh, axis=1), -jnp.inf))
        sh *= 2
    F = pair_cols(cs, F_LANE)
    a_p = pair_cols(a_t, F_LANE)
    A_p = pair_cols(A_t, F_LANE)
    Dm = jnp.where(causal2, F + _rows_from_cols(a_p, eye2), -jnp.inf)
    Q = tiles(0)
    K = tiles(D_ML)
    V = tiles(2 * D_ML)
    OG = tiles(3 * D_ML)
    if L < DH:
        zpad = jnp.zeros((npair, DH - L, LANE), f32)
        Kp, Vp = jnp.concatenate([K, zpad], axis=1), jnp.concatenate([V, zpad], axis=1)
    else:
        Kp, Vp = K, V
    k_bd = jnp.concatenate([jnp.where(hi_half, 0.0, Kp), jnp.where(hi_half, Kp, 0.0)], axis=1).astype(bf16)
    v_bd = jnp.concatenate([jnp.where(hi_half, 0.0, Vp), jnp.where(hi_half, Vp, 0.0)], axis=1).astype(bf16)
    vo = jnp.concatenate([v_bd, jnp.broadcast_to(ones_bd, (npair, LANE, LANE))], axis=-1)
    s_raw = jnp.einsum('pik,pjk->pij', Q.astype(bf16), k_bd, preferred_element_type=f32)
    v_one = jnp.concatenate([V, jnp.ones((npair, L, LANE), f32)], axis=-1).astype(bf16)

    def chunk(x, ci):
        return x.reshape((bb, nc, NP) + x.shape[1:])[:, ci].reshape((bb * NP,) + x.shape[1:])

    nw = jnp.stack([nw_ref[0:1, pp * LANE:(pp + 1) * LANE] for pp in range(NP)])
    nw = jnp.broadcast_to(nw[None], (bb, NP, 1, LANE)).reshape(bb * NP, 1, LANE)
    mean_w = (blk.astype(f32) * (1.0 / DH)).astype(bf16)

    def head_mean(x):
        hi = x.astype(bf16)
        lo = (x - hi.astype(f32)).astype(bf16)
        mw = jnp.broadcast_to(mean_w, (x.shape[0], LANE, LANE))
        return (jnp.einsum('pik,pkj->pij', hi, mw, preferred_element_type=f32)
                + jnp.einsum('pik,pkj->pij', lo, mw, preferred_element_type=f32))

    CN = cn_ref[...]
    m = mp_ref[:, 0:1, :]
    for ci in range(nc):
        F_c, A_c, a_c = chunk(F, ci), chunk(A_p, ci), chunk(a_p, ci)
        M = F_c + jnp.maximum(m, A_c)
        wD = jnp.exp(chunk(Dm, ci) - M)
        wI = jnp.exp(F_c + m - M)
        sc = chunk(s_raw, ci) * wD
        numq = (jnp.concatenate([wI, wI], axis=-1) * _bdot(chunk(Q, ci), CN)
                + jnp.einsum('pik,pkj->pij', sc.astype(bf16), chunk(vo, ci), preferred_element_type=f32))
        hh = numq[:, :, 0:LANE] / jnp.maximum(jnp.abs(numq[:, :, LANE:]), jnp.exp(-M))
        m_new = M[:, L - 1:L, :]
        f_last = F_c[:, L - 1:L, :]
        wk = jnp.exp(f_last - F_c + (a_c + F_c) - m_new)
        dc = jnp.exp(f_last + m - m_new)
        kw = chunk(K, ci) * wk
        upd = jnp.einsum('pki,pkj->pij', kw.astype(bf16), chunk(v_one, ci), preferred_element_type=f32)
        CN = jnp.concatenate([dc, dc], axis=-1) * CN + jnp.where(blk2, upd, 0.0)
        m = m_new
        hh = chunk(OG, ci) * hh
        hc = hh - head_mean(hh)
        var = head_mean(hc * hc)
        y = (hc * lax.rsqrt(var + LN_EPS) * nw).reshape(bb, NP, L, LANE)
        for bi in range(bb):
            for pp in range(NP):
                yc_ref[bi, ci * L:(ci + 1) * L, pp * LANE:(pp + 1) * LANE] = y[bi, pp]
    cn_ref[...] = CN
    mp_ref[:, 0:1, :] = m


def _mlstm_state_in(c0_ref, n0_ref, m0_ref, cn_ref, mp_ref, *, bb):
    DH, NP = ML_DH, ML_HEADS // 2
    if c0_ref is None:
        cn_ref[...] = jnp.zeros(cn_ref.shape, f32)
        mp_ref[...] = jnp.zeros(mp_ref.shape, f32)
        return
    hi_half = lax.broadcasted_iota(jnp.int32, (1, LANE), 1) >= DH
    blk = (lax.broadcasted_iota(jnp.int32, (LANE, LANE), 0) // DH
           == lax.broadcasted_iota(jnp.int32, (LANE, LANE), 1) // DH)
    for bi in range(bb):
        for pp in range(NP):
            c = c0_ref[bi, pp]
            cn_ref[bi * NP + pp, :, 0:LANE] = jnp.concatenate(
                [jnp.where(hi_half, 0.0, c), jnp.where(hi_half, c, 0.0)], axis=0)
            n_col = jnp.broadcast_to(n0_ref[bi, pp:pp + 1, :], (LANE, LANE)).T
            cn_ref[bi * NP + pp, :, LANE:] = jnp.where(blk, n_col, 0.0)
            m_a = m0_ref[bi, 0:1, 2 * pp:2 * pp + 1]
            m_b = m0_ref[bi, 0:1, 2 * pp + 1:2 * pp + 2]
            mp_ref[bi * NP + pp, 0:1, :] = jnp.where(hi_half, m_b, m_a)


def _mlstm_state_out(cn_ref, mp_ref, c_ref, n_ref, m_ref, *, bb):
    DH, NP = ML_DH, ML_HEADS // 2
    hi_half = lax.broadcasted_iota(jnp.int32, (1, LANE), 1) >= DH
    for bi in range(bb):
        for pp in range(NP):
            c_bd = cn_ref[bi * NP + pp, :, 0:LANE]
            c_ref[bi, pp] = jnp.where(hi_half, c_bd[DH:], c_bd[0:DH])
            n_t = cn_ref[bi * NP + pp, :, LANE:].T
            n_ref[bi, pp:pp + 1, :] = jnp.where(hi_half, n_t[DH:DH + 1, :], n_t[0:1, :])
            for hl in range(2):
                m_ref[bi, 0:1, 2 * pp + hl:2 * pp + hl + 1] = mp_ref[bi * NP + pp, 0:1, hl * DH:hl * DH + 1]


def _mlstm_kernel(*refs, l, first, bb, nc, L, has_state, n_prev):
    m4_ref, gates_ref, nw_ref = refs[:3]
    refs = refs[3:]
    c0_ref = n0_ref = m0_ref = None
    if has_state:
        c0_ref, n0_ref, m0_ref = refs[:3]
        refs = refs[3:]
    yc_ref, c_ref, n_ref, m_ref, cn_ref, mp_ref = refs[n_prev:]
    c_ref, n_ref, m_ref = _own_layer([c_ref, n_ref, m_ref], l, first)
    t = pl.program_id(1)

    @pl.when(t == 0)
    def _():
        _mlstm_state_in(c0_ref, n0_ref, m0_ref, cn_ref, mp_ref, bb=bb)

    _mlstm_chunks(m4_ref, gates_ref, nw_ref, yc_ref, cn_ref, mp_ref, bb=bb, nc=nc, L=L)

    @pl.when(t == pl.num_programs(1) - 1)
    def _():
        _mlstm_state_out(cn_ref, mp_ref, c_ref, n_ref, m_ref, bb=bb)


def _mlstm_call(l, m4, gates, p, states_in, prev_states, *, bb, nc, L):
    B, T, _ = m4.shape
    tt = nc * L
    tile = functools.partial(_tile_spec, bb, tt)
    npair = bb * (ML_HEADS // 2)
    return _stacked_call(
        _mlstm_kernel, l=l, B=B, bb=bb, nc=nc, L=L, grid=(B // bb, T // tt),
        inputs=[m4, gates, p["ml_norm_w"]],
        in_specs=[tile(4 * D_ML), tile(LANE), _weight_spec(l, (1, D_ML))],
        states_in=states_in, prev_states=prev_states,
        state_tails=[(ML_HEADS // 2, ML_DH, LANE), (ML_HEADS // 2, LANE), (1, ML_HEADS)],
        tile_shapes=[jax.ShapeDtypeStruct((B, T, D_ML), f32)], tile_specs=[tile(D_ML)],
        scratch_shapes=[pltpu.VMEM((npair, LANE, 2 * LANE), f32), pltpu.VMEM((npair, 8, LANE), f32)],
        name="mlstm")


def _out_ffn_kernel(*refs, l, first, bb, tt, has_state, n_prev):
    (x_ref, ya_ref, yb_ref, yc_ref, wo_ref, ln1_ref, wup_ref, fcw_ref, wdn_ref, ln2_ref) = refs[:10]
    refs = refs[10:]
    if has_state:
        hf_ref = refs[0]
        refs = refs[1:]
    xo_ref, tf_ref, sf_ref = refs[n_prev:]
    (tf_ref,) = _own_layer([tf_ref], l, first)
    t = pl.program_id(1)

    @pl.when(t == 0)
    def _():
        if has_state:
            sf_ref[:, HIST - 2:HIST, :] = hf_ref[...]
        else:
            sf_ref[:, 0:HIST, :] = jnp.zeros((bb, HIST, D_FF), f32)

    def stages(r0, th):
        rows = bb * th
        rs = slice(r0, r0 + th)
        x = x_ref[:, rs, :].reshape(rows, D_MODEL)
        mix = (jnp.dot(ya_ref[:, rs, :].reshape(rows, D_CONV).astype(bf16), wo_ref[0:D_CONV, :],
                       preferred_element_type=f32)
               + jnp.dot(yb_ref[:, rs, :].reshape(rows, D_GDN).astype(bf16),
                         wo_ref[D_CONV:D_CONV + D_GDN, :], preferred_element_type=f32)
               + jnp.dot(yc_ref[:, rs, :].reshape(rows, D_ML).astype(bf16), wo_ref[D_CONV + D_GDN:, :],
                         preferred_element_type=f32))
        yield
        x1 = _layer_norm(ALPHA * x + mix, ln1_ref[0:1, :], ln1_ref[1:2, :])
        x1b = x1.astype(bf16)
        yield
        acc = None
        for lo, hi in FF_CHUNKS:
            gate = jnp.dot(x1b, wup_ref[:, lo:hi], preferred_element_type=f32)
            val = jnp.dot(x1b, wup_ref[:, D_FF + lo:D_FF + hi], preferred_element_type=f32)
            yield
            sf_ref[:, HIST + r0:HIST + r0 + th, lo:hi] = gate.reshape(bb, th, hi - lo)
            conv = _causal_dwconv(sf_ref[:, r0:r0 + HIST + th, lo:hi], fcw_ref[:, lo:hi])
            hmid = _silu(conv).reshape(rows, hi - lo) * val
            yield
            part = jnp.dot(hmid.astype(bf16), wdn_ref[lo:hi, :], preferred_element_type=f32)
            acc = part if acc is None else acc + part
            yield
        x2 = _layer_norm(ALPHA * x1 + acc, ln2_ref[0:1, :], ln2_ref[1:2, :])
        xo_ref[:, rs, :] = x2.reshape(bb, th, D_MODEL)

    nh = 2 if tt % (2 * MXU_DIM) == 0 else 1
    for _ in itertools.zip_longest(*[stages(i * (tt // nh), tt // nh) for i in range(nh)]):
        pass
    tf_ref[...] = sf_ref[:, tt + HIST - 2:tt + HIST, :]
    sf_ref[:, 0:HIST, :] = sf_ref[:, tt:tt + HIST, :]


def _out_ffn_call(l, x, ya, yb, yc, p, states_in, prev_states, *, bb, tt):
    B, T, _ = x.shape
    tile = functools.partial(_tile_spec, bb, tt)
    return _stacked_call(
        _out_ffn_kernel, l=l, B=B, bb=bb, tt=tt, grid=(B // bb, T // tt),
        inputs=[x, ya, yb, yc, p["w_o"], p["ln1"], p["w_up"], p["ffn_conv_w"], p["w_down"], p["ln2"]],
        in_specs=[tile(D_MODEL), tile(D_CONV), tile(D_GDN), tile(D_ML),
                  _weight_spec(l, (D_MODEL, D_MODEL)), _weight_spec(l, (2, D_MODEL)),
                  _weight_spec(l, (D_MODEL, 2 * D_FF)), _weight_spec(l, (3, D_FF)),
                  _weight_spec(l, (D_FF, D_MODEL)), _weight_spec(l, (2, D_MODEL))],
        states_in=states_in, prev_states=prev_states, state_tails=[(2, D_FF)],
        tile_shapes=[jax.ShapeDtypeStruct((B, T, D_MODEL), f32)], tile_specs=[tile(D_MODEL)],
        scratch_shapes=[pltpu.VMEM((bb, tt + HIST, D_FF), f32)], name="out_ffn")


def _prep_params(w_in, conv_w, gdn_conv_w, gdn_a_log, gdn_dt_bias, gdn_norm_w,
                 ml_i_bias, ml_f_bias, ml_norm_w, w_o, ln1_g, ln1_b,
                 w_up, ffn_conv_w, w_down, ln2_g, ln2_b):
    gate_cols = jnp.concatenate([w_in[:, :, COL_GA:COL_MQ], w_in[:, :, COL_MI:D_IN]], axis=2)
    gate_cols = jnp.pad(gate_cols, ((0, 0), (0, 0), (0, LANE - gate_cols.shape[2])))
    w_tail = jnp.concatenate([w_in[:, :, COL_MQ:COL_MI], gate_cols], axis=2).astype(bf16)
    zeros4 = jnp.zeros((DEPTH, 4), f32)
    bias_row = jnp.pad(jnp.concatenate([gdn_dt_bias, zeros4, ml_i_bias, ml_f_bias], axis=1),
                       ((0, 0), (0, LANE - 16)))
    alog_row = jnp.pad(gdn_a_log, ((0, 0), (0, LANE - GDN_HEADS)))
    gate_p = jnp.zeros((DEPTH, 8, LANE), f32).at[:, 0].set(bias_row).at[:, 1].set(alog_row)
    return dict(
        w_in=w_in.astype(bf16), w_tail=w_tail, conv_w=conv_w, gdn_conv_w=gdn_conv_w, gate_p=gate_p,
        gdn_norm_w=gdn_norm_w.reshape(DEPTH, 1, GDN_D), ml_norm_w=ml_norm_w.reshape(DEPTH, 1, D_ML),
        w_o=w_o.astype(bf16), ln1=jnp.stack([ln1_g, ln1_b], axis=1),
        w_up=w_up.astype(bf16), ffn_conv_w=ffn_conv_w, w_down=w_down.astype(bf16),
        ln2=jnp.stack([ln2_g, ln2_b], axis=1),
    )


def _tiling(B, T):
    L = min(CHUNK, T)
    if T >= 512:
        return dict(bb=1, tt=256, fbb=1, ftt=512, L=L, rbb=min(B, 8), gnc=2, mnc=2)
    return dict(bb=256 // T, tt=T, fbb=256 // T, ftt=T, L=L, rbb=min(B, 16), gnc=T // L, mnc=T // L)


def _trunk(x, states, p):
    B, T, _ = x.shape
    c = _tiling(B, T)
    conv_new = gdn_new = ml_new = ffn_new = None
    for l in range(DEPTH):
        sel = lambda idx: None if states is None else [states[i] for i in idx]
        ya, qkv, gz, m4, gates, *conv_new = _proj_call(l, x, p, sel((0, 1)), conv_new,
                                                       bb=c["bb"], tt=c["tt"])
        yb, *gdn_new = _gdn_call(l, qkv, gates, gz, p, sel((2,)), gdn_new,
                                 bb=c["rbb"], nc=c["gnc"], L=c["L"])
        yc, *ml_new = _mlstm_call(l, m4, gates, p, sel((3, 4, 5)), ml_new,
                                  bb=c["rbb"], nc=c["mnc"], L=c["L"])
        x, *ffn_new = _out_ffn_call(l, x, ya, yb, yc, p, sel((6,)), ffn_new,
                                    bb=c["fbb"], tt=c["ftt"])
    return x, (conv_new[0], conv_new[1], gdn_new[0], _unpair_heads(ml_new[0]),
               ml_new[1].reshape(DEPTH, B, ML_HEADS, ML_DH),
               ml_new[2].reshape(DEPTH, B, ML_HEADS), ffn_new[0])


def _pair_heads(c):
    d, b = c.shape[:2]
    return (c.reshape(d, b, ML_HEADS // 2, 2, ML_DH, ML_DH).transpose(0, 1, 2, 4, 3, 5)
            .reshape(d, b, ML_HEADS // 2, ML_DH, 2 * ML_DH))


def _unpair_heads(c):
    d, b = c.shape[:2]
    return (c.reshape(d, b, ML_HEADS // 2, ML_DH, 2, ML_DH).transpose(0, 1, 2, 4, 3, 5)
            .reshape(d, b, ML_HEADS, ML_DH, ML_DH))


def kernel(x_prompt, x_sample, state_conv_mix, state_gdn_conv, state_gdn, state_mlstm_c,
           state_mlstm_n, state_mlstm_m, state_ffn_conv,
           w_in, conv_w, gdn_conv_w, gdn_a_log, gdn_dt_bias, gdn_norm_w,
           ml_i_bias, ml_f_bias, ml_norm_w, w_o, ln1_g, ln1_b,
           w_up, ffn_conv_w, w_down, ln2_g, ln2_b):
    p = _prep_params(w_in, conv_w, gdn_conv_w, gdn_a_log, gdn_dt_bias, gdn_norm_w,
                     ml_i_bias, ml_f_bias, ml_norm_w, w_o, ln1_g, ln1_b,
                     w_up, ffn_conv_w, w_down, ln2_g, ln2_b)
    Bs = x_sample.shape[0]
    y_prompt, p_new = _trunk(x_prompt, None, p)
    s_states = (state_conv_mix, state_gdn_conv, state_gdn, _pair_heads(state_mlstm_c),
                state_mlstm_n.reshape(DEPTH, Bs, ML_HEADS // 2, 2 * ML_DH),
                state_mlstm_m.reshape(DEPTH, Bs, 1, ML_HEADS), state_ffn_conv)
    y_sample, s_new = _trunk(x_sample, s_states, p)
    return (y_prompt, y_sample, *p_new, *s_new)
```

```python
import functools
import itertools

import jax
import jax.numpy as jnp
from jax import lax
from jax.experimental import pallas as pl
from jax.experimental.pallas import tpu as pltpu

f32 = jnp.float32
bf16 = jnp.bfloat16

D_MODEL = 1024
DEPTH = 2
D_CONV = 256
GDN_HEADS = 4
GDN_D = 128
D_GDN = GDN_HEADS * GDN_D
D_QKV = 3 * D_GDN
ML_HEADS = 4
ML_DH = 64
D_ML = ML_HEADS * ML_DH
D_FF = 2816
CHUNK = 64
ALPHA = (2.0 * DEPTH) ** 0.25
LN_EPS = 1e-5
NORM_EPS = 1e-6

LANE = 128
HIST = 8
COL_A = 0
COL_G = 3 * D_CONV
COL_GA = COL_G + 4 * D_GDN
COL_MQ = COL_GA + 2 * GDN_HEADS
COL_MI = COL_MQ + 4 * D_ML
D_IN = COL_MI + 2 * ML_HEADS
D_TAIL = 4 * D_ML + LANE
G_LANE, B_LANE, I_LANE, F_LANE = 0, 4, 8, 12
MXU_DIM = 256
FF_CHUNKS = ((0, 6 * MXU_DIM), (6 * MXU_DIM, D_FF))
VMEM_LIMIT = 56 * 1024 * 1024
_PARAMS = pltpu.CompilerParams(dimension_semantics=("arbitrary", "arbitrary"),
                               vmem_limit_bytes=VMEM_LIMIT)


def _softplus(z):
    return jnp.maximum(z, 0.0) + jnp.log1p(jnp.exp(-jnp.abs(z)))


def _silu(x):
    h = 0.5 * x
    return h + h * jnp.tanh(h)


def _causal_dwconv(xh, w):
    width = w.shape[0]
    y = w[width - 1:width, :] * xh[:, HIST:, :]
    for j in range(width - 1):
        y = y + w[j:j + 1, :] * pltpu.roll(xh, width - 1 - j, axis=1)[:, HIST:, :]
    return y


def _layer_norm(x, g, b):
    mu = jnp.mean(x, -1, keepdims=True)
    xc = x - mu
    var = jnp.mean(xc * xc, -1, keepdims=True)
    return xc * lax.rsqrt(var + LN_EPS) * g + b


def _tile_spec(bb, tt, c):
    return pl.BlockSpec((bb, tt, c), lambda b, t: (b, t, 0))


def _state_spec(l, bb, tail, every_layer=False):
    if every_layer:
        return pl.BlockSpec((DEPTH, bb) + tail, lambda b, t: (0, b) + (0,) * len(tail))
    return pl.BlockSpec((None, bb) + tail, lambda b, t: (l, b) + (0,) * len(tail))


def _own_layer(refs, l, first):
    if not first:
        return refs

    @pl.when(pl.program_id(1) == 0)
    def _():
        for r in refs:
            for j in range(DEPTH):
                if j != l:
                    r[j] = jnp.zeros(r.shape[1:], f32)

    return [r.at[l] for r in refs]


def _weight_spec(l, tail):
    return pl.BlockSpec((None,) + tail, lambda b, t: (l,) + (0,) * len(tail),
                        pipeline_mode=pl.Buffered(1))


def _stacked_call(kernel, *, l, B, bb, grid, inputs, in_specs, states_in, prev_states, state_tails,
                  tile_shapes, tile_specs, scratch_shapes, name, state_in_tails=None, **static):
    first = prev_states is None
    args = list(inputs)
    specs = list(in_specs)
    if states_in is not None:
        args += list(states_in)
        specs += [_state_spec(l, bb, tail) for tail in (state_in_tails or state_tails)]
    aliases = {}
    if not first:
        for i, p in enumerate(prev_states):
            aliases[len(args)] = len(tile_shapes) + i
            args.append(p)
            specs.append(pl.BlockSpec(memory_space=pl.ANY))
    return pl.pallas_call(
        functools.partial(kernel, l=l, first=first, has_state=states_in is not None,
                          n_prev=0 if first else len(prev_states), bb=bb, **static),
        grid=grid, in_specs=specs,
        out_specs=tuple(tile_specs) + tuple(_state_spec(l, bb, tail, every_layer=first)
                                            for tail in state_tails),
        out_shape=tuple(tile_shapes) + tuple(jax.ShapeDtypeStruct((DEPTH, B) + tail, f32)
                                             for tail in state_tails),
        scratch_shapes=scratch_shapes, input_output_aliases=aliases,
        compiler_params=_PARAMS, name=name)(*args)


def _proj_kernel(*refs, l, first, bb, tt, has_state, n_prev):
    x_ref, w_ref, wt_ref, cw_ref, gcw_ref, gp_ref = refs[:6]
    refs = refs[6:]
    if has_state:
        ha_ref, hg_ref = refs[:2]
        refs = refs[2:]
    ya_ref, qkv_ref, gz_ref, m4_ref, gates_ref, ta_ref, tg_ref, sa_ref, sg_ref = refs[n_prev:]
    ta_ref, tg_ref = _own_layer([ta_ref, tg_ref], l, first)
    t = pl.program_id(1)

    @pl.when(t == 0)
    def _():
        if has_state:
            sa_ref[:, HIST - 2:HIST, :] = ha_ref[...]
            sg_ref[:, HIST - 3:HIST, :] = hg_ref[...]
        else:
            sa_ref[:, 0:HIST, :] = jnp.zeros((bb, HIST, D_CONV), f32)
            sg_ref[:, 0:HIST, :] = jnp.zeros((bb, HIST, D_QKV), f32)

    rows = bb * tt
    xb = x_ref[...].reshape(rows, D_MODEL).astype(bf16)

    def post_a(pa):
        a_b = pa[:, 0:D_CONV]
        sa_ref[:, HIST:, :] = (pa[:, D_CONV:2 * D_CONV] * pa[:, 2 * D_CONV:3 * D_CONV]).reshape(bb, tt, D_CONV)
        conv = _causal_dwconv(sa_ref[...], cw_ref[...])
        ya_ref[...] = a_b.reshape(bb, tt, D_CONV) * conv
        ta_ref[...] = sa_ref[:, tt + HIST - 2:tt + HIST, :]
        sa_ref[:, 0:HIST, :] = sa_ref[:, tt:tt + HIST, :]

    def post_qkv(c0, pg):
        w = pg.shape[1]
        sg_ref[:, HIST:, c0:c0 + w] = pg.reshape(bb, tt, w)
        for cb in range(c0 // LANE, (c0 + w) // LANE):
            cs = slice(cb * LANE, (cb + 1) * LANE)
            c = _silu(_causal_dwconv(sg_ref[:, :, cs], gcw_ref[:, cs]))
            if cb < 2 * GDN_HEADS:
                c = c * lax.rsqrt(jnp.sum(c * c, -1, keepdims=True) + NORM_EPS)
                if cb < GDN_HEADS:
                    c = c * (GDN_D ** -0.5)
            qkv_ref[:, :, cs] = c
        tg_ref[:, :, c0:c0 + w] = sg_ref[:, tt + HIST - 3:tt + HIST, c0:c0 + w]
        sg_ref[:, 0:HIST, c0:c0 + w] = sg_ref[:, tt:tt + HIST, c0:c0 + w]

    def post_gz(c0, pz):
        gz_ref[:, :, c0:c0 + pz.shape[1]] = _silu(pz).reshape(bb, tt, pz.shape[1])

    def post_m(part, pm):
        if part == 1:
            pm = pm * (ML_DH ** -0.5)
        elif part == 3:
            pm = jax.nn.sigmoid(pm)
        m4_ref[:, :, part * D_ML:(part + 1) * D_ML] = pm.reshape(bb, tt, D_ML)

    def post_gates(z):
        z = z + gp_ref[0:1, :]
        lane = lax.broadcasted_iota(jnp.int32, z.shape, 1)
        g = -jnp.exp(gp_ref[1:2, :]) * _softplus(z)
        gates = jnp.where(lane < B_LANE, g,
                          jnp.where(lane < I_LANE, jax.nn.sigmoid(z),
                                    jnp.where(lane < F_LANE, z,
                                              jnp.where(lane < F_LANE + ML_HEADS, -_softplus(-z), 0.0))))
        gates_ref[...] = gates.reshape(bb, tt, LANE)

    W = MXU_DIM
    qkv = [(w_ref, COL_G + c, COL_G + c + W, functools.partial(post_qkv, c)) for c in range(0, D_QKV, W)]
    gz = [(w_ref, COL_G + D_QKV + c, COL_G + D_QKV + c + W, functools.partial(post_gz, c))
          for c in range(0, D_GDN, W)]
    mm = [(wt_ref, i * D_ML, (i + 1) * D_ML, functools.partial(post_m, i)) for i in range(4)]
    light = [(w_ref, COL_A, COL_G, post_a)] + mm + gz + [(wt_ref, 4 * D_ML, 4 * D_ML + LANE, post_gates)]
    tasks = []
    for i in range(max(len(qkv), len(light))):
        tasks += qkv[i:i + 1] + light[i:i + 1]
    pending = None
    for ref, lo, hi, post in tasks:
        res = jnp.dot(xb, ref[:, lo:hi], preferred_element_type=f32)
        if pending is not None:
            pending[0](pending[1])
        pending = (post, res)
    pending[0](pending[1])


def _proj_call(l, x, p, states_in, prev_states, *, bb, tt):
    B, T, _ = x.shape
    tile = functools.partial(_tile_spec, bb, tt)
    widths = (D_CONV, D_QKV, D_GDN, 4 * D_ML, LANE)
    return _stacked_call(
        _proj_kernel, l=l, B=B, bb=bb, tt=tt, grid=(B // bb, T // tt),
        inputs=[x, p["w_in"], p["w_tail"], p["conv_w"], p["gdn_conv_w"], p["gate_p"]],
        in_specs=[tile(D_MODEL), _weight_spec(l, (D_MODEL, D_IN)), _weight_spec(l, (D_MODEL, D_TAIL)),
                  _weight_spec(l, (3, D_CONV)), _weight_spec(l, (4, D_QKV)), _weight_spec(l, (8, LANE))],
        states_in=states_in, prev_states=prev_states, state_tails=[(2, D_CONV), (3, D_QKV)],
        tile_shapes=[jax.ShapeDtypeStruct((B, T, w), f32) for w in widths],
        tile_specs=[tile(w) for w in widths],
        scratch_shapes=[pltpu.VMEM((bb, tt + HIST, D_CONV), f32),
                        pltpu.VMEM((bb, tt + HIST, D_QKV), f32)],
        name="in_proj")


def _bdot(a, b):
    return jnp.einsum('pik,pkj->pij', a.astype(bf16), b.astype(bf16), preferred_element_type=f32)


def _bdot_nt(a, b):
    return jnp.einsum('pik,pjk->pij', a.astype(bf16), b.astype(bf16), preferred_element_type=f32)


def _bdot_tn(a, b):
    return jnp.einsum('pki,pkj->pij', a.astype(bf16), b.astype(bf16), preferred_element_type=f32)


def _split3(x):
    hi = x.astype(bf16)
    r1 = x - hi.astype(f32)
    mid = r1.astype(bf16)
    lo = (r1 - mid.astype(f32)).astype(bf16)
    return hi, mid, lo


def _gate_tiles(gates_ref, *, bb, nc, L):
    gt = jnp.stack([gates_ref[bi, ci * L:(ci + 1) * L, :] for bi in range(bb) for ci in range(nc)])
    tril = lax.broadcasted_iota(jnp.int32, (L, L), 1) <= lax.broadcasted_iota(jnp.int32, (L, L), 0)
    trb = jnp.broadcast_to(tril.astype(bf16), (bb * nc, L, L))
    cs = sum(jnp.einsum('pij,pjk->pik', trb, t, preferred_element_type=f32) for t in _split3(gt))
    return gt, cs


def _rows_from_cols(x, eye2):
    L = x.shape[1]
    onesb = jnp.ones((x.shape[0], L, L), bf16)
    return sum(jnp.einsum('pij,pjk->pik', onesb, t, preferred_element_type=f32)
               for t in _split3(x * eye2))


def _gdn_kernel(*refs, l, first, bb, nc, L, has_state, n_prev):
    qkv_ref, gates_ref, gz_ref, nw_ref = refs[:4]
    refs = refs[4:]
    if has_state:
        s0_ref = refs[0]
        refs = refs[1:]
    yb_ref, s_ref = refs[n_prev:]
    (s_ref,) = _own_layer([s_ref], l, first)
    t = pl.program_id(1)

    @pl.when(t == 0)
    def _():
        s_ref[...] = s0_ref[...] if has_state else jnp.zeros(s_ref.shape, f32)

    NP = GDN_HEADS // 2
    W2 = 2 * L
    nbc, npair = bb * nc, bb * nc * NP
    hi2 = lax.broadcasted_iota(jnp.int32, (1, W2), 1) >= L
    row = lax.broadcasted_iota(jnp.int32, (L, W2), 0)
    col = lax.broadcasted_iota(jnp.int32, (L, W2), 1) % L
    causal2, strict2 = col <= row, col < row
    eye2 = (col == row).astype(f32)
    steps = L.bit_length() - 2

    def head_tiles(col0, hl):
        return jnp.stack([qkv_ref[bi, ci * L:(ci + 1) * L,
                                  col0 + (2 * pp + hl) * GDN_D:col0 + (2 * pp + hl + 1) * GDN_D]
                          for bi in range(bb) for ci in range(nc) for pp in range(NP)])

    def pair_col(x, lane0, hl):
        return jnp.stack([x[:, :, lane0 + 2 * pp + hl:lane0 + 2 * pp + hl + 1] for pp in range(NP)],
                         axis=1).reshape(npair, L, 1)

    gt, cs = _gate_tiles(gates_ref, bb=bb, nc=nc, L=L)
    G = [pair_col(cs, G_LANE, hl) for hl in range(2)]
    beta = [pair_col(gt, B_LANE, hl) for hl in range(2)]
    g_col = jnp.where(hi2, G[1], G[0])
    g_row = _rows_from_cols(g_col, eye2)
    decay = jnp.where(causal2, jnp.exp(jnp.where(causal2, g_col - g_row, 0.0)), 0.0)
    q = [head_tiles(0, hl) for hl in range(2)]
    k = [head_tiles(D_GDN, hl) for hl in range(2)]
    v = [head_tiles(2 * D_GDN, hl) for hl in range(2)]
    kb = [k[hl] * beta[hl] for hl in range(2)]
    zd = jnp.zeros((npair, L, GDN_D), f32)
    lhs = jnp.concatenate([jnp.concatenate(kb, axis=-1), jnp.concatenate(q, axis=-1)], axis=1)
    k_bd = jnp.concatenate([jnp.concatenate([k[0], zd], axis=-1),
                            jnp.concatenate([zd, k[1]], axis=-1)], axis=1)
    kkqk = _bdot_nt(lhs, k_bd)
    kk = kkqk[:, 0:L] * decay
    qk = kkqk[:, L:] * decay
    def block_diag(x):
        return jnp.concatenate([jnp.where(hi2, 0.0, x), jnp.where(hi2, x, 0.0)], axis=1)

    N = jnp.where(strict2, kk, 0.0)
    R = -N
    P = _bdot(N, block_diag(N))
    for step in range(steps):
        if step < steps - 1:
            x = _bdot(jnp.concatenate([P, R], axis=1), block_diag(P))
            R = R + P + x[:, L:]
            P = x[:, 0:L]
        else:
            R = R + P + _bdot(R, block_diag(P))
    eG = [jnp.exp(G[hl]) for hl in range(2)]
    rhs = [jnp.concatenate([v[hl] * beta[hl], kb[hl] * eG[hl]], axis=-1) for hl in range(2)]
    z2 = jnp.zeros((npair, L, 2 * GDN_D), f32)
    w = [rhs[0] + _bdot(R, jnp.concatenate([rhs[0], z2], axis=1)),
         rhs[1] + _bdot(R, jnp.concatenate([z2, rhs[1]], axis=1))]
    qg = [q[hl] * eG[hl] for hl in range(2)]
    g_last = [G[hl][:, L - 1:L, :] for hl in range(2)]
    kdec = [k[hl] * jnp.exp(g_last[hl] - G[hl]) for hl in range(2)]
    e_last = [jnp.exp(g_last[hl]) for hl in range(2)]

    def chunk(xs, ci):
        return jnp.concatenate([x.reshape((bb, nc, NP) + x.shape[1:])[:, ci]
                                .reshape((bb * NP,) + x.shape[1:]) for x in xs], axis=0)

    order = [(bi, 2 * pp + hl) for hl in range(2) for bi in range(bb) for pp in range(NP)]
    nw = nw_ref[...]
    S = jnp.stack([s_ref[bi, h] for bi, h in order])
    zu = jnp.zeros((bb * NP, L, GDN_D), f32)
    for ci in range(nc):
        rs = slice(ci * L, (ci + 1) * L)
        w_c = chunk(w, ci)
        xs = _bdot(jnp.concatenate([w_c[:, :, GDN_D:], chunk(qg, ci)], axis=1), S)
        u = w_c[:, :, 0:GDN_D] - xs[:, 0:L]
        u_bd = jnp.concatenate([jnp.concatenate([u[0:bb * NP], zu], axis=1),
                                jnp.concatenate([zu, u[bb * NP:]], axis=1)], axis=0)
        o = xs[:, L:] + _bdot(chunk([qk, qk], ci), u_bd)
        S = S * chunk(e_last, ci) + _bdot_tn(chunk(kdec, ci), u)
        o = o * lax.rsqrt(jnp.mean(o * o, -1, keepdims=True) + NORM_EPS) * nw
        for i, (bi, h) in enumerate(order):
            hs = slice(h * GDN_D, (h + 1) * GDN_D)
            yb_ref[bi, rs, hs] = o[i] * gz_ref[bi, rs, hs]
    for i, (bi, h) in enumerate(order):
        s_ref[bi, h] = S[i]


def _gdn_call(l, qkv, gates, gz, p, states_in, prev_states, *, bb, nc, L):
    B, T, _ = qkv.shape
    tt = nc * L
    tile = functools.partial(_tile_spec, bb, tt)
    return _stacked_call(
        _gdn_kernel, l=l, B=B, bb=bb, nc=nc, L=L, grid=(B // bb, T // tt),
        inputs=[qkv, gates, gz, p["gdn_norm_w"]],
        in_specs=[tile(D_QKV), tile(LANE), tile(D_GDN), _weight_spec(l, (1, GDN_D))],
        states_in=states_in, prev_states=prev_states, state_tails=[(GDN_HEADS, GDN_D, GDN_D)],
        tile_shapes=[jax.ShapeDtypeStruct((B, T, D_GDN), f32)], tile_specs=[tile(D_GDN)],
        scratch_shapes=[], name="gdn")


def _mlstm_chunks(m4_ref, gates_ref, nw_ref, yc_ref, cn_ref, mp_ref, *, bb, nc, L):
    DH = ML_DH
    NP = ML_HEADS // 2
    nbc, npair = bb * nc, bb * nc * NP
    lane = lax.broadcasted_iota(jnp.int32, (1, LANE), 1)
    hi_half = lane >= DH
    row = lax.broadcasted_iota(jnp.int32, (L, LANE), 0)
    col = lax.broadcasted_iota(jnp.int32, (L, LANE), 1) % DH
    causal2 = col <= row
    eye2 = (col == row).astype(f32)
    r2 = lax.broadcasted_iota(jnp.int32, (LANE, LANE), 0) // DH
    c2 = lax.broadcasted_iota(jnp.int32, (LANE, LANE), 1) // DH
    blk = r2 == c2
    blk2 = jnp.concatenate([blk, blk], axis=1)
    ones_bd = blk.astype(bf16)

    def tiles(col0):
        return jnp.stack([m4_ref[bi, ci * L:(ci + 1) * L, col0 + pp * LANE:col0 + (pp + 1) * LANE]
                          for bi in range(bb) for ci in range(nc) for pp in range(NP)])

    def pair_cols(x, lane0):
        a = jnp.stack([x[:, :, lane0 + 2 * pp:lane0 + 2 * pp + 1] for pp in range(NP)], axis=1)
        b = jnp.stack([x[:, :, lane0 + 2 * pp + 1:lane0 + 2 * pp + 2] for pp in range(NP)], axis=1)
        return jnp.where(hi_half, b, a).reshape(npair, L, LANE)

    gt, cs = _gate_tiles(gates_ref, bb=bb, nc=nc, L=L)
    a_t = pltpu.roll(gt, F_LANE - I_LANE, axis=2) - cs
    A_t = a_t
    rows3 = lax.broadcasted_iota(jnp.int32, a_t.shape, 1)
    sh = 1
    while sh < L:
        A_t = jnp.maximum(A_t, jnp.where(rows3 >= sh, pltpu.roll(A_t, sh, axis=1), -jnp.inf))
        sh *= 2
    F = pair_cols(cs, F_LANE)
    a_p = pair_cols(a_t, F_LANE)
    A_p = pair_cols(A_t, F_LANE)
    Dm = jnp.where(causal2, F + _rows_from_cols(a_p, eye2), -jnp.inf)
    Q = tiles(0)
    K = tiles(D_ML)
    V = tiles(2 * D_ML)
    OG = tiles(3 * D_ML)
    if L < DH:
        zpad = jnp.zeros((npair, DH - L, LANE), f32)
        Kp, Vp = jnp.concatenate([K, zpad], axis=1), jnp.concatenate([V, zpad], axis=1)
    else:
        Kp, Vp = K, V
    k_bd = jnp.concatenate([jnp.where(hi_half, 0.0, Kp), jnp.where(hi_half, Kp, 0.0)], axis=1).astype(bf16)
    v_bd = jnp.concatenate([jnp.where(hi_half, 0.0, Vp), jnp.where(hi_half, Vp, 0.0)], axis=1).astype(bf16)
    vo = jnp.concatenate([v_bd, jnp.broadcast_to(ones_bd, (npair, LANE, LANE))], axis=-1)
    s_raw = jnp.einsum('pik,pjk->pij', Q.astype(bf16), k_bd, preferred_element_type=f32)
    v_one = jnp.concatenate([V, jnp.ones((npair, L, LANE), f32)], axis=-1).astype(bf16)

    def chunk(x, ci):
        return x.reshape((bb, nc, NP) + x.shape[1:])[:, ci].reshape((bb * NP,) + x.shape[1:])

    nw = jnp.stack([nw_ref[0:1, pp * LANE:(pp + 1) * LANE] for pp in range(NP)])
    nw = jnp.broadcast_to(nw[None], (bb, NP, 1, LANE)).reshape(bb * NP, 1, LANE)
    mean_w = (blk.astype(f32) * (1.0 / DH)).astype(bf16)

    def head_mean(x):
        hi = x.astype(bf16)
        lo = (x - hi.astype(f32)).astype(bf16)
        mw = jnp.broadcast_to(mean_w, (x.shape[0], LANE, LANE))
        return (jnp.einsum('pik,pkj->pij', hi, mw, preferred_element_type=f32)
                + jnp.einsum('pik,pkj->pij', lo, mw, preferred_element_type=f32))

    CN = cn_ref[...]
    m = mp_ref[:, 0:1, :]
    for ci in range(nc):
        F_c, A_c, a_c = chunk(F, ci), chunk(A_p, ci), chunk(a_p, ci)
        M = F_c + jnp.maximum(m, A_c)
        wD = jnp.exp(chunk(Dm, ci) - M)
        wI = jnp.exp(F_c + m - M)
        sc = chunk(s_raw, ci) * wD
        numq = (jnp.concatenate([wI, wI], axis=-1) * _bdot(chunk(Q, ci), CN)
                + jnp.einsum('pik,pkj->pij', sc.astype(bf16), chunk(vo, ci), preferred_element_type=f32))
        hh = numq[:, :, 0:LANE] / jnp.maximum(jnp.abs(numq[:, :, LANE:]), jnp.exp(-M))
        m_new = M[:, L - 1:L, :]
        f_last = F_c[:, L - 1:L, :]
        wk = jnp.exp(f_last - F_c + (a_c + F_c) - m_new)
        dc = jnp.exp(f_last + m - m_new)
        kw = chunk(K, ci) * wk
        upd = jnp.einsum('pki,pkj->pij', kw.astype(bf16), chunk(v_one, ci), preferred_element_type=f32)
        CN = jnp.concatenate([dc, dc], axis=-1) * CN + jnp.where(blk2, upd, 0.0)
        m = m_new
        hh = chunk(OG, ci) * hh
        hc = hh - head_mean(hh)
        var = head_mean(hc * hc)
        y = (hc * lax.rsqrt(var + LN_EPS) * nw).reshape(bb, NP, L, LANE)
        for bi in range(bb):
            for pp in range(NP):
                yc_ref[bi, ci * L:(ci + 1) * L, pp * LANE:(pp + 1) * LANE] = y[bi, pp]
    cn_ref[...] = CN
    mp_ref[:, 0:1, :] = m


def _mlstm_state_in(c0_ref, n0_ref, m0_ref, cn_ref, mp_ref, *, bb):
    DH, NP = ML_DH, ML_HEADS // 2
    if c0_ref is None:
        cn_ref[...] = jnp.zeros(cn_ref.shape, f32)
        mp_ref[...] = jnp.zeros(mp_ref.shape, f32)
        return
    hi_half = lax.broadcasted_iota(jnp.int32, (1, LANE), 1) >= DH
    blk = (lax.broadcasted_iota(jnp.int32, (LANE, LANE), 0) // DH
           == lax.broadcasted_iota(jnp.int32, (LANE, LANE), 1) // DH)
    for bi in range(bb):
        for pp in range(NP):
            c = c0_ref[bi, pp]
            cn_ref[bi * NP + pp, :, 0:LANE] = jnp.concatenate(
                [jnp.where(hi_half, 0.0, c), jnp.where(hi_half, c, 0.0)], axis=0)
            n_col = jnp.broadcast_to(n0_ref[bi, pp:pp + 1, :], (LANE, LANE)).T
            cn_ref[bi * NP + pp, :, LANE:] = jnp.where(blk, n_col, 0.0)
            m_a = m0_ref[bi, 0:1, 2 * pp:2 * pp + 1]
            m_b = m0_ref[bi, 0:1, 2 * pp + 1:2 * pp + 2]
            mp_ref[bi * NP + pp, 0:1, :] = jnp.where(hi_half, m_b, m_a)


def _mlstm_state_out(cn_ref, mp_ref, c_ref, n_ref, m_ref, *, bb):
    DH, NP = ML_DH, ML_HEADS // 2
    for bi in range(bb):
        for pp in range(NP):
            for hl in range(2):
                h = 2 * pp + hl
                rs = slice(hl * DH, (hl + 1) * DH)
                c_ref[bi, h] = cn_ref[bi * NP + pp, rs, hl * DH:(hl + 1) * DH]
                n_blk = cn_ref[bi * NP + pp, rs, LANE + hl * DH:LANE + (hl + 1) * DH]
                n_ref[bi, h:h + 1, :] = n_blk.T[0:1, :]
                m_ref[bi, 0:1, h:h + 1] = mp_ref[bi * NP + pp, 0:1, hl * DH:hl * DH + 1]


def _mlstm_kernel(*refs, l, first, bb, nc, L, has_state, n_prev):
    m4_ref, gates_ref, nw_ref = refs[:3]
    refs = refs[3:]
    c0_ref = n0_ref = m0_ref = None
    if has_state:
        c0_ref, n0_ref, m0_ref = refs[:3]
        refs = refs[3:]
    yc_ref, c_ref, n_ref, m_ref, cn_ref, mp_ref = refs[n_prev:]
    c_ref, n_ref, m_ref = _own_layer([c_ref, n_ref, m_ref], l, first)
    t = pl.program_id(1)

    @pl.when(t == 0)
    def _():
        _mlstm_state_in(c0_ref, n0_ref, m0_ref, cn_ref, mp_ref, bb=bb)

    _mlstm_chunks(m4_ref, gates_ref, nw_ref, yc_ref, cn_ref, mp_ref, bb=bb, nc=nc, L=L)

    @pl.when(t == pl.num_programs(1) - 1)
    def _():
        _mlstm_state_out(cn_ref, mp_ref, c_ref, n_ref, m_ref, bb=bb)


def _mlstm_call(l, m4, gates, p, states_in, prev_states, *, bb, nc, L):
    B, T, _ = m4.shape
    tt = nc * L
    tile = functools.partial(_tile_spec, bb, tt)
    npair = bb * (ML_HEADS // 2)
    return _stacked_call(
        _mlstm_kernel, l=l, B=B, bb=bb, nc=nc, L=L, grid=(B // bb, T // tt),
        inputs=[m4, gates, p["ml_norm_w"]],
        in_specs=[tile(4 * D_ML), tile(LANE), _weight_spec(l, (1, D_ML))],
        states_in=states_in, prev_states=prev_states,
        state_tails=[(ML_HEADS, ML_DH, ML_DH), (ML_HEADS, ML_DH), (1, ML_HEADS)],
        state_in_tails=[(ML_HEADS // 2, ML_DH, LANE), (ML_HEADS // 2, LANE), (1, ML_HEADS)],
        tile_shapes=[jax.ShapeDtypeStruct((B, T, D_ML), f32)], tile_specs=[tile(D_ML)],
        scratch_shapes=[pltpu.VMEM((npair, LANE, 2 * LANE), f32), pltpu.VMEM((npair, 8, LANE), f32)],
        name="mlstm")


def _out_ffn_kernel(*refs, l, first, bb, tt, has_state, n_prev):
    (x_ref, ya_ref, yb_ref, yc_ref, wo_ref, ln1_ref, wup_ref, fcw_ref, wdn_ref, ln2_ref) = refs[:10]
    refs = refs[10:]
    if has_state:
        hf_ref = refs[0]
        refs = refs[1:]
    xo_ref, tf_ref, sf_ref = refs[n_prev:]
    (tf_ref,) = _own_layer([tf_ref], l, first)
    t = pl.program_id(1)

    @pl.when(t == 0)
    def _():
        if has_state:
            sf_ref[:, HIST - 2:HIST, :] = hf_ref[...]
        else:
            sf_ref[:, 0:HIST, :] = jnp.zeros((bb, HIST, D_FF), f32)

    def stages(r0, th):
        rows = bb * th
        rs = slice(r0, r0 + th)
        x = x_ref[:, rs, :].reshape(rows, D_MODEL)
        mix = (jnp.dot(ya_ref[:, rs, :].reshape(rows, D_CONV).astype(bf16), wo_ref[0:D_CONV, :],
                       preferred_element_type=f32)
               + jnp.dot(yb_ref[:, rs, :].reshape(rows, D_GDN).astype(bf16),
                         wo_ref[D_CONV:D_CONV + D_GDN, :], preferred_element_type=f32)
               + jnp.dot(yc_ref[:, rs, :].reshape(rows, D_ML).astype(bf16), wo_ref[D_CONV + D_GDN:, :],
                         preferred_element_type=f32))
        yield
        x1 = _layer_norm(ALPHA * x + mix, ln1_ref[0:1, :], ln1_ref[1:2, :])
        x1b = x1.astype(bf16)
        yield
        acc = None
        for lo, hi in FF_CHUNKS:
            gate = jnp.dot(x1b, wup_ref[:, lo:hi], preferred_element_type=f32)
            val = jnp.dot(x1b, wup_ref[:, D_FF + lo:D_FF + hi], preferred_element_type=f32)
            yield
            sf_ref[:, HIST + r0:HIST + r0 + th, lo:hi] = gate.reshape(bb, th, hi - lo)
            conv = _causal_dwconv(sf_ref[:, r0:r0 + HIST + th, lo:hi], fcw_ref[:, lo:hi])
            hmid = _silu(conv).reshape(rows, hi - lo) * val
            yield
            part = jnp.dot(hmid.astype(bf16), wdn_ref[lo:hi, :], preferred_element_type=f32)
            acc = part if acc is None else acc + part
            yield
        x2 = _layer_norm(ALPHA * x1 + acc, ln2_ref[0:1, :], ln2_ref[1:2, :])
        xo_ref[:, rs, :] = x2.reshape(bb, th, D_MODEL)

    nh = 2 if tt % (2 * MXU_DIM) == 0 else 1
    for _ in itertools.zip_longest(*[stages(i * (tt // nh), tt // nh) for i in range(nh)]):
        pass
    tf_ref[...] = sf_ref[:, tt + HIST - 2:tt + HIST, :]
    sf_ref[:, 0:HIST, :] = sf_ref[:, tt:tt + HIST, :]


def _out_ffn_call(l, x, ya, yb, yc, p, states_in, prev_states, *, bb, tt):
    B, T, _ = x.shape
    tile = functools.partial(_tile_spec, bb, tt)
    return _stacked_call(
        _out_ffn_kernel, l=l, B=B, bb=bb, tt=tt, grid=(B // bb, T // tt),
        inputs=[x, ya, yb, yc, p["w_o"], p["ln1"], p["w_up"], p["ffn_conv_w"], p["w_down"], p["ln2"]],
        in_specs=[tile(D_MODEL), tile(D_CONV), tile(D_GDN), tile(D_ML),
                  _weight_spec(l, (D_MODEL, D_MODEL)), _weight_spec(l, (2, D_MODEL)),
                  _weight_spec(l, (D_MODEL, 2 * D_FF)), _weight_spec(l, (3, D_FF)),
                  _weight_spec(l, (D_FF, D_MODEL)), _weight_spec(l, (2, D_MODEL))],
        states_in=states_in, prev_states=prev_states, state_tails=[(2, D_FF)],
        tile_shapes=[jax.ShapeDtypeStruct((B, T, D_MODEL), f32)], tile_specs=[tile(D_MODEL)],
        scratch_shapes=[pltpu.VMEM((bb, tt + HIST, D_FF), f32)], name="out_ffn")


def _prep_params(w_in, conv_w, gdn_conv_w, gdn_a_log, gdn_dt_bias, gdn_norm_w,
                 ml_i_bias, ml_f_bias, ml_norm_w, w_o, ln1_g, ln1_b,
                 w_up, ffn_conv_w, w_down, ln2_g, ln2_b):
    gate_cols = jnp.concatenate([w_in[:, :, COL_GA:COL_MQ], w_in[:, :, COL_MI:D_IN]], axis=2)
    gate_cols = jnp.pad(gate_cols, ((0, 0), (0, 0), (0, LANE - gate_cols.shape[2])))
    w_tail = jnp.concatenate([w_in[:, :, COL_MQ:COL_MI], gate_cols], axis=2).astype(bf16)
    zeros4 = jnp.zeros((DEPTH, 4), f32)
    bias_row = jnp.pad(jnp.concatenate([gdn_dt_bias, zeros4, ml_i_bias, ml_f_bias], axis=1),
                       ((0, 0), (0, LANE - 16)))
    alog_row = jnp.pad(gdn_a_log, ((0, 0), (0, LANE - GDN_HEADS)))
    gate_p = jnp.zeros((DEPTH, 8, LANE), f32).at[:, 0].set(bias_row).at[:, 1].set(alog_row)
    return dict(
        w_in=w_in.astype(bf16), w_tail=w_tail, conv_w=conv_w, gdn_conv_w=gdn_conv_w, gate_p=gate_p,
        gdn_norm_w=gdn_norm_w.reshape(DEPTH, 1, GDN_D), ml_norm_w=ml_norm_w.reshape(DEPTH, 1, D_ML),
        w_o=w_o.astype(bf16), ln1=jnp.stack([ln1_g, ln1_b], axis=1),
        w_up=w_up.astype(bf16), ffn_conv_w=ffn_conv_w, w_down=w_down.astype(bf16),
        ln2=jnp.stack([ln2_g, ln2_b], axis=1),
    )


def _tiling(B, T):
    L = min(CHUNK, T)
    if T >= 512:
        return dict(bb=1, tt=256, fbb=1, ftt=512, L=L, rbb=min(B, 8), gnc=2, mnc=2)
    return dict(bb=256 // T, tt=T, fbb=256 // T, ftt=T, L=L, rbb=min(B, 16), gnc=T // L, mnc=T // L)


def _trunk(x, states, p):
    B, T, _ = x.shape
    c = _tiling(B, T)
    conv_new = gdn_new = ml_new = ffn_new = None
    for l in range(DEPTH):
        sel = lambda idx: None if states is None else [states[i] for i in idx]
        ya, qkv, gz, m4, gates, *conv_new = _proj_call(l, x, p, sel((0, 1)), conv_new,
                                                       bb=c["bb"], tt=c["tt"])
        yb, *gdn_new = _gdn_call(l, qkv, gates, gz, p, sel((2,)), gdn_new,
                                 bb=c["rbb"], nc=c["gnc"], L=c["L"])
        yc, *ml_new = _mlstm_call(l, m4, gates, p, sel((3, 4, 5)), ml_new,
                                  bb=c["rbb"], nc=c["mnc"], L=c["L"])
        x, *ffn_new = _out_ffn_call(l, x, ya, yb, yc, p, sel((6,)), ffn_new,
                                    bb=c["fbb"], tt=c["ftt"])
    return x, (conv_new[0], conv_new[1], gdn_new[0], ml_new[0], ml_new[1],
               ml_new[2].reshape(DEPTH, B, ML_HEADS), ffn_new[0])


def _pair_heads(c):
    d, b = c.shape[:2]
    return (c.reshape(d, b, ML_HEADS // 2, 2, ML_DH, ML_DH).transpose(0, 1, 2, 4, 3, 5)
            .reshape(d, b, ML_HEADS // 2, ML_DH, 2 * ML_DH))


def kernel(x_prompt, x_sample, state_conv_mix, state_gdn_conv, state_gdn, state_mlstm_c,
           state_mlstm_n, state_mlstm_m, state_ffn_conv,
           w_in, conv_w, gdn_conv_w, gdn_a_log, gdn_dt_bias, gdn_norm_w,
           ml_i_bias, ml_f_bias, ml_norm_w, w_o, ln1_g, ln1_b,
           w_up, ffn_conv_w, w_down, ln2_g, ln2_b):
    p = _prep_params(w_in, conv_w, gdn_conv_w, gdn_a_log, gdn_dt_bias, gdn_norm_w,
                     ml_i_bias, ml_f_bias, ml_norm_w, w_o, ln1_g, ln1_b,
                     w_up, ffn_conv_w, w_down, ln2_g, ln2_b)
    Bs = x_sample.shape[0]
    y_prompt, p_new = _trunk(x_prompt, None, p)
    s_states = (state_conv_mix, state_gdn_conv, state_gdn, _pair_heads(state_mlstm_c),
                state_mlstm_n.reshape(DEPTH, Bs, ML_HEADS // 2, 2 * ML_DH),
                state_mlstm_m.reshape(DEPTH, Bs, 1, ML_HEADS), state_ffn_conv)
    y_sample, s_new = _trunk(x_sample, s_states, p)
    return (y_prompt, y_sample, *p_new, *s_new)
```

```python
import functools
import itertools

import jax
import jax.numpy as jnp
from jax import lax
from jax.experimental import pallas as pl
from jax.experimental.pallas import tpu as pltpu

f32 = jnp.float32
bf16 = jnp.bfloat16

D_MODEL = 1024
DEPTH = 2
D_CONV = 256
GDN_HEADS = 4
GDN_D = 128
D_GDN = GDN_HEADS * GDN_D
D_QKV = 3 * D_GDN
ML_HEADS = 4
ML_DH = 64
D_ML = ML_HEADS * ML_DH
D_FF = 2816
CHUNK = 64
ALPHA = (2.0 * DEPTH) ** 0.25
LN_EPS = 1e-5
NORM_EPS = 1e-6

LANE = 128
HIST = 8
COL_A = 0
COL_G = 3 * D_CONV
COL_GA = COL_G + 4 * D_GDN
COL_MQ = COL_GA + 2 * GDN_HEADS
COL_MI = COL_MQ + 4 * D_ML
D_IN = COL_MI + 2 * ML_HEADS
D_TAIL = 4 * D_ML + LANE
G_LANE, B_LANE, I_LANE, F_LANE = 0, 4, 8, 12
MXU_DIM = 256
FF_CHUNKS = ((0, 6 * MXU_DIM), (6 * MXU_DIM, D_FF))
VMEM_LIMIT = 56 * 1024 * 1024
_PARAMS = pltpu.CompilerParams(dimension_semantics=("arbitrary", "arbitrary"),
                               vmem_limit_bytes=VMEM_LIMIT)


def _softplus(z):
    return jnp.maximum(z, 0.0) + jnp.log1p(jnp.exp(-jnp.abs(z)))


def _silu(x):
    h = 0.5 * x
    return h + h * jnp.tanh(h)


def _causal_dwconv(xh, w):
    width = w.shape[0]
    y = w[width - 1:width, :] * xh[:, HIST:, :]
    for j in range(width - 1):
        y = y + w[j:j + 1, :] * pltpu.roll(xh, width - 1 - j, axis=1)[:, HIST:, :]
    return y


def _layer_norm(x, g, b):
    mu = jnp.mean(x, -1, keepdims=True)
    xc = x - mu
    var = jnp.mean(xc * xc, -1, keepdims=True)
    return xc * lax.rsqrt(var + LN_EPS) * g + b


def _tile_spec(bb, tt, c):
    return pl.BlockSpec((bb, tt, c), lambda b, t: (b, t, 0))


def _state_spec(l, bb, tail, every_layer=False):
    if every_layer:
        return pl.BlockSpec((DEPTH, bb) + tail, lambda b, t: (0, b) + (0,) * len(tail))
    return pl.BlockSpec((None, bb) + tail, lambda b, t: (l, b) + (0,) * len(tail))


def _own_layer(refs, l, first):
    if not first:
        return refs

    @pl.when(pl.program_id(1) == 0)
    def _():
        for r in refs:
            for j in range(DEPTH):
                if j != l:
                    r[j] = jnp.zeros(r.shape[1:], f32)

    return [r.at[l] for r in refs]


def _weight_spec(l, tail):
    return pl.BlockSpec((None,) + tail, lambda b, t: (l,) + (0,) * len(tail),
                        pipeline_mode=pl.Buffered(1))


def _stacked_call(kernel, *, l, B, bb, grid, inputs, in_specs, states_in, prev_states, state_tails,
                  tile_shapes, tile_specs, scratch_shapes, name, state_in_tails=None, **static):
    first = prev_states is None
    args = list(inputs)
    specs = list(in_specs)
    if states_in is not None:
        args += list(states_in)
        specs += [_state_spec(l, bb, tail) for tail in (state_in_tails or state_tails)]
    aliases = {}
    if not first:
        for i, p in enumerate(prev_states):
            aliases[len(args)] = len(tile_shapes) + i
            args.append(p)
            specs.append(pl.BlockSpec(memory_space=pl.ANY))
    return pl.pallas_call(
        functools.partial(kernel, l=l, first=first, has_state=states_in is not None,
                          n_prev=0 if first else len(prev_states), bb=bb, **static),
        grid=grid, in_specs=specs,
        out_specs=tuple(tile_specs) + tuple(_state_spec(l, bb, tail, every_layer=first)
                                            for tail in state_tails),
        out_shape=tuple(tile_shapes) + tuple(jax.ShapeDtypeStruct((DEPTH, B) + tail, f32)
                                             for tail in state_tails),
        scratch_shapes=scratch_shapes, input_output_aliases=aliases,
        compiler_params=_PARAMS, name=name)(*args)


def _proj_kernel(*refs, l, first, bb, tt, has_state, n_prev):
    x_ref, w_ref, wt_ref, cw_ref, gcw_ref, gp_ref = refs[:6]
    refs = refs[6:]
    if has_state:
        ha_ref, hg_ref = refs[:2]
        refs = refs[2:]
    ya_ref, qkv_ref, gz_ref, m4_ref, gates_ref, ta_ref, tg_ref, sa_ref, sg_ref = refs[n_prev:]
    ta_ref, tg_ref = _own_layer([ta_ref, tg_ref], l, first)
    t = pl.program_id(1)

    @pl.when(t == 0)
    def _():
        if has_state:
            sa_ref[:, HIST - 2:HIST, :] = ha_ref[...]
            sg_ref[:, HIST - 3:HIST, :] = hg_ref[...]
        else:
            sa_ref[:, 0:HIST, :] = jnp.zeros((bb, HIST, D_CONV), f32)
            sg_ref[:, 0:HIST, :] = jnp.zeros((bb, HIST, D_QKV), f32)

    rows = bb * tt
    xb = x_ref[...].reshape(rows, D_MODEL).astype(bf16)

    def post_a(pa):
        a_b = pa[:, 0:D_CONV]
        sa_ref[:, HIST:, :] = (pa[:, D_CONV:2 * D_CONV] * pa[:, 2 * D_CONV:3 * D_CONV]).reshape(bb, tt, D_CONV)
        conv = _causal_dwconv(sa_ref[...], cw_ref[...])
        ya_ref[...] = a_b.reshape(bb, tt, D_CONV) * conv
        ta_ref[...] = sa_ref[:, tt + HIST - 2:tt + HIST, :]
        sa_ref[:, 0:HIST, :] = sa_ref[:, tt:tt + HIST, :]

    def post_qkv(c0, pg):
        w = pg.shape[1]
        sg_ref[:, HIST:, c0:c0 + w] = pg.reshape(bb, tt, w)
        for cb in range(c0 // LANE, (c0 + w) // LANE):
            cs = slice(cb * LANE, (cb + 1) * LANE)
            c = _silu(_causal_dwconv(sg_ref[:, :, cs], gcw_ref[:, cs]))
            if cb < 2 * GDN_HEADS:
                c = c * lax.rsqrt(jnp.sum(c * c, -1, keepdims=True) + NORM_EPS)
                if cb < GDN_HEADS:
                    c = c * (GDN_D ** -0.5)
            qkv_ref[:, :, cs] = c
        tg_ref[:, :, c0:c0 + w] = sg_ref[:, tt + HIST - 3:tt + HIST, c0:c0 + w]
        sg_ref[:, 0:HIST, c0:c0 + w] = sg_ref[:, tt:tt + HIST, c0:c0 + w]

    def post_gz(c0, pz):
        gz_ref[:, :, c0:c0 + pz.shape[1]] = _silu(pz).reshape(bb, tt, pz.shape[1])

    def post_m(part, pm):
        if part == 1:
            pm = pm * (ML_DH ** -0.5)
        elif part == 3:
            pm = jax.nn.sigmoid(pm)
        m4_ref[:, :, part * D_ML:(part + 1) * D_ML] = pm.reshape(bb, tt, D_ML)

    def post_gates(z):
        z = z + gp_ref[0:1, :]
        lane = lax.broadcasted_iota(jnp.int32, z.shape, 1)
        g = -jnp.exp(gp_ref[1:2, :]) * _softplus(z)
        gates = jnp.where(lane < B_LANE, g,
                          jnp.where(lane < I_LANE, jax.nn.sigmoid(z),
                                    jnp.where(lane < F_LANE, z,
                                              jnp.where(lane < F_LANE + ML_HEADS, -_softplus(-z), 0.0))))
        gates_ref[...] = gates.reshape(bb, tt, LANE)

    W = MXU_DIM
    qkv = [(w_ref, COL_G + c, COL_G + c + W, functools.partial(post_qkv, c)) for c in range(0, D_QKV, W)]
    gz = [(w_ref, COL_G + D_QKV + c, COL_G + D_QKV + c + W, functools.partial(post_gz, c))
          for c in range(0, D_GDN, W)]
    mm = [(wt_ref, i * D_ML, (i + 1) * D_ML, functools.partial(post_m, i)) for i in range(4)]
    light = [(w_ref, COL_A, COL_G, post_a)] + mm + gz + [(wt_ref, 4 * D_ML, 4 * D_ML + LANE, post_gates)]
    tasks = []
    for i in range(max(len(qkv), len(light))):
        tasks += qkv[i:i + 1] + light[i:i + 1]
    pending = None
    for ref, lo, hi, post in tasks:
        res = jnp.dot(xb, ref[:, lo:hi], preferred_element_type=f32)
        if pending is not None:
            pending[0](pending[1])
        pending = (post, res)
    pending[0](pending[1])


def _proj_call(l, x, p, states_in, prev_states, *, bb, tt):
    B, T, _ = x.shape
    tile = functools.partial(_tile_spec, bb, tt)
    widths = (D_CONV, D_QKV, D_GDN, 4 * D_ML, LANE)
    return _stacked_call(
        _proj_kernel, l=l, B=B, bb=bb, tt=tt, grid=(B // bb, T // tt),
        inputs=[x, p["w_in"], p["w_tail"], p["conv_w"], p["gdn_conv_w"], p["gate_p"]],
        in_specs=[tile(D_MODEL), _weight_spec(l, (D_MODEL, D_IN)), _weight_spec(l, (D_MODEL, D_TAIL)),
                  _weight_spec(l, (3, D_CONV)), _weight_spec(l, (4, D_QKV)), _weight_spec(l, (8, LANE))],
        states_in=states_in, prev_states=prev_states, state_tails=[(2, D_CONV), (3, D_QKV)],
        tile_shapes=[jax.ShapeDtypeStruct((B, T, w), f32) for w in widths],
        tile_specs=[tile(w) for w in widths],
        scratch_shapes=[pltpu.VMEM((bb, tt + HIST, D_CONV), f32),
                        pltpu.VMEM((bb, tt + HIST, D_QKV), f32)],
        name="in_proj")


def _bdot(a, b):
    return jnp.einsum('pik,pkj->pij', a.astype(bf16), b.astype(bf16), preferred_element_type=f32)


def _bdot_nt(a, b):
    return jnp.einsum('pik,pjk->pij', a.astype(bf16), b.astype(bf16), preferred_element_type=f32)


def _bdot_tn(a, b):
    return jnp.einsum('pki,pkj->pij', a.astype(bf16), b.astype(bf16), preferred_element_type=f32)


def _split3(x):
    hi = x.astype(bf16)
    r1 = x - hi.astype(f32)
    mid = r1.astype(bf16)
    lo = (r1 - mid.astype(f32)).astype(bf16)
    return hi, mid, lo


def _gate_tiles(gates_ref, *, bb, nc, L):
    gt = jnp.stack([gates_ref[bi, ci * L:(ci + 1) * L, :] for bi in range(bb) for ci in range(nc)])
    tril = lax.broadcasted_iota(jnp.int32, (L, L), 1) <= lax.broadcasted_iota(jnp.int32, (L, L), 0)
    trb = jnp.broadcast_to(tril.astype(bf16), (bb * nc, L, L))
    cs = sum(jnp.einsum('pij,pjk->pik', trb, t, preferred_element_type=f32) for t in _split3(gt))
    return gt, cs


def _rows_from_cols(x, eye2):
    L = x.shape[1]
    onesb = jnp.ones((x.shape[0], L, L), bf16)
    return sum(jnp.einsum('pij,pjk->pik', onesb, t, preferred_element_type=f32)
               for t in _split3(x * eye2))


def _gdn_kernel(*refs, l, first, bb, nc, L, has_state, n_prev):
    qkv_ref, gates_ref, gz_ref, nw_ref = refs[:4]
    refs = refs[4:]
    if has_state:
        s0_ref = refs[0]
        refs = refs[1:]
    yb_ref, s_ref = refs[n_prev:]
    (s_ref,) = _own_layer([s_ref], l, first)
    t = pl.program_id(1)

    @pl.when(t == 0)
    def _():
        s_ref[...] = s0_ref[...] if has_state else jnp.zeros(s_ref.shape, f32)

    NP = GDN_HEADS // 2
    W2 = 2 * L
    nbc, npair = bb * nc, bb * nc * NP
    hi2 = lax.broadcasted_iota(jnp.int32, (1, W2), 1) >= L
    row = lax.broadcasted_iota(jnp.int32, (L, W2), 0)
    col = lax.broadcasted_iota(jnp.int32, (L, W2), 1) % L
    causal2, strict2 = col <= row, col < row
    eye2 = (col == row).astype(f32)
    def head_tiles(col0, hl):
        return jnp.stack([qkv_ref[bi, ci * L:(ci + 1) * L,
                                  col0 + (2 * pp + hl) * GDN_D:col0 + (2 * pp + hl + 1) * GDN_D]
                          for bi in range(bb) for ci in range(nc) for pp in range(NP)])

    def pair_col(x, lane0, hl):
        return jnp.stack([x[:, :, lane0 + 2 * pp + hl:lane0 + 2 * pp + hl + 1] for pp in range(NP)],
                         axis=1).reshape(npair, L, 1)

    gt, cs = _gate_tiles(gates_ref, bb=bb, nc=nc, L=L)
    G = [pair_col(cs, G_LANE, hl) for hl in range(2)]
    beta = [pair_col(gt, B_LANE, hl) for hl in range(2)]
    g_col = jnp.where(hi2, G[1], G[0])
    g_row = _rows_from_cols(g_col, eye2)
    decay = jnp.where(causal2, jnp.exp(jnp.where(causal2, g_col - g_row, 0.0)), 0.0)
    q = [head_tiles(0, hl) for hl in range(2)]
    k = [head_tiles(D_GDN, hl) for hl in range(2)]
    v = [head_tiles(2 * D_GDN, hl) for hl in range(2)]
    kb = [k[hl] * beta[hl] for hl in range(2)]
    zd = jnp.zeros((npair, L, GDN_D), f32)
    lhs = jnp.concatenate([jnp.concatenate(kb, axis=-1), jnp.concatenate(q, axis=-1)], axis=1)
    k_bd = jnp.concatenate([jnp.concatenate([k[0], zd], axis=-1),
                            jnp.concatenate([zd, k[1]], axis=-1)], axis=1)
    kkqk = _bdot_nt(lhs, k_bd)
    kk = kkqk[:, 0:L] * decay
    qk = kkqk[:, L:] * decay
    def block_diag(x):
        return jnp.concatenate([jnp.where(hi2, 0.0, x), jnp.where(hi2, x, 0.0)], axis=1)

    N = jnp.where(strict2, kk, 0.0)
    R = None
    b = 1
    while b < L:
        lower_left = (row // (2 * b) == col // (2 * b)) & (row % (2 * b) >= b) & (col % (2 * b) < b)
        c = jnp.where(lower_left, N, 0.0)
        if R is None:
            R = -c
        else:
            y = c + _bdot(c, block_diag(R))
            R = R - y - _bdot(R, block_diag(y))
        b *= 2
    eG = [jnp.exp(G[hl]) for hl in range(2)]
    rhs = [jnp.concatenate([v[hl] * beta[hl], kb[hl] * eG[hl]], axis=-1) for hl in range(2)]
    z2 = jnp.zeros((npair, L, 2 * GDN_D), f32)
    w = [rhs[0] + _bdot(R, jnp.concatenate([rhs[0], z2], axis=1)),
         rhs[1] + _bdot(R, jnp.concatenate([z2, rhs[1]], axis=1))]
    qg = [q[hl] * eG[hl] for hl in range(2)]
    g_last = [G[hl][:, L - 1:L, :] for hl in range(2)]
    kdec = [k[hl] * jnp.exp(g_last[hl] - G[hl]) for hl in range(2)]
    e_last = [jnp.exp(g_last[hl]) for hl in range(2)]

    def chunk(xs, ci):
        return jnp.concatenate([x.reshape((bb, nc, NP) + x.shape[1:])[:, ci]
                                .reshape((bb * NP,) + x.shape[1:]) for x in xs], axis=0)

    order = [(bi, 2 * pp + hl) for hl in range(2) for bi in range(bb) for pp in range(NP)]
    nw = nw_ref[...]
    S = jnp.stack([s_ref[bi, h] for bi, h in order])
    zu = jnp.zeros((bb * NP, L, GDN_D), f32)
    for ci in range(nc):
        rs = slice(ci * L, (ci + 1) * L)
        w_c = chunk(w, ci)
        xs = _bdot(jnp.concatenate([w_c[:, :, GDN_D:], chunk(qg, ci)], axis=1), S)
        u = w_c[:, :, 0:GDN_D] - xs[:, 0:L]
        u_bd = jnp.concatenate([jnp.concatenate([u[0:bb * NP], zu], axis=1),
                                jnp.concatenate([zu, u[bb * NP:]], axis=1)], axis=0)
        o = xs[:, L:] + _bdot(chunk([qk, qk], ci), u_bd)
        S = S * chunk(e_last, ci) + _bdot_tn(chunk(kdec, ci), u)
        o = o * lax.rsqrt(jnp.mean(o * o, -1, keepdims=True) + NORM_EPS) * nw
        for i, (bi, h) in enumerate(order):
            hs = slice(h * GDN_D, (h + 1) * GDN_D)
            yb_ref[bi, rs, hs] = o[i] * gz_ref[bi, rs, hs]
    for i, (bi, h) in enumerate(order):
        s_ref[bi, h] = S[i]


def _gdn_call(l, qkv, gates, gz, p, states_in, prev_states, *, bb, nc, L):
    B, T, _ = qkv.shape
    tt = nc * L
    tile = functools.partial(_tile_spec, bb, tt)
    return _stacked_call(
        _gdn_kernel, l=l, B=B, bb=bb, nc=nc, L=L, grid=(B // bb, T // tt),
        inputs=[qkv, gates, gz, p["gdn_norm_w"]],
        in_specs=[tile(D_QKV), tile(LANE), tile(D_GDN), _weight_spec(l, (1, GDN_D))],
        states_in=states_in, prev_states=prev_states, state_tails=[(GDN_HEADS, GDN_D, GDN_D)],
        tile_shapes=[jax.ShapeDtypeStruct((B, T, D_GDN), f32)], tile_specs=[tile(D_GDN)],
        scratch_shapes=[], name="gdn")


def _mlstm_chunks(m4_ref, gates_ref, nw_ref, yc_ref, cn_ref, mp_ref, *, bb, nc, L):
    DH = ML_DH
    NP = ML_HEADS // 2
    nbc, npair = bb * nc, bb * nc * NP
    lane = lax.broadcasted_iota(jnp.int32, (1, LANE), 1)
    hi_half = lane >= DH
    row = lax.broadcasted_iota(jnp.int32, (L, LANE), 0)
    col = lax.broadcasted_iota(jnp.int32, (L, LANE), 1) % DH
    causal2 = col <= row
    eye2 = (col == row).astype(f32)
    r2 = lax.broadcasted_iota(jnp.int32, (LANE, LANE), 0) // DH
    c2 = lax.broadcasted_iota(jnp.int32, (LANE, LANE), 1) // DH
    blk = r2 == c2
    blk2 = jnp.concatenate([blk, blk], axis=1)
    ones_bd = blk.astype(bf16)

    def tiles(col0):
        return jnp.stack([m4_ref[bi, ci * L:(ci + 1) * L, col0 + pp * LANE:col0 + (pp + 1) * LANE]
                          for bi in range(bb) for ci in range(nc) for pp in range(NP)])

    def pair_cols(x, lane0):
        a = jnp.stack([x[:, :, lane0 + 2 * pp:lane0 + 2 * pp + 1] for pp in range(NP)], axis=1)
        b = jnp.stack([x[:, :, lane0 + 2 * pp + 1:lane0 + 2 * pp + 2] for pp in range(NP)], axis=1)
        return jnp.where(hi_half, b, a).reshape(npair, L, LANE)

    gt, cs = _gate_tiles(gates_ref, bb=bb, nc=nc, L=L)
    a_t = pltpu.roll(gt, F_LANE - I_LANE, axis=2) - cs
    A_t = a_t
    rows3 = lax.broadcasted_iota(jnp.int32, a_t.shape, 1)
    sh = 1
    while sh < L:
        A_t = jnp.maximum(A_t, jnp.where(rows3 >= sh, pltpu.roll(A_t, sh, axis=1), -jnp.inf))
        sh *= 2
    F = pair_cols(cs, F_LANE)
    a_p = pair_cols(a_t, F_LANE)
    A_p = pair_cols(A_t, F_LANE)
    Dm = jnp.where(causal2, F + _rows_from_cols(a_p, eye2), -jnp.inf)
    Q = tiles(0)
    K = tiles(D_ML)
    V = tiles(2 * D_ML)
    OG = tiles(3 * D_ML)
    if L < DH:
        zpad = jnp.zeros((npair, DH - L, LANE), f32)
        Kp, Vp = jnp.concatenate([K, zpad], axis=1), jnp.concatenate([V, zpad], axis=1)
    else:
        Kp, Vp = K, V
    k_bd = jnp.concatenate([jnp.where(hi_half, 0.0, Kp), jnp.where(hi_half, Kp, 0.0)], axis=1).astype(bf16)
    v_bd = jnp.concatenate([jnp.where(hi_half, 0.0, Vp), jnp.where(hi_half, Vp, 0.0)], axis=1).astype(bf16)
    vo = jnp.concatenate([v_bd, jnp.broadcast_to(ones_bd, (npair, LANE, LANE))], axis=-1)
    s_raw = jnp.einsum('pik,pjk->pij', Q.astype(bf16), k_bd, preferred_element_type=f32)
    v_one = jnp.concatenate([V, jnp.ones((npair, L, LANE), f32)], axis=-1).astype(bf16)

    def chunk(x, ci):
        return x.reshape((bb, nc, NP) + x.shape[1:])[:, ci].reshape((bb * NP,) + x.shape[1:])

    nw = jnp.stack([nw_ref[0:1, pp * LANE:(pp + 1) * LANE] for pp in range(NP)])
    nw = jnp.broadcast_to(nw[None], (bb, NP, 1, LANE)).reshape(bb * NP, 1, LANE)
    mean_w = (blk.astype(f32) * (1.0 / DH)).astype(bf16)

    def head_mean(x):
        hi = x.astype(bf16)
        lo = (x - hi.astype(f32)).astype(bf16)
        mw = jnp.broadcast_to(mean_w, (x.shape[0], LANE, LANE))
        return (jnp.einsum('pik,pkj->pij', hi, mw, preferred_element_type=f32)
                + jnp.einsum('pik,pkj->pij', lo, mw, preferred_element_type=f32))

    CN = cn_ref[...]
    m = mp_ref[:, 0:1, :]
    for ci in range(nc):
        F_c, A_c, a_c = chunk(F, ci), chunk(A_p, ci), chunk(a_p, ci)
        M = F_c + jnp.maximum(m, A_c)
        wD = jnp.exp(chunk(Dm, ci) - M)
        wI = jnp.exp(F_c + m - M)
        sc = chunk(s_raw, ci) * wD
        numq = (jnp.concatenate([wI, wI], axis=-1) * _bdot(chunk(Q, ci), CN)
                + jnp.einsum('pik,pkj->pij', sc.astype(bf16), chunk(vo, ci), preferred_element_type=f32))
        hh = numq[:, :, 0:LANE] / jnp.maximum(jnp.abs(numq[:, :, LANE:]), jnp.exp(-M))
        m_new = M[:, L - 1:L, :]
        f_last = F_c[:, L - 1:L, :]
        wk = jnp.exp(f_last - F_c + (a_c + F_c) - m_new)
        dc = jnp.exp(f_last + m - m_new)
        kw = chunk(K, ci) * wk
        upd = jnp.einsum('pki,pkj->pij', kw.astype(bf16), chunk(v_one, ci), preferred_element_type=f32)
        CN = jnp.concatenate([dc, dc], axis=-1) * CN + jnp.where(blk2, upd, 0.0)
        m = m_new
        hh = chunk(OG, ci) * hh
        hc = hh - head_mean(hh)
        var = head_mean(hc * hc)
        y = (hc * lax.rsqrt(var + LN_EPS) * nw).reshape(bb, NP, L, LANE)
        for bi in range(bb):
            for pp in range(NP):
                yc_ref[bi, ci * L:(ci + 1) * L, pp * LANE:(pp + 1) * LANE] = y[bi, pp]
    cn_ref[...] = CN
    mp_ref[:, 0:1, :] = m


def _mlstm_state_in(c0_ref, n0_ref, m0_ref, cn_ref, mp_ref, *, bb):
    DH, NP = ML_DH, ML_HEADS // 2
    if c0_ref is None:
        cn_ref[...] = jnp.zeros(cn_ref.shape, f32)
        mp_ref[...] = jnp.zeros(mp_ref.shape, f32)
        return
    hi_half = lax.broadcasted_iota(jnp.int32, (1, LANE), 1) >= DH
    blk = (lax.broadcasted_iota(jnp.int32, (LANE, LANE), 0) // DH
           == lax.broadcasted_iota(jnp.int32, (LANE, LANE), 1) // DH)
    for bi in range(bb):
        for pp in range(NP):
            c = c0_ref[bi, pp]
            cn_ref[bi * NP + pp, :, 0:LANE] = jnp.concatenate(
                [jnp.where(hi_half, 0.0, c), jnp.where(hi_half, c, 0.0)], axis=0)
            n_col = jnp.broadcast_to(n0_ref[bi, pp:pp + 1, :], (LANE, LANE)).T
            cn_ref[bi * NP + pp, :, LANE:] = jnp.where(blk, n_col, 0.0)
            m_a = m0_ref[bi, 0:1, 2 * pp:2 * pp + 1]
            m_b = m0_ref[bi, 0:1, 2 * pp + 1:2 * pp + 2]
            mp_ref[bi * NP + pp, 0:1, :] = jnp.where(hi_half, m_b, m_a)


def _mlstm_state_out(cn_ref, mp_ref, c_ref, n_ref, m_ref, *, bb):
    DH, NP = ML_DH, ML_HEADS // 2
    for bi in range(bb):
        for pp in range(NP):
            for hl in range(2):
                h = 2 * pp + hl
                rs = slice(hl * DH, (hl + 1) * DH)
                c_ref[bi, h] = cn_ref[bi * NP + pp, rs, hl * DH:(hl + 1) * DH]
                n_blk = cn_ref[bi * NP + pp, rs, LANE + hl * DH:LANE + (hl + 1) * DH]
                n_ref[bi, h:h + 1, :] = n_blk.T[0:1, :]
                m_ref[bi, 0:1, h:h + 1] = mp_ref[bi * NP + pp, 0:1, hl * DH:hl * DH + 1]


def _mlstm_kernel(*refs, l, first, bb, nc, L, has_state, n_prev):
    m4_ref, gates_ref, nw_ref = refs[:3]
    refs = refs[3:]
    c0_ref = n0_ref = m0_ref = None
    if has_state:
        c0_ref, n0_ref, m0_ref = refs[:3]
        refs = refs[3:]
    yc_ref, c_ref, n_ref, m_ref, cn_ref, mp_ref = refs[n_prev:]
    c_ref, n_ref, m_ref = _own_layer([c_ref, n_ref, m_ref], l, first)
    t = pl.program_id(1)

    @pl.when(t == 0)
    def _():
        _mlstm_state_in(c0_ref, n0_ref, m0_ref, cn_ref, mp_ref, bb=bb)

    _mlstm_chunks(m4_ref, gates_ref, nw_ref, yc_ref, cn_ref, mp_ref, bb=bb, nc=nc, L=L)

    @pl.when(t == pl.num_programs(1) - 1)
    def _():
        _mlstm_state_out(cn_ref, mp_ref, c_ref, n_ref, m_ref, bb=bb)


def _mlstm_call(l, m4, gates, p, states_in, prev_states, *, bb, nc, L):
    B, T, _ = m4.shape
    tt = nc * L
    tile = functools.partial(_tile_spec, bb, tt)
    npair = bb * (ML_HEADS // 2)
    return _stacked_call(
        _mlstm_kernel, l=l, B=B, bb=bb, nc=nc, L=L, grid=(B // bb, T // tt),
        inputs=[m4, gates, p["ml_norm_w"]],
        in_specs=[tile(4 * D_ML), tile(LANE), _weight_spec(l, (1, D_ML))],
        states_in=states_in, prev_states=prev_states,
        state_tails=[(ML_HEADS, ML_DH, ML_DH), (ML_HEADS, ML_DH), (1, ML_HEADS)],
        state_in_tails=[(ML_HEADS // 2, ML_DH, LANE), (ML_HEADS // 2, LANE), (1, ML_HEADS)],
        tile_shapes=[jax.ShapeDtypeStruct((B, T, D_ML), f32)], tile_specs=[tile(D_ML)],
        scratch_shapes=[pltpu.VMEM((npair, LANE, 2 * LANE), f32), pltpu.VMEM((npair, 8, LANE), f32)],
        name="mlstm")


def _out_ffn_kernel(*refs, l, first, bb, tt, has_state, n_prev):
    (x_ref, ya_ref, yb_ref, yc_ref, wo_ref, ln1_ref, wup_ref, fcw_ref, wdn_ref, ln2_ref) = refs[:10]
    refs = refs[10:]
    if has_state:
        hf_ref = refs[0]
        refs = refs[1:]
    xo_ref, tf_ref, sf_ref = refs[n_prev:]
    (tf_ref,) = _own_layer([tf_ref], l, first)
    t = pl.program_id(1)

    @pl.when(t == 0)
    def _():
        if has_state:
            sf_ref[:, HIST - 2:HIST, :] = hf_ref[...]
        else:
            sf_ref[:, 0:HIST, :] = jnp.zeros((bb, HIST, D_FF), f32)

    def stages(r0, th):
        rows = bb * th
        rs = slice(r0, r0 + th)
        x = x_ref[:, rs, :].reshape(rows, D_MODEL)
        mix = (jnp.dot(ya_ref[:, rs, :].reshape(rows, D_CONV).astype(bf16), wo_ref[0:D_CONV, :],
                       preferred_element_type=f32)
               + jnp.dot(yb_ref[:, rs, :].reshape(rows, D_GDN).astype(bf16),
                         wo_ref[D_CONV:D_CONV + D_GDN, :], preferred_element_type=f32)
               + jnp.dot(yc_ref[:, rs, :].reshape(rows, D_ML).astype(bf16), wo_ref[D_CONV + D_GDN:, :],
                         preferred_element_type=f32))
        yield
        x1 = _layer_norm(ALPHA * x + mix, ln1_ref[0:1, :], ln1_ref[1:2, :])
        x1b = x1.astype(bf16)
        yield
        acc = None
        for lo, hi in FF_CHUNKS:
            gate = jnp.dot(x1b, wup_ref[:, lo:hi], preferred_element_type=f32)
            val = jnp.dot(x1b, wup_ref[:, D_FF + lo:D_FF + hi], preferred_element_type=f32)
            yield
            sf_ref[:, HIST + r0:HIST + r0 + th, lo:hi] = gate.reshape(bb, th, hi - lo)
            conv = _causal_dwconv(sf_ref[:, r0:r0 + HIST + th, lo:hi], fcw_ref[:, lo:hi])
            hmid = _silu(conv).reshape(rows, hi - lo) * val
            yield
            part = jnp.dot(hmid.astype(bf16), wdn_ref[lo:hi, :], preferred_element_type=f32)
            acc = part if acc is None else acc + part
            yield
        x2 = _layer_norm(ALPHA * x1 + acc, ln2_ref[0:1, :], ln2_ref[1:2, :])
        xo_ref[:, rs, :] = x2.reshape(bb, th, D_MODEL)

    nh = 2 if tt % (2 * MXU_DIM) == 0 else 1
    for _ in itertools.zip_longest(*[stages(i * (tt // nh), tt // nh) for i in range(nh)]):
        pass
    tf_ref[...] = sf_ref[:, tt + HIST - 2:tt + HIST, :]
    sf_ref[:, 0:HIST, :] = sf_ref[:, tt:tt + HIST, :]


def _out_ffn_call(l, x, ya, yb, yc, p, states_in, prev_states, *, bb, tt):
    B, T, _ = x.shape
    tile = functools.partial(_tile_spec, bb, tt)
    return _stacked_call(
        _out_ffn_kernel, l=l, B=B, bb=bb, tt=tt, grid=(B // bb, T // tt),
        inputs=[x, ya, yb, yc, p["w_o"], p["ln1"], p["w_up"], p["ffn_conv_w"], p["w_down"], p["ln2"]],
        in_specs=[tile(D_MODEL), tile(D_CONV), tile(D_GDN), tile(D_ML),
                  _weight_spec(l, (D_MODEL, D_MODEL)), _weight_spec(l, (2, D_MODEL)),
                  _weight_spec(l, (D_MODEL, 2 * D_FF)), _weight_spec(l, (3, D_FF)),
                  _weight_spec(l, (D_FF, D_MODEL)), _weight_spec(l, (2, D_MODEL))],
        states_in=states_in, prev_states=prev_states, state_tails=[(2, D_FF)],
        tile_shapes=[jax.ShapeDtypeStruct((B, T, D_MODEL), f32)], tile_specs=[tile(D_MODEL)],
        scratch_shapes=[pltpu.VMEM((bb, tt + HIST, D_FF), f32)], name="out_ffn")


def _prep_params(w_in, conv_w, gdn_conv_w, gdn_a_log, gdn_dt_bias, gdn_norm_w,
                 ml_i_bias, ml_f_bias, ml_norm_w, w_o, ln1_g, ln1_b,
                 w_up, ffn_conv_w, w_down, ln2_g, ln2_b):
    gate_cols = jnp.concatenate([w_in[:, :, COL_GA:COL_MQ], w_in[:, :, COL_MI:D_IN]], axis=2)
    gate_cols = jnp.pad(gate_cols, ((0, 0), (0, 0), (0, LANE - gate_cols.shape[2])))
    w_tail = jnp.concatenate([w_in[:, :, COL_MQ:COL_MI], gate_cols], axis=2).astype(bf16)
    zeros4 = jnp.zeros((DEPTH, 4), f32)
    bias_row = jnp.pad(jnp.concatenate([gdn_dt_bias, zeros4, ml_i_bias, ml_f_bias], axis=1),
                       ((0, 0), (0, LANE - 16)))
    alog_row = jnp.pad(gdn_a_log, ((0, 0), (0, LANE - GDN_HEADS)))
    gate_p = jnp.zeros((DEPTH, 8, LANE), f32).at[:, 0].set(bias_row).at[:, 1].set(alog_row)
    return dict(
        w_in=w_in.astype(bf16), w_tail=w_tail, conv_w=conv_w, gdn_conv_w=gdn_conv_w, gate_p=gate_p,
        gdn_norm_w=gdn_norm_w.reshape(DEPTH, 1, GDN_D), ml_norm_w=ml_norm_w.reshape(DEPTH, 1, D_ML),
        w_o=w_o.astype(bf16), ln1=jnp.stack([ln1_g, ln1_b], axis=1),
        w_up=w_up.astype(bf16), ffn_conv_w=ffn_conv_w, w_down=w_down.astype(bf16),
        ln2=jnp.stack([ln2_g, ln2_b], axis=1),
    )


def _tiling(B, T):
    L = min(CHUNK, T)
    if T >= 512:
        return dict(bb=1, tt=256, fbb=1, ftt=512, L=L, rbb=min(B, 8), gnc=2, mnc=2)
    return dict(bb=256 // T, tt=T, fbb=256 // T, ftt=T, L=L, rbb=min(B, 16), gnc=T // L, mnc=T // L)


def _trunk(x, states, p):
    B, T, _ = x.shape
    c = _tiling(B, T)
    conv_new = gdn_new = ml_new = ffn_new = None
    for l in range(DEPTH):
        sel = lambda idx: None if states is None else [states[i] for i in idx]
        ya, qkv, gz, m4, gates, *conv_new = _proj_call(l, x, p, sel((0, 1)), conv_new,
                                                       bb=c["bb"], tt=c["tt"])
        yb, *gdn_new = _gdn_call(l, qkv, gates, gz, p, sel((2,)), gdn_new,
                                 bb=c["rbb"], nc=c["gnc"], L=c["L"])
        yc, *ml_new = _mlstm_call(l, m4, gates, p, sel((3, 4, 5)), ml_new,
                                  bb=c["rbb"], nc=c["mnc"], L=c["L"])
        x, *ffn_new = _out_ffn_call(l, x, ya, yb, yc, p, sel((6,)), ffn_new,
                                    bb=c["fbb"], tt=c["ftt"])
    return x, (conv_new[0], conv_new[1], gdn_new[0], ml_new[0], ml_new[1],
               ml_new[2].reshape(DEPTH, B, ML_HEADS), ffn_new[0])


def _pair_heads(c):
    d, b = c.shape[:2]
    return (c.reshape(d, b, ML_HEADS // 2, 2, ML_DH, ML_DH).transpose(0, 1, 2, 4, 3, 5)
            .reshape(d, b, ML_HEADS // 2, ML_DH, 2 * ML_DH))


def kernel(x_prompt, x_sample, state_conv_mix, state_gdn_conv, state_gdn, state_mlstm_c,
           state_mlstm_n, state_mlstm_m, state_ffn_conv,
           w_in, conv_w, gdn_conv_w, gdn_a_log, gdn_dt_bias, gdn_norm_w,
           ml_i_bias, ml_f_bias, ml_norm_w, w_o, ln1_g, ln1_b,
           w_up, ffn_conv_w, w_down, ln2_g, ln2_b):
    p = _prep_params(w_in, conv_w, gdn_conv_w, gdn_a_log, gdn_dt_bias, gdn_norm_w,
                     ml_i_bias, ml_f_bias, ml_norm_w, w_o, ln1_g, ln1_b,
                     w_up, ffn_conv_w, w_down, ln2_g, ln2_b)
    Bs = x_sample.shape[0]
    y_prompt, p_new = _trunk(x_prompt, None, p)
    s_states = (state_conv_mix, state_gdn_conv, state_gdn, _pair_heads(state_mlstm_c),
                state_mlstm_n.reshape(DEPTH, Bs, ML_HEADS // 2, 2 * ML_DH),
                state_mlstm_m.reshape(DEPTH, Bs, 1, ML_HEADS), state_ffn_conv)
    y_sample, s_new = _trunk(x_sample, s_states, p)
    return (y_prompt, y_sample, *p_new, *s_new)
```

```python
import functools
import itertools

import jax
import jax.numpy as jnp
from jax import lax
from jax.experimental import pallas as pl
from jax.experimental.pallas import tpu as pltpu

f32 = jnp.float32
bf16 = jnp.bfloat16

D_MODEL = 1024
DEPTH = 2
D_CONV = 256
GDN_HEADS = 4
GDN_D = 128
D_GDN = GDN_HEADS * GDN_D
D_QKV = 3 * D_GDN
ML_HEADS = 4
ML_DH = 64
D_ML = ML_HEADS * ML_DH
D_FF = 2816
CHUNK = 64
ALPHA = (2.0 * DEPTH) ** 0.25
LN_EPS = 1e-5
NORM_EPS = 1e-6

LANE = 128
HIST = 8
COL_A = 0
COL_G = 3 * D_CONV
COL_GA = COL_G + 4 * D_GDN
COL_MQ = COL_GA + 2 * GDN_HEADS
COL_MI = COL_MQ + 4 * D_ML
D_IN = COL_MI + 2 * ML_HEADS
D_TAIL = 4 * D_ML + LANE
G_LANE, B_LANE, I_LANE, F_LANE = 0, 4, 8, 12
MXU_DIM = 256
FF_CHUNKS = ((0, 6 * MXU_DIM), (6 * MXU_DIM, D_FF))
VMEM_LIMIT = 56 * 1024 * 1024
_PARAMS = pltpu.CompilerParams(dimension_semantics=("arbitrary", "arbitrary"),
                               vmem_limit_bytes=VMEM_LIMIT)


def _softplus(z):
    return jnp.maximum(z, 0.0) + jnp.log1p(jnp.exp(-jnp.abs(z)))


def _silu(x):
    h = 0.5 * x
    return h + h * jnp.tanh(h)


def _causal_dwconv(xh, w):
    width = w.shape[0]
    y = w[width - 1:width, :] * xh[:, HIST:, :]
    for j in range(width - 1):
        y = y + w[j:j + 1, :] * pltpu.roll(xh, width - 1 - j, axis=1)[:, HIST:, :]
    return y


def _layer_norm(x, g, b):
    mu = jnp.mean(x, -1, keepdims=True)
    xc = x - mu
    var = jnp.mean(xc * xc, -1, keepdims=True)
    return xc * lax.rsqrt(var + LN_EPS) * g + b


def _tile_spec(bb, tt, c):
    return pl.BlockSpec((bb, tt, c), lambda b, t: (b, t, 0))


def _state_spec(l, bb, tail, every_layer=False):
    if every_layer:
        return pl.BlockSpec((DEPTH, bb) + tail, lambda b, t: (0, b) + (0,) * len(tail))
    return pl.BlockSpec((None, bb) + tail, lambda b, t: (l, b) + (0,) * len(tail))


def _own_layer(refs, l, first):
    if not first:
        return refs

    @pl.when(pl.program_id(1) == 0)
    def _():
        for r in refs:
            for j in range(DEPTH):
                if j != l:
                    r[j] = jnp.zeros(r.shape[1:], f32)

    return [r.at[l] for r in refs]


def _weight_spec(l, tail):
    return pl.BlockSpec((None,) + tail, lambda b, t: (l,) + (0,) * len(tail),
                        pipeline_mode=pl.Buffered(1))


def _stacked_call(kernel, *, l, B, bb, grid, inputs, in_specs, states_in, prev_states, state_tails,
                  tile_shapes, tile_specs, scratch_shapes, name, state_in_tails=None, **static):
    first = prev_states is None
    args = list(inputs)
    specs = list(in_specs)
    if states_in is not None:
        args += list(states_in)
        specs += [_state_spec(l, bb, tail) for tail in (state_in_tails or state_tails)]
    aliases = {}
    if not first:
        for i, p in enumerate(prev_states):
            aliases[len(args)] = len(tile_shapes) + i
            args.append(p)
            specs.append(pl.BlockSpec(memory_space=pl.ANY))
    return pl.pallas_call(
        functools.partial(kernel, l=l, first=first, has_state=states_in is not None,
                          n_prev=0 if first else len(prev_states), bb=bb, **static),
        grid=grid, in_specs=specs,
        out_specs=tuple(tile_specs) + tuple(_state_spec(l, bb, tail, every_layer=first)
                                            for tail in state_tails),
        out_shape=tuple(tile_shapes) + tuple(jax.ShapeDtypeStruct((DEPTH, B) + tail, f32)
                                             for tail in state_tails),
        scratch_shapes=scratch_shapes, input_output_aliases=aliases,
        compiler_params=_PARAMS, name=name)(*args)


def _proj_kernel(*refs, l, first, bb, tt, has_state, n_prev):
    x_ref, w_ref, wt_ref, cw_ref, gcw_ref, gp_ref = refs[:6]
    refs = refs[6:]
    if has_state:
        ha_ref, hg_ref = refs[:2]
        refs = refs[2:]
    ya_ref, qkv_ref, gz_ref, m4_ref, gates_ref, ta_ref, tg_ref, sa_ref, sg_ref = refs[n_prev:]
    ta_ref, tg_ref = _own_layer([ta_ref, tg_ref], l, first)
    t = pl.program_id(1)

    @pl.when(t == 0)
    def _():
        if has_state:
            sa_ref[:, HIST - 2:HIST, :] = ha_ref[...]
            sg_ref[:, HIST - 3:HIST, :] = hg_ref[...]
        else:
            sa_ref[:, 0:HIST, :] = jnp.zeros((bb, HIST, D_CONV), f32)
            sg_ref[:, 0:HIST, :] = jnp.zeros((bb, HIST, D_QKV), f32)

    rows = bb * tt
    xb = x_ref[...].reshape(rows, D_MODEL).astype(bf16)

    def post_a(pa):
        a_b = pa[:, 0:D_CONV]
        sa_ref[:, HIST:, :] = (pa[:, D_CONV:2 * D_CONV] * pa[:, 2 * D_CONV:3 * D_CONV]).reshape(bb, tt, D_CONV)
        conv = _causal_dwconv(sa_ref[...], cw_ref[...])
        ya_ref[...] = a_b.reshape(bb, tt, D_CONV) * conv
        ta_ref[...] = sa_ref[:, tt + HIST - 2:tt + HIST, :]
        sa_ref[:, 0:HIST, :] = sa_ref[:, tt:tt + HIST, :]

    def post_qkv(c0, pg):
        w = pg.shape[1]
        sg_ref[:, HIST:, c0:c0 + w] = pg.reshape(bb, tt, w)
        for cb in range(c0 // LANE, (c0 + w) // LANE):
            cs = slice(cb * LANE, (cb + 1) * LANE)
            c = _silu(_causal_dwconv(sg_ref[:, :, cs], gcw_ref[:, cs]))
            if cb < 2 * GDN_HEADS:
                c = c * lax.rsqrt(jnp.sum(c * c, -1, keepdims=True) + NORM_EPS)
                if cb < GDN_HEADS:
                    c = c * (GDN_D ** -0.5)
            qkv_ref[:, :, cs] = c
        tg_ref[:, :, c0:c0 + w] = sg_ref[:, tt + HIST - 3:tt + HIST, c0:c0 + w]
        sg_ref[:, 0:HIST, c0:c0 + w] = sg_ref[:, tt:tt + HIST, c0:c0 + w]

    def post_gz(c0, pz):
        gz_ref[:, :, c0:c0 + pz.shape[1]] = _silu(pz).reshape(bb, tt, pz.shape[1])

    def post_m(part, pm):
        if part == 1:
            pm = pm * (ML_DH ** -0.5)
        elif part == 3:
            pm = jax.nn.sigmoid(pm)
        m4_ref[:, :, part * D_ML:(part + 1) * D_ML] = pm.reshape(bb, tt, D_ML)

    def post_gates(z):
        z = z + gp_ref[0:1, :]
        lane = lax.broadcasted_iota(jnp.int32, z.shape, 1)
        g = -jnp.exp(gp_ref[1:2, :]) * _softplus(z)
        gates = jnp.where(lane < B_LANE, g,
                          jnp.where(lane < I_LANE, jax.nn.sigmoid(z),
                                    jnp.where(lane < F_LANE, z,
                                              jnp.where(lane < F_LANE + ML_HEADS, -_softplus(-z), 0.0))))
        gates_ref[...] = gates.reshape(bb, tt, LANE)

    W = MXU_DIM
    qkv = [(w_ref, COL_G + c, COL_G + c + W, functools.partial(post_qkv, c)) for c in range(0, D_QKV, W)]
    gz = [(w_ref, COL_G + D_QKV + c, COL_G + D_QKV + c + W, functools.partial(post_gz, c))
          for c in range(0, D_GDN, W)]
    mm = [(wt_ref, i * D_ML, (i + 1) * D_ML, functools.partial(post_m, i)) for i in range(4)]
    light = [(w_ref, COL_A, COL_G, post_a)] + mm + gz + [(wt_ref, 4 * D_ML, 4 * D_ML + LANE, post_gates)]
    tasks = []
    for i in range(max(len(qkv), len(light))):
        tasks += qkv[i:i + 1] + light[i:i + 1]
    pending = None
    for ref, lo, hi, post in tasks:
        res = jnp.dot(xb, ref[:, lo:hi], preferred_element_type=f32)
        if pending is not None:
            pending[0](pending[1])
        pending = (post, res)
    pending[0](pending[1])


def _proj_call(l, x, p, states_in, prev_states, *, bb, tt):
    B, T, _ = x.shape
    tile = functools.partial(_tile_spec, bb, tt)
    widths = (D_CONV, D_QKV, D_GDN, 4 * D_ML, LANE)
    return _stacked_call(
        _proj_kernel, l=l, B=B, bb=bb, tt=tt, grid=(B // bb, T // tt),
        inputs=[x, p["w_in"], p["w_tail"], p["conv_w"], p["gdn_conv_w"], p["gate_p"]],
        in_specs=[tile(D_MODEL), _weight_spec(l, (D_MODEL, D_IN)), _weight_spec(l, (D_MODEL, D_TAIL)),
                  _weight_spec(l, (3, D_CONV)), _weight_spec(l, (4, D_QKV)), _weight_spec(l, (8, LANE))],
        states_in=states_in, prev_states=prev_states, state_tails=[(2, D_CONV), (3, D_QKV)],
        tile_shapes=[jax.ShapeDtypeStruct((B, T, w), f32) for w in widths],
        tile_specs=[tile(w) for w in widths],
        scratch_shapes=[pltpu.VMEM((bb, tt + HIST, D_CONV), f32),
                        pltpu.VMEM((bb, tt + HIST, D_QKV), f32)],
        name="in_proj")


def _bdot(a, b):
    return jnp.einsum('pik,pkj->pij', a.astype(bf16), b.astype(bf16), preferred_element_type=f32)


def _bdot_nt(a, b):
    return jnp.einsum('pik,pjk->pij', a.astype(bf16), b.astype(bf16), preferred_element_type=f32)


def _bdot_tn(a, b):
    return jnp.einsum('pki,pkj->pij', a.astype(bf16), b.astype(bf16), preferred_element_type=f32)


def _split3(x):
    hi = x.astype(bf16)
    r1 = x - hi.astype(f32)
    mid = r1.astype(bf16)
    lo = (r1 - mid.astype(f32)).astype(bf16)
    return hi, mid, lo


def _gate_tiles(gates_ref, *, bb, nc, L):
    gt = jnp.stack([gates_ref[bi, ci * L:(ci + 1) * L, :] for bi in range(bb) for ci in range(nc)])
    tril = lax.broadcasted_iota(jnp.int32, (L, L), 1) <= lax.broadcasted_iota(jnp.int32, (L, L), 0)
    trb = jnp.broadcast_to(jnp.concatenate([tril.astype(bf16)] * 3, axis=1), (bb * nc, L, 3 * L))
    cs = jnp.einsum('pij,pjk->pik', trb, jnp.concatenate(_split3(gt), axis=1), preferred_element_type=f32)
    return gt, cs


def _rows_from_cols(x, eye2):
    L = x.shape[1]
    onesb = jnp.ones((x.shape[0], L, 3 * L), bf16)
    return jnp.einsum('pij,pjk->pik', onesb, jnp.concatenate(_split3(x * eye2), axis=1),
                      preferred_element_type=f32)


def _gdn_kernel(*refs, l, first, bb, nc, L, has_state, n_prev):
    qkv_ref, gates_ref, gz_ref, nw_ref = refs[:4]
    refs = refs[4:]
    if has_state:
        s0_ref = refs[0]
        refs = refs[1:]
    yb_ref, s_ref = refs[n_prev:]
    (s_ref,) = _own_layer([s_ref], l, first)
    t = pl.program_id(1)

    @pl.when(t == 0)
    def _():
        s_ref[...] = s0_ref[...] if has_state else jnp.zeros(s_ref.shape, f32)

    NP = GDN_HEADS // 2
    W2 = 2 * L
    nbc, npair = bb * nc, bb * nc * NP
    hi2 = lax.broadcasted_iota(jnp.int32, (1, W2), 1) >= L
    row = lax.broadcasted_iota(jnp.int32, (L, W2), 0)
    col = lax.broadcasted_iota(jnp.int32, (L, W2), 1) % L
    causal2, strict2 = col <= row, col < row
    eye2 = (col == row).astype(f32)
    def head_tiles(col0, hl):
        return jnp.stack([qkv_ref[bi, ci * L:(ci + 1) * L,
                                  col0 + (2 * pp + hl) * GDN_D:col0 + (2 * pp + hl + 1) * GDN_D]
                          for bi in range(bb) for ci in range(nc) for pp in range(NP)])

    def pair_col(x, lane0, hl):
        return jnp.stack([x[:, :, lane0 + 2 * pp + hl:lane0 + 2 * pp + hl + 1] for pp in range(NP)],
                         axis=1).reshape(npair, L, 1)

    gt, cs = _gate_tiles(gates_ref, bb=bb, nc=nc, L=L)
    G = [pair_col(cs, G_LANE, hl) for hl in range(2)]
    beta = [pair_col(gt, B_LANE, hl) for hl in range(2)]
    g_col = jnp.where(hi2, G[1], G[0])
    g_row = _rows_from_cols(g_col, eye2)
    decay = jnp.where(causal2, jnp.exp(jnp.where(causal2, g_col - g_row, 0.0)), 0.0)
    q = [head_tiles(0, hl) for hl in range(2)]
    k = [head_tiles(D_GDN, hl) for hl in range(2)]
    v = [head_tiles(2 * D_GDN, hl) for hl in range(2)]
    kb = [k[hl] * beta[hl] for hl in range(2)]
    zd = jnp.zeros((npair, L, GDN_D), f32)
    lhs = jnp.concatenate([jnp.concatenate(kb, axis=-1), jnp.concatenate(q, axis=-1)], axis=1)
    k_bd = jnp.concatenate([jnp.concatenate([k[0], zd], axis=-1),
                            jnp.concatenate([zd, k[1]], axis=-1)], axis=1)
    kkqk = _bdot_nt(lhs, k_bd)
    kk = kkqk[:, 0:L] * decay
    qk = kkqk[:, L:] * decay
    def block_diag(x):
        return jnp.concatenate([jnp.where(hi2, 0.0, x), jnp.where(hi2, x, 0.0)], axis=1)

    N = jnp.where(strict2, kk, 0.0)
    R = None
    b = 1
    while b < L:
        lower_left = (row // (2 * b) == col // (2 * b)) & (row % (2 * b) >= b) & (col % (2 * b) < b)
        c = jnp.where(lower_left, N, 0.0)
        if R is None:
            R = -c
        else:
            y = c + _bdot(c, block_diag(R))
            R = R - y - _bdot(R, block_diag(y))
        b *= 2
    eG = [jnp.exp(G[hl]) for hl in range(2)]
    rhs = [jnp.concatenate([v[hl] * beta[hl], kb[hl] * eG[hl]], axis=-1) for hl in range(2)]
    z2 = jnp.zeros((npair, L, 2 * GDN_D), f32)
    w = [rhs[0] + _bdot(R, jnp.concatenate([rhs[0], z2], axis=1)),
         rhs[1] + _bdot(R, jnp.concatenate([z2, rhs[1]], axis=1))]
    qg = [q[hl] * eG[hl] for hl in range(2)]
    g_last = [G[hl][:, L - 1:L, :] for hl in range(2)]
    kdec = [k[hl] * jnp.exp(g_last[hl] - G[hl]) for hl in range(2)]
    e_last = [jnp.exp(g_last[hl]) for hl in range(2)]

    def chunk(xs, ci):
        return jnp.concatenate([x.reshape((bb, nc, NP) + x.shape[1:])[:, ci]
                                .reshape((bb * NP,) + x.shape[1:]) for x in xs], axis=0)

    order = [(bi, 2 * pp + hl) for hl in range(2) for bi in range(bb) for pp in range(NP)]
    nw = nw_ref[...]
    S = jnp.stack([s_ref[bi, h] for bi, h in order])
    zu = jnp.zeros((bb * NP, L, GDN_D), f32)
    for ci in range(nc):
        rs = slice(ci * L, (ci + 1) * L)
        w_c = chunk(w, ci)
        xs = _bdot(jnp.concatenate([w_c[:, :, GDN_D:], chunk(qg, ci)], axis=1), S)
        u = w_c[:, :, 0:GDN_D] - xs[:, 0:L]
        u_bd = jnp.concatenate([jnp.concatenate([u[0:bb * NP], zu], axis=1),
                                jnp.concatenate([zu, u[bb * NP:]], axis=1)], axis=0)
        o = xs[:, L:] + _bdot(chunk([qk, qk], ci), u_bd)
        S = S * chunk(e_last, ci) + _bdot_tn(chunk(kdec, ci), u)
        o = o * lax.rsqrt(jnp.mean(o * o, -1, keepdims=True) + NORM_EPS) * nw
        for i, (bi, h) in enumerate(order):
            hs = slice(h * GDN_D, (h + 1) * GDN_D)
            yb_ref[bi, rs, hs] = o[i] * gz_ref[bi, rs, hs]
    for i, (bi, h) in enumerate(order):
        s_ref[bi, h] = S[i]


def _gdn_call(l, qkv, gates, gz, p, states_in, prev_states, *, bb, nc, L):
    B, T, _ = qkv.shape
    tt = nc * L
    tile = functools.partial(_tile_spec, bb, tt)
    return _stacked_call(
        _gdn_kernel, l=l, B=B, bb=bb, nc=nc, L=L, grid=(B // bb, T // tt),
        inputs=[qkv, gates, gz, p["gdn_norm_w"]],
        in_specs=[tile(D_QKV), tile(LANE), tile(D_GDN), _weight_spec(l, (1, GDN_D))],
        states_in=states_in, prev_states=prev_states, state_tails=[(GDN_HEADS, GDN_D, GDN_D)],
        tile_shapes=[jax.ShapeDtypeStruct((B, T, D_GDN), f32)], tile_specs=[tile(D_GDN)],
        scratch_shapes=[], name="gdn")


def _mlstm_chunks(m4_ref, gates_ref, nw_ref, yc_ref, cn_ref, mp_ref, *, bb, nc, L):
    DH = ML_DH
    NP = ML_HEADS // 2
    nbc, npair = bb * nc, bb * nc * NP
    lane = lax.broadcasted_iota(jnp.int32, (1, LANE), 1)
    hi_half = lane >= DH
    row = lax.broadcasted_iota(jnp.int32, (L, LANE), 0)
    col = lax.broadcasted_iota(jnp.int32, (L, LANE), 1) % DH
    causal2 = col <= row
    eye2 = (col == row).astype(f32)
    r2 = lax.broadcasted_iota(jnp.int32, (LANE, LANE), 0) // DH
    c2 = lax.broadcasted_iota(jnp.int32, (LANE, LANE), 1) // DH
    blk = r2 == c2
    blk2 = jnp.concatenate([blk, blk], axis=1)
    ones_bd = blk.astype(bf16)

    def tiles(col0):
        return jnp.stack([m4_ref[bi, ci * L:(ci + 1) * L, col0 + pp * LANE:col0 + (pp + 1) * LANE]
                          for bi in range(bb) for ci in range(nc) for pp in range(NP)])

    def pair_cols(x, lane0):
        a = jnp.stack([x[:, :, lane0 + 2 * pp:lane0 + 2 * pp + 1] for pp in range(NP)], axis=1)
        b = jnp.stack([x[:, :, lane0 + 2 * pp + 1:lane0 + 2 * pp + 2] for pp in range(NP)], axis=1)
        return jnp.where(hi_half, b, a).reshape(npair, L, LANE)

    gt, cs = _gate_tiles(gates_ref, bb=bb, nc=nc, L=L)
    a_t = pltpu.roll(gt, F_LANE - I_LANE, axis=2) - cs
    A_t = a_t
    rows3 = lax.broadcasted_iota(jnp.int32, a_t.shape, 1)
    sh = 1
    while sh < L:
        A_t = jnp.maximum(A_t, jnp.where(rows3 >= sh, pltpu.roll(A_t, sh, axis=1), -jnp.inf))
        sh *= 2
    F = pair_cols(cs, F_LANE)
    a_p = pair_cols(a_t, F_LANE)
    A_p = pair_cols(A_t, F_LANE)
    Dm = jnp.where(causal2, F + _rows_from_cols(a_p, eye2), -jnp.inf)
    Q = tiles(0)
    K = tiles(D_ML)
    V = tiles(2 * D_ML)
    OG = tiles(3 * D_ML)
    if L < DH:
        zpad = jnp.zeros((npair, DH - L, LANE), f32)
        Kp, Vp = jnp.concatenate([K, zpad], axis=1), jnp.concatenate([V, zpad], axis=1)
    else:
        Kp, Vp = K, V
    k_bd = jnp.concatenate([jnp.where(hi_half, 0.0, Kp), jnp.where(hi_half, Kp, 0.0)], axis=1).astype(bf16)
    v_bd = jnp.concatenate([jnp.where(hi_half, 0.0, Vp), jnp.where(hi_half, Vp, 0.0)], axis=1).astype(bf16)
    vo = jnp.concatenate([v_bd, jnp.broadcast_to(ones_bd, (npair, LANE, LANE))], axis=-1)
    s_raw = jnp.einsum('pik,pjk->pij', Q.astype(bf16), k_bd, preferred_element_type=f32)
    v_one = jnp.concatenate([V, jnp.ones((npair, L, LANE), f32)], axis=-1).astype(bf16)

    def chunk(x, ci):
        return x.reshape((bb, nc, NP) + x.shape[1:])[:, ci].reshape((bb * NP,) + x.shape[1:])

    nw = jnp.stack([nw_ref[0:1, pp * LANE:(pp + 1) * LANE] for pp in range(NP)])
    nw = jnp.broadcast_to(nw[None], (bb, NP, 1, LANE)).reshape(bb * NP, 1, LANE)
    mean_w = (blk.astype(f32) * (1.0 / DH)).astype(bf16)

    def head_mean(x):
        hi = x.astype(bf16)
        lo = (x - hi.astype(f32)).astype(bf16)
        mw = jnp.broadcast_to(jnp.concatenate([mean_w, mean_w], axis=0), (x.shape[0], 2 * LANE, LANE))
        return jnp.einsum('pik,pkj->pij', jnp.concatenate([hi, lo], axis=-1), mw, preferred_element_type=f32)

    CN = cn_ref[...]
    m = mp_ref[:, 0:1, :]
    for ci in range(nc):
        F_c, A_c, a_c = chunk(F, ci), chunk(A_p, ci), chunk(a_p, ci)
        M = F_c + jnp.maximum(m, A_c)
        wD = jnp.exp(chunk(Dm, ci) - M)
        wI = jnp.exp(F_c + m - M)
        sc = chunk(s_raw, ci) * wD
        numq = (jnp.concatenate([wI, wI], axis=-1) * _bdot(chunk(Q, ci), CN)
                + jnp.einsum('pik,pkj->pij', sc.astype(bf16), chunk(vo, ci), preferred_element_type=f32))
        hh = numq[:, :, 0:LANE] / jnp.maximum(jnp.abs(numq[:, :, LANE:]), jnp.exp(-M))
        m_new = M[:, L - 1:L, :]
        f_last = F_c[:, L - 1:L, :]
        wk = jnp.exp(f_last - F_c + (a_c + F_c) - m_new)
        dc = jnp.exp(f_last + m - m_new)
        kw = chunk(K, ci) * wk
        upd = jnp.einsum('pki,pkj->pij', kw.astype(bf16), chunk(v_one, ci), preferred_element_type=f32)
        CN = jnp.concatenate([dc, dc], axis=-1) * CN + jnp.where(blk2, upd, 0.0)
        m = m_new
        hh = chunk(OG, ci) * hh
        hc = hh - head_mean(hh)
        var = head_mean(hc * hc)
        y = (hc * lax.rsqrt(var + LN_EPS) * nw).reshape(bb, NP, L, LANE)
        for bi in range(bb):
            for pp in range(NP):
                yc_ref[bi, ci * L:(ci + 1) * L, pp * LANE:(pp + 1) * LANE] = y[bi, pp]
    cn_ref[...] = CN
    mp_ref[:, 0:1, :] = m


def _mlstm_state_in(c0_ref, n0_ref, m0_ref, cn_ref, mp_ref, *, bb):
    DH, NP = ML_DH, ML_HEADS // 2
    if c0_ref is None:
        cn_ref[...] = jnp.zeros(cn_ref.shape, f32)
        mp_ref[...] = jnp.zeros(mp_ref.shape, f32)
        return
    hi_half = lax.broadcasted_iota(jnp.int32, (1, LANE), 1) >= DH
    blk = (lax.broadcasted_iota(jnp.int32, (LANE, LANE), 0) // DH
           == lax.broadcasted_iota(jnp.int32, (LANE, LANE), 1) // DH)
    for bi in range(bb):
        for pp in range(NP):
            c = c0_ref[bi, pp]
            cn_ref[bi * NP + pp, :, 0:LANE] = jnp.concatenate(
                [jnp.where(hi_half, 0.0, c), jnp.where(hi_half, c, 0.0)], axis=0)
            n_col = jnp.broadcast_to(n0_ref[bi, pp:pp + 1, :], (LANE, LANE)).T
            cn_ref[bi * NP + pp, :, LANE:] = jnp.where(blk, n_col, 0.0)
            m_a = m0_ref[bi, 0:1, 2 * pp:2 * pp + 1]
            m_b = m0_ref[bi, 0:1, 2 * pp + 1:2 * pp + 2]
            mp_ref[bi * NP + pp, 0:1, :] = jnp.where(hi_half, m_b, m_a)


def _mlstm_state_out(cn_ref, mp_ref, c_ref, n_ref, m_ref, *, bb):
    DH, NP = ML_DH, ML_HEADS // 2
    for bi in range(bb):
        for pp in range(NP):
            for hl in range(2):
                h = 2 * pp + hl
                rs = slice(hl * DH, (hl + 1) * DH)
                c_ref[bi, h] = cn_ref[bi * NP + pp, rs, hl * DH:(hl + 1) * DH]
                n_blk = cn_ref[bi * NP + pp, rs, LANE + hl * DH:LANE + (hl + 1) * DH]
                n_ref[bi, h:h + 1, :] = n_blk.T[0:1, :]
                m_ref[bi, 0:1, h:h + 1] = mp_ref[bi * NP + pp, 0:1, hl * DH:hl * DH + 1]


def _mlstm_kernel(*refs, l, first, bb, nc, L, has_state, n_prev):
    m4_ref, gates_ref, nw_ref = refs[:3]
    refs = refs[3:]
    c0_ref = n0_ref = m0_ref = None
    if has_state:
        c0_ref, n0_ref, m0_ref = refs[:3]
        refs = refs[3:]
    yc_ref, c_ref, n_ref, m_ref, cn_ref, mp_ref = refs[n_prev:]
    c_ref, n_ref, m_ref = _own_layer([c_ref, n_ref, m_ref], l, first)
    t = pl.program_id(1)

    @pl.when(t == 0)
    def _():
        _mlstm_state_in(c0_ref, n0_ref, m0_ref, cn_ref, mp_ref, bb=bb)

    _mlstm_chunks(m4_ref, gates_ref, nw_ref, yc_ref, cn_ref, mp_ref, bb=bb, nc=nc, L=L)

    @pl.when(t == pl.num_programs(1) - 1)
    def _():
        _mlstm_state_out(cn_ref, mp_ref, c_ref, n_ref, m_ref, bb=bb)


def _mlstm_call(l, m4, gates, p, states_in, prev_states, *, bb, nc, L):
    B, T, _ = m4.shape
    tt = nc * L
    tile = functools.partial(_tile_spec, bb, tt)
    npair = bb * (ML_HEADS // 2)
    return _stacked_call(
        _mlstm_kernel, l=l, B=B, bb=bb, nc=nc, L=L, grid=(B // bb, T // tt),
        inputs=[m4, gates, p["ml_norm_w"]],
        in_specs=[tile(4 * D_ML), tile(LANE), _weight_spec(l, (1, D_ML))],
        states_in=states_in, prev_states=prev_states,
        state_tails=[(ML_HEADS, ML_DH, ML_DH), (ML_HEADS, ML_DH), (1, ML_HEADS)],
        state_in_tails=[(ML_HEADS // 2, ML_DH, LANE), (ML_HEADS // 2, LANE), (1, ML_HEADS)],
        tile_shapes=[jax.ShapeDtypeStruct((B, T, D_ML), f32)], tile_specs=[tile(D_ML)],
        scratch_shapes=[pltpu.VMEM((npair, LANE, 2 * LANE), f32), pltpu.VMEM((npair, 8, LANE), f32)],
        name="mlstm")


def _out_ffn_kernel(*refs, l, first, bb, tt, has_state, n_prev):
    (x_ref, ya_ref, yb_ref, yc_ref, wo_ref, ln1_ref, wup_ref, fcw_ref, wdn_ref, ln2_ref) = refs[:10]
    refs = refs[10:]
    if has_state:
        hf_ref = refs[0]
        refs = refs[1:]
    xo_ref, tf_ref, sf_ref = refs[n_prev:]
    (tf_ref,) = _own_layer([tf_ref], l, first)
    t = pl.program_id(1)

    @pl.when(t == 0)
    def _():
        if has_state:
            sf_ref[:, HIST - 2:HIST, :] = hf_ref[...]
        else:
            sf_ref[:, 0:HIST, :] = jnp.zeros((bb, HIST, D_FF), f32)

    def stages(r0, th):
        rows = bb * th
        rs = slice(r0, r0 + th)
        x = x_ref[:, rs, :].reshape(rows, D_MODEL)
        mix = (jnp.dot(ya_ref[:, rs, :].reshape(rows, D_CONV).astype(bf16), wo_ref[0:D_CONV, :],
                       preferred_element_type=f32)
               + jnp.dot(yb_ref[:, rs, :].reshape(rows, D_GDN).astype(bf16),
                         wo_ref[D_CONV:D_CONV + D_GDN, :], preferred_element_type=f32)
               + jnp.dot(yc_ref[:, rs, :].reshape(rows, D_ML).astype(bf16), wo_ref[D_CONV + D_GDN:, :],
                         preferred_element_type=f32))
        yield
        x1 = _layer_norm(ALPHA * x + mix, ln1_ref[0:1, :], ln1_ref[1:2, :])
        x1b = x1.astype(bf16)
        yield
        acc = None
        for lo, hi in FF_CHUNKS:
            gate = jnp.dot(x1b, wup_ref[:, lo:hi], preferred_element_type=f32)
            val = jnp.dot(x1b, wup_ref[:, D_FF + lo:D_FF + hi], preferred_element_type=f32)
            yield
            sf_ref[:, HIST + r0:HIST + r0 + th, lo:hi] = gate.reshape(bb, th, hi - lo)
            conv = _causal_dwconv(sf_ref[:, r0:r0 + HIST + th, lo:hi], fcw_ref[:, lo:hi])
            hmid = _silu(conv).reshape(rows, hi - lo) * val
            yield
            part = jnp.dot(hmid.astype(bf16), wdn_ref[lo:hi, :], preferred_element_type=f32)
            acc = part if acc is None else acc + part
            yield
        x2 = _layer_norm(ALPHA * x1 + acc, ln2_ref[0:1, :], ln2_ref[1:2, :])
        xo_ref[:, rs, :] = x2.reshape(bb, th, D_MODEL)

    nh = 2 if tt % (2 * MXU_DIM) == 0 else 1
    for _ in itertools.zip_longest(*[stages(i * (tt // nh), tt // nh) for i in range(nh)]):
        pass
    tf_ref[...] = sf_ref[:, tt + HIST - 2:tt + HIST, :]
    sf_ref[:, 0:HIST, :] = sf_ref[:, tt:tt + HIST, :]


def _out_ffn_call(l, x, ya, yb, yc, p, states_in, prev_states, *, bb, tt):
    B, T, _ = x.shape
    tile = functools.partial(_tile_spec, bb, tt)
    return _stacked_call(
        _out_ffn_kernel, l=l, B=B, bb=bb, tt=tt, grid=(B // bb, T // tt),
        inputs=[x, ya, yb, yc, p["w_o"], p["ln1"], p["w_up"], p["ffn_conv_w"], p["w_down"], p["ln2"]],
        in_specs=[tile(D_MODEL), tile(D_CONV), tile(D_GDN), tile(D_ML),
                  _weight_spec(l, (D_MODEL, D_MODEL)), _weight_spec(l, (2, D_MODEL)),
                  _weight_spec(l, (D_MODEL, 2 * D_FF)), _weight_spec(l, (3, D_FF)),
                  _weight_spec(l, (D_FF, D_MODEL)), _weight_spec(l, (2, D_MODEL))],
        states_in=states_in, prev_states=prev_states, state_tails=[(2, D_FF)],
        tile_shapes=[jax.ShapeDtypeStruct((B, T, D_MODEL), f32)], tile_specs=[tile(D_MODEL)],
        scratch_shapes=[pltpu.VMEM((bb, tt + HIST, D_FF), f32)], name="out_ffn")


def _prep_params(w_in, conv_w, gdn_conv_w, gdn_a_log, gdn_dt_bias, gdn_norm_w,
                 ml_i_bias, ml_f_bias, ml_norm_w, w_o, ln1_g, ln1_b,
                 w_up, ffn_conv_w, w_down, ln2_g, ln2_b):
    gate_cols = jnp.concatenate([w_in[:, :, COL_GA:COL_MQ], w_in[:, :, COL_MI:D_IN]], axis=2)
    gate_cols = jnp.pad(gate_cols, ((0, 0), (0, 0), (0, LANE - gate_cols.shape[2])))
    w_tail = jnp.concatenate([w_in[:, :, COL_MQ:COL_MI], gate_cols], axis=2).astype(bf16)
    zeros4 = jnp.zeros((DEPTH, 4), f32)
    bias_row = jnp.pad(jnp.concatenate([gdn_dt_bias, zeros4, ml_i_bias, ml_f_bias], axis=1),
                       ((0, 0), (0, LANE - 16)))
    alog_row = jnp.pad(gdn_a_log, ((0, 0), (0, LANE - GDN_HEADS)))
    gate_p = jnp.zeros((DEPTH, 8, LANE), f32).at[:, 0].set(bias_row).at[:, 1].set(alog_row)
    return dict(
        w_in=w_in.astype(bf16), w_tail=w_tail, conv_w=conv_w, gdn_conv_w=gdn_conv_w, gate_p=gate_p,
        gdn_norm_w=gdn_norm_w.reshape(DEPTH, 1, GDN_D), ml_norm_w=ml_norm_w.reshape(DEPTH, 1, D_ML),
        w_o=w_o.astype(bf16), ln1=jnp.stack([ln1_g, ln1_b], axis=1),
        w_up=w_up.astype(bf16), ffn_conv_w=ffn_conv_w, w_down=w_down.astype(bf16),
        ln2=jnp.stack([ln2_g, ln2_b], axis=1),
    )


def _tiling(B, T):
    L = min(CHUNK, T)
    if T >= 512:
        return dict(bb=1, tt=256, fbb=1, ftt=512, L=L, rbb=min(B, 8), gnc=2, mnc=2)
    return dict(bb=256 // T, tt=T, fbb=256 // T, ftt=T, L=L, rbb=min(B, 16), gnc=T // L, mnc=T // L)


def _trunk(x, states, p):
    B, T, _ = x.shape
    c = _tiling(B, T)
    conv_new = gdn_new = ml_new = ffn_new = None
    for l in range(DEPTH):
        sel = lambda idx: None if states is None else [states[i] for i in idx]
        ya, qkv, gz, m4, gates, *conv_new = _proj_call(l, x, p, sel((0, 1)), conv_new,
                                                       bb=c["bb"], tt=c["tt"])
        yb, *gdn_new = _gdn_call(l, qkv, gates, gz, p, sel((2,)), gdn_new,
                                 bb=c["rbb"], nc=c["gnc"], L=c["L"])
        yc, *ml_new = _mlstm_call(l, m4, gates, p, sel((3, 4, 5)), ml_new,
                                  bb=c["rbb"], nc=c["mnc"], L=c["L"])
        x, *ffn_new = _out_ffn_call(l, x, ya, yb, yc, p, sel((6,)), ffn_new,
                                    bb=c["fbb"], tt=c["ftt"])
    return x, (conv_new[0], conv_new[1], gdn_new[0], ml_new[0], ml_new[1],
               ml_new[2].reshape(DEPTH, B, ML_HEADS), ffn_new[0])


def _pair_heads(c):
    d, b = c.shape[:2]
    return (c.reshape(d, b, ML_HEADS // 2, 2, ML_DH, ML_DH).transpose(0, 1, 2, 4, 3, 5)
            .reshape(d, b, ML_HEADS // 2, ML_DH, 2 * ML_DH))


def kernel(x_prompt, x_sample, state_conv_mix, state_gdn_conv, state_gdn, state_mlstm_c,
           state_mlstm_n, state_mlstm_m, state_ffn_conv,
           w_in, conv_w, gdn_conv_w, gdn_a_log, gdn_dt_bias, gdn_norm_w,
           ml_i_bias, ml_f_bias, ml_norm_w, w_o, ln1_g, ln1_b,
           w_up, ffn_conv_w, w_down, ln2_g, ln2_b):
    p = _prep_params(w_in, conv_w, gdn_conv_w, gdn_a_log, gdn_dt_bias, gdn_norm_w,
                     ml_i_bias, ml_f_bias, ml_norm_w, w_o, ln1_g, ln1_b,
                     w_up, ffn_conv_w, w_down, ln2_g, ln2_b)
    Bs = x_sample.shape[0]
    y_prompt, p_new = _trunk(x_prompt, None, p)
    s_states = (state_conv_mix, state_gdn_conv, state_gdn, _pair_heads(state_mlstm_c),
                state_mlstm_n.reshape(DEPTH, Bs, ML_HEADS // 2, 2 * ML_DH),
                state_mlstm_m.reshape(DEPTH, Bs, 1, ML_HEADS), state_ffn_conv)
    y_sample, s_new = _trunk(x_sample, s_states, p)
    return (y_prompt, y_sample, *p_new, *s_new)
```

```python
import functools
import itertools

import jax
import jax.numpy as jnp
from jax import lax
from jax.experimental import pallas as pl
from jax.experimental.pallas import tpu as pltpu

f32 = jnp.float32
bf16 = jnp.bfloat16

D_MODEL = 1024
DEPTH = 2
D_CONV = 256
GDN_HEADS = 4
GDN_D = 128
D_GDN = GDN_HEADS * GDN_D
D_QKV = 3 * D_GDN
ML_HEADS = 4
ML_DH = 64
D_ML = ML_HEADS * ML_DH
D_FF = 2816
CHUNK = 64
ALPHA = (2.0 * DEPTH) ** 0.25
LN_EPS = 1e-5
NORM_EPS = 1e-6

LANE = 128
HIST = 8
COL_A = 0
COL_G = 3 * D_CONV
COL_GA = COL_G + 4 * D_GDN
COL_MQ = COL_GA + 2 * GDN_HEADS
COL_MI = COL_MQ + 4 * D_ML
D_IN = COL_MI + 2 * ML_HEADS
D_TAIL = 4 * D_ML + LANE
G_LANE, B_LANE, I_LANE, F_LANE = 0, 4, 8, 12
MXU_DIM = 256
FF_CHUNKS = ((0, 6 * MXU_DIM), (6 * MXU_DIM, D_FF))
VMEM_LIMIT = 56 * 1024 * 1024
_PARAMS = pltpu.CompilerParams(dimension_semantics=("arbitrary", "arbitrary"),
                               vmem_limit_bytes=VMEM_LIMIT)


def _softplus(z):
    return jnp.maximum(z, 0.0) + jnp.log1p(jnp.exp(-jnp.abs(z)))


def _silu(x):
    h = 0.5 * x
    return h + h * jnp.tanh(h)


def _causal_dwconv(xh, w):
    width = w.shape[0]
    y = w[width - 1:width, :] * xh[:, HIST:, :]
    for j in range(width - 1):
        y = y + w[j:j + 1, :] * pltpu.roll(xh, width - 1 - j, axis=1)[:, HIST:, :]
    return y


def _layer_norm(x, g, b):
    mu = jnp.mean(x, -1, keepdims=True)
    xc = x - mu
    var = jnp.mean(xc * xc, -1, keepdims=True)
    return xc * lax.rsqrt(var + LN_EPS) * g + b


def _tile_spec(bb, tt, c):
    return pl.BlockSpec((bb, tt, c), lambda b, t: (b, t, 0))


def _state_spec(l, bb, tail, every_layer=False):
    if every_layer:
        return pl.BlockSpec((DEPTH, bb) + tail, lambda b, t: (0, b) + (0,) * len(tail))
    return pl.BlockSpec((None, bb) + tail, lambda b, t: (l, b) + (0,) * len(tail))


def _own_layer(refs, l, first):
    if not first:
        return refs

    @pl.when(pl.program_id(1) == 0)
    def _():
        for r in refs:
            for j in range(DEPTH):
                if j != l:
                    r[j] = jnp.zeros(r.shape[1:], f32)

    return [r.at[l] for r in refs]


def _weight_spec(l, tail):
    return pl.BlockSpec((None,) + tail, lambda b, t: (l,) + (0,) * len(tail),
                        pipeline_mode=pl.Buffered(1))


def _stacked_call(kernel, *, l, B, bb, grid, inputs, in_specs, states_in, prev_states, state_tails,
                  tile_shapes, tile_specs, scratch_shapes, name, state_in_tails=None, **static):
    first = prev_states is None
    args = list(inputs)
    specs = list(in_specs)
    if states_in is not None:
        args += list(states_in)
        specs += [_state_spec(l, bb, tail) for tail in (state_in_tails or state_tails)]
    aliases = {}
    if not first:
        for i, p in enumerate(prev_states):
            aliases[len(args)] = len(tile_shapes) + i
            args.append(p)
            specs.append(pl.BlockSpec(memory_space=pl.ANY))
    return pl.pallas_call(
        functools.partial(kernel, l=l, first=first, has_state=states_in is not None,
                          n_prev=0 if first else len(prev_states), bb=bb, **static),
        grid=grid, in_specs=specs,
        out_specs=tuple(tile_specs) + tuple(_state_spec(l, bb, tail, every_layer=first)
                                            for tail in state_tails),
        out_shape=tuple(tile_shapes) + tuple(jax.ShapeDtypeStruct((DEPTH, B) + tail, f32)
                                             for tail in state_tails),
        scratch_shapes=scratch_shapes, input_output_aliases=aliases,
        compiler_params=_PARAMS, name=name)(*args)


def _proj_kernel(*refs, l, first, bb, tt, has_state, n_prev):
    x_ref, w_ref, wt_ref, cw_ref, gcw_ref, gp_ref = refs[:6]
    refs = refs[6:]
    if has_state:
        ha_ref, hg_ref = refs[:2]
        refs = refs[2:]
    ya_ref, qkv_ref, gz_ref, m4_ref, gates_ref, ta_ref, tg_ref, sa_ref, sg_ref = refs[n_prev:]
    ta_ref, tg_ref = _own_layer([ta_ref, tg_ref], l, first)
    t = pl.program_id(1)

    @pl.when(t == 0)
    def _():
        if has_state:
            sa_ref[:, HIST - 2:HIST, :] = ha_ref[...]
            sg_ref[:, HIST - 3:HIST, :] = hg_ref[...]
        else:
            sa_ref[:, 0:HIST, :] = jnp.zeros((bb, HIST, D_CONV), f32)
            sg_ref[:, 0:HIST, :] = jnp.zeros((bb, HIST, D_QKV), f32)

    rows = bb * tt
    xb = x_ref[...].reshape(rows, D_MODEL).astype(bf16)

    def post_a(pa):
        a_b = pa[:, 0:D_CONV]
        sa_ref[:, HIST:, :] = (pa[:, D_CONV:2 * D_CONV] * pa[:, 2 * D_CONV:3 * D_CONV]).reshape(bb, tt, D_CONV)
        conv = _causal_dwconv(sa_ref[...], cw_ref[...])
        ya_ref[...] = a_b.reshape(bb, tt, D_CONV) * conv
        ta_ref[...] = sa_ref[:, tt + HIST - 2:tt + HIST, :]
        sa_ref[:, 0:HIST, :] = sa_ref[:, tt:tt + HIST, :]

    def post_qkv(c0, pg):
        w = pg.shape[1]
        sg_ref[:, HIST:, c0:c0 + w] = pg.reshape(bb, tt, w)
        for cb in range(c0 // LANE, (c0 + w) // LANE):
            cs = slice(cb * LANE, (cb + 1) * LANE)
            c = _silu(_causal_dwconv(sg_ref[:, :, cs], gcw_ref[:, cs]))
            if cb < 2 * GDN_HEADS:
                c = c * lax.rsqrt(jnp.sum(c * c, -1, keepdims=True) + NORM_EPS)
                if cb < GDN_HEADS:
                    c = c * (GDN_D ** -0.5)
            qkv_ref[:, :, cs] = c
        tg_ref[:, :, c0:c0 + w] = sg_ref[:, tt + HIST - 3:tt + HIST, c0:c0 + w]
        sg_ref[:, 0:HIST, c0:c0 + w] = sg_ref[:, tt:tt + HIST, c0:c0 + w]

    def post_gz(c0, pz):
        gz_ref[:, :, c0:c0 + pz.shape[1]] = _silu(pz).reshape(bb, tt, pz.shape[1])

    def post_m(part, pm):
        if part == 1:
            pm = pm * (ML_DH ** -0.5)
        elif part == 3:
            pm = jax.nn.sigmoid(pm)
        m4_ref[:, :, part * D_ML:(part + 1) * D_ML] = pm.reshape(bb, tt, D_ML)

    def post_gates(z):
        z = z + gp_ref[0:1, :]
        lane = lax.broadcasted_iota(jnp.int32, z.shape, 1)
        g = -jnp.exp(gp_ref[1:2, :]) * _softplus(z)
        gates = jnp.where(lane < B_LANE, g,
                          jnp.where(lane < I_LANE, jax.nn.sigmoid(z),
                                    jnp.where(lane < F_LANE, z,
                                              jnp.where(lane < F_LANE + ML_HEADS, -_softplus(-z), 0.0))))
        gates_ref[...] = gates.reshape(bb, tt, LANE)

    W = MXU_DIM
    qkv = [(w_ref, COL_G + c, COL_G + c + W, functools.partial(post_qkv, c)) for c in range(0, D_QKV, W)]
    gz = [(w_ref, COL_G + D_QKV + c, COL_G + D_QKV + c + W, functools.partial(post_gz, c))
          for c in range(0, D_GDN, W)]
    mm = [(wt_ref, i * D_ML, (i + 1) * D_ML, functools.partial(post_m, i)) for i in range(4)]
    light = [(w_ref, COL_A, COL_G, post_a)] + mm + gz + [(wt_ref, 4 * D_ML, 4 * D_ML + LANE, post_gates)]
    tasks = []
    for i in range(max(len(qkv), len(light))):
        tasks += qkv[i:i + 1] + light[i:i + 1]
    pending = None
    for ref, lo, hi, post in tasks:
        res = jnp.dot(xb, ref[:, lo:hi], preferred_element_type=f32)
        if pending is not None:
            pending[0](pending[1])
        pending = (post, res)
    pending[0](pending[1])


def _proj_call(l, x, p, states_in, prev_states, *, bb, tt):
    B, T, _ = x.shape
    tile = functools.partial(_tile_spec, bb, tt)
    widths = (D_CONV, D_QKV, D_GDN, 4 * D_ML, LANE)
    return _stacked_call(
        _proj_kernel, l=l, B=B, bb=bb, tt=tt, grid=(B // bb, T // tt),
        inputs=[x, p["w_in"], p["w_tail"], p["conv_w"], p["gdn_conv_w"], p["gate_p"]],
        in_specs=[tile(D_MODEL), _weight_spec(l, (D_MODEL, D_IN)), _weight_spec(l, (D_MODEL, D_TAIL)),
                  _weight_spec(l, (3, D_CONV)), _weight_spec(l, (4, D_QKV)), _weight_spec(l, (8, LANE))],
        states_in=states_in, prev_states=prev_states, state_tails=[(2, D_CONV), (3, D_QKV)],
        tile_shapes=[jax.ShapeDtypeStruct((B, T, w), f32) for w in widths],
        tile_specs=[tile(w) for w in widths],
        scratch_shapes=[pltpu.VMEM((bb, tt + HIST, D_CONV), f32),
                        pltpu.VMEM((bb, tt + HIST, D_QKV), f32)],
        name="in_proj")


def _bdot(a, b):
    return jnp.einsum('pik,pkj->pij', a.astype(bf16), b.astype(bf16), preferred_element_type=f32)


def _bdot_nt(a, b):
    return jnp.einsum('pik,pjk->pij', a.astype(bf16), b.astype(bf16), preferred_element_type=f32)


def _bdot_tn(a, b):
    return jnp.einsum('pki,pkj->pij', a.astype(bf16), b.astype(bf16), preferred_element_type=f32)


def _split3(x):
    hi = x.astype(bf16)
    r1 = x - hi.astype(f32)
    mid = r1.astype(bf16)
    lo = (r1 - mid.astype(f32)).astype(bf16)
    return hi, mid, lo


def _gate_tiles(gates_ref, *, bb, nc, L):
    gt = jnp.stack([gates_ref[bi, ci * L:(ci + 1) * L, :] for bi in range(bb) for ci in range(nc)])
    tril = lax.broadcasted_iota(jnp.int32, (L, L), 1) <= lax.broadcasted_iota(jnp.int32, (L, L), 0)
    trb = jnp.broadcast_to(jnp.concatenate([tril.astype(bf16)] * 3, axis=1), (bb * nc, L, 3 * L))
    cs = jnp.einsum('pij,pjk->pik', trb, jnp.concatenate(_split3(gt), axis=1), preferred_element_type=f32)
    return gt, cs


def _rows_from_cols(x, eye2):
    L = x.shape[1]
    onesb = jnp.ones((x.shape[0], L, 3 * L), bf16)
    return jnp.einsum('pij,pjk->pik', onesb, jnp.concatenate(_split3(x * eye2), axis=1),
                      preferred_element_type=f32)


def _gdn_kernel(*refs, l, first, bb, nc, L, has_state, n_prev):
    qkv_ref, gates_ref, gz_ref, nw_ref = refs[:4]
    refs = refs[4:]
    if has_state:
        s0_ref = refs[0]
        refs = refs[1:]
    yb_ref, s_ref = refs[n_prev:]
    (s_ref,) = _own_layer([s_ref], l, first)
    t = pl.program_id(1)

    @pl.when(t == 0)
    def _():
        s_ref[...] = s0_ref[...] if has_state else jnp.zeros(s_ref.shape, f32)

    _gdn_chunks(qkv_ref, gates_ref, gz_ref, nw_ref, yb_ref, s_ref, bb=bb, nc=nc, L=L)


def _gdn_chunks(qkv_ref, gates_ref, gz_ref, nw_ref, yb_ref, s_ref, *, bb, nc, L):
    NP = GDN_HEADS // 2
    W2 = 2 * L
    nbc, npair = bb * nc, bb * nc * NP
    hi2 = lax.broadcasted_iota(jnp.int32, (1, W2), 1) >= L
    row = lax.broadcasted_iota(jnp.int32, (L, W2), 0)
    col = lax.broadcasted_iota(jnp.int32, (L, W2), 1) % L
    causal2, strict2 = col <= row, col < row
    eye2 = (col == row).astype(f32)
    def head_tiles(col0, hl):
        return jnp.stack([qkv_ref[bi, ci * L:(ci + 1) * L,
                                  col0 + (2 * pp + hl) * GDN_D:col0 + (2 * pp + hl + 1) * GDN_D]
                          for bi in range(bb) for ci in range(nc) for pp in range(NP)])

    def pair_col(x, lane0, hl):
        return jnp.stack([x[:, :, lane0 + 2 * pp + hl:lane0 + 2 * pp + hl + 1] for pp in range(NP)],
                         axis=1).reshape(npair, L, 1)

    gt, cs = _gate_tiles(gates_ref, bb=bb, nc=nc, L=L)
    G = [pair_col(cs, G_LANE, hl) for hl in range(2)]
    beta = [pair_col(gt, B_LANE, hl) for hl in range(2)]
    g_col = jnp.where(hi2, G[1], G[0])
    g_row = _rows_from_cols(g_col, eye2)
    decay = jnp.where(causal2, jnp.exp(jnp.where(causal2, g_col - g_row, 0.0)), 0.0)
    q = [head_tiles(0, hl) for hl in range(2)]
    k = [head_tiles(D_GDN, hl) for hl in range(2)]
    v = [head_tiles(2 * D_GDN, hl) for hl in range(2)]
    kb = [k[hl] * beta[hl] for hl in range(2)]
    zd = jnp.zeros((npair, L, GDN_D), f32)
    lhs = jnp.concatenate([jnp.concatenate(kb, axis=-1), jnp.concatenate(q, axis=-1)], axis=1)
    k_bd = jnp.concatenate([jnp.concatenate([k[0], zd], axis=-1),
                            jnp.concatenate([zd, k[1]], axis=-1)], axis=1)
    kkqk = _bdot_nt(lhs, k_bd)
    kk = kkqk[:, 0:L] * decay
    qk = kkqk[:, L:] * decay
    def block_diag(x):
        return jnp.concatenate([jnp.where(hi2, 0.0, x), jnp.where(hi2, x, 0.0)], axis=1)

    N = jnp.where(strict2, kk, 0.0)
    R = None
    b = 1
    while b < L:
        lower_left = (row // (2 * b) == col // (2 * b)) & (row % (2 * b) >= b) & (col % (2 * b) < b)
        c = jnp.where(lower_left, N, 0.0)
        if R is None:
            R = -c
        else:
            y = c + _bdot(c, block_diag(R))
            R = R - y - _bdot(R, block_diag(y))
        b *= 2
    eG = [jnp.exp(G[hl]) for hl in range(2)]
    rhs = [jnp.concatenate([v[hl] * beta[hl], kb[hl] * eG[hl]], axis=-1) for hl in range(2)]
    z2 = jnp.zeros((npair, L, 2 * GDN_D), f32)
    w = [rhs[0] + _bdot(R, jnp.concatenate([rhs[0], z2], axis=1)),
         rhs[1] + _bdot(R, jnp.concatenate([z2, rhs[1]], axis=1))]
    qg = [q[hl] * eG[hl] for hl in range(2)]
    g_last = [G[hl][:, L - 1:L, :] for hl in range(2)]
    kdec = [k[hl] * jnp.exp(g_last[hl] - G[hl]) for hl in range(2)]
    e_last = [jnp.exp(g_last[hl]) for hl in range(2)]

    def chunk(xs, ci):
        return jnp.concatenate([x.reshape((bb, nc, NP) + x.shape[1:])[:, ci]
                                .reshape((bb * NP,) + x.shape[1:]) for x in xs], axis=0)

    order = [(bi, 2 * pp + hl) for hl in range(2) for bi in range(bb) for pp in range(NP)]
    nw = nw_ref[...]
    S = jnp.stack([s_ref[bi, h] for bi, h in order])
    zu = jnp.zeros((bb * NP, L, GDN_D), f32)
    for ci in range(nc):
        rs = slice(ci * L, (ci + 1) * L)
        w_c = chunk(w, ci)
        xs = _bdot(jnp.concatenate([w_c[:, :, GDN_D:], chunk(qg, ci)], axis=1), S)
        u = w_c[:, :, 0:GDN_D] - xs[:, 0:L]
        u_bd = jnp.concatenate([jnp.concatenate([u[0:bb * NP], zu], axis=1),
                                jnp.concatenate([zu, u[bb * NP:]], axis=1)], axis=0)
        o = xs[:, L:] + _bdot(chunk([qk, qk], ci), u_bd)
        S = S * chunk(e_last, ci) + _bdot_tn(chunk(kdec, ci), u)
        o = o * lax.rsqrt(jnp.mean(o * o, -1, keepdims=True) + NORM_EPS) * nw
        for i, (bi, h) in enumerate(order):
            hs = slice(h * GDN_D, (h + 1) * GDN_D)
            yb_ref[bi, rs, hs] = o[i] * gz_ref[bi, rs, hs]
    for i, (bi, h) in enumerate(order):
        s_ref[bi, h] = S[i]


def _gdn_call(l, qkv, gates, gz, p, states_in, prev_states, *, bb, nc, L):
    B, T, _ = qkv.shape
    tt = nc * L
    tile = functools.partial(_tile_spec, bb, tt)
    return _stacked_call(
        _gdn_kernel, l=l, B=B, bb=bb, nc=nc, L=L, grid=(B // bb, T // tt),
        inputs=[qkv, gates, gz, p["gdn_norm_w"]],
        in_specs=[tile(D_QKV), tile(LANE), tile(D_GDN), _weight_spec(l, (1, GDN_D))],
        states_in=states_in, prev_states=prev_states, state_tails=[(GDN_HEADS, GDN_D, GDN_D)],
        tile_shapes=[jax.ShapeDtypeStruct((B, T, D_GDN), f32)], tile_specs=[tile(D_GDN)],
        scratch_shapes=[], name="gdn")


def _mlstm_chunks(m4_ref, gates_ref, nw_ref, yc_ref, cn_ref, mp_ref, *, bb, nc, L):
    DH = ML_DH
    NP = ML_HEADS // 2
    nbc, npair = bb * nc, bb * nc * NP
    lane = lax.broadcasted_iota(jnp.int32, (1, LANE), 1)
    hi_half = lane >= DH
    row = lax.broadcasted_iota(jnp.int32, (L, LANE), 0)
    col = lax.broadcasted_iota(jnp.int32, (L, LANE), 1) % DH
    causal2 = col <= row
    eye2 = (col == row).astype(f32)
    r2 = lax.broadcasted_iota(jnp.int32, (LANE, LANE), 0) // DH
    c2 = lax.broadcasted_iota(jnp.int32, (LANE, LANE), 1) // DH
    blk = r2 == c2
    blk2 = jnp.concatenate([blk, blk], axis=1)
    ones_bd = blk.astype(bf16)

    def tiles(col0):
        return jnp.stack([m4_ref[bi, ci * L:(ci + 1) * L, col0 + pp * LANE:col0 + (pp + 1) * LANE]
                          for bi in range(bb) for ci in range(nc) for pp in range(NP)])

    def pair_cols(x, lane0):
        a = jnp.stack([x[:, :, lane0 + 2 * pp:lane0 + 2 * pp + 1] for pp in range(NP)], axis=1)
        b = jnp.stack([x[:, :, lane0 + 2 * pp + 1:lane0 + 2 * pp + 2] for pp in range(NP)], axis=1)
        return jnp.where(hi_half, b, a).reshape(npair, L, LANE)

    gt, cs = _gate_tiles(gates_ref, bb=bb, nc=nc, L=L)
    a_t = pltpu.roll(gt, F_LANE - I_LANE, axis=2) - cs
    A_t = a_t
    rows3 = lax.broadcasted_iota(jnp.int32, a_t.shape, 1)
    sh = 1
    while sh < L:
        A_t = jnp.maximum(A_t, jnp.where(rows3 >= sh, pltpu.roll(A_t, sh, axis=1), -jnp.inf))
        sh *= 2
    F = pair_cols(cs, F_LANE)
    a_p = pair_cols(a_t, F_LANE)
    A_p = pair_cols(A_t, F_LANE)
    Dm = jnp.where(causal2, F + _rows_from_cols(a_p, eye2), -jnp.inf)
    Q = tiles(0)
    K = tiles(D_ML)
    V = tiles(2 * D_ML)
    OG = tiles(3 * D_ML)
    if L < DH:
        zpad = jnp.zeros((npair, DH - L, LANE), f32)
        Kp, Vp = jnp.concatenate([K, zpad], axis=1), jnp.concatenate([V, zpad], axis=1)
    else:
        Kp, Vp = K, V
    k_bd = jnp.concatenate([jnp.where(hi_half, 0.0, Kp), jnp.where(hi_half, Kp, 0.0)], axis=1).astype(bf16)
    v_bd = jnp.concatenate([jnp.where(hi_half, 0.0, Vp), jnp.where(hi_half, Vp, 0.0)], axis=1).astype(bf16)
    vo = jnp.concatenate([v_bd, jnp.broadcast_to(ones_bd, (npair, LANE, LANE))], axis=-1)
    s_raw = jnp.einsum('pik,pjk->pij', Q.astype(bf16), k_bd, preferred_element_type=f32)
    v_one = jnp.concatenate([V, jnp.ones((npair, L, LANE), f32)], axis=-1).astype(bf16)

    def chunk(x, ci):
        return x.reshape((bb, nc, NP) + x.shape[1:])[:, ci].reshape((bb * NP,) + x.shape[1:])

    nw = jnp.stack([nw_ref[0:1, pp * LANE:(pp + 1) * LANE] for pp in range(NP)])
    nw = jnp.broadcast_to(nw[None], (bb, NP, 1, LANE)).reshape(bb * NP, 1, LANE)
    mean_w = (blk.astype(f32) * (1.0 / DH)).astype(bf16)

    def head_mean(x):
        hi = x.astype(bf16)
        lo = (x - hi.astype(f32)).astype(bf16)
        mw = jnp.broadcast_to(jnp.concatenate([mean_w, mean_w], axis=0), (x.shape[0], 2 * LANE, LANE))
        return jnp.einsum('pik,pkj->pij', jnp.concatenate([hi, lo], axis=-1), mw, preferred_element_type=f32)

    CN = cn_ref[...]
    m = mp_ref[:, 0:1, :]
    for ci in range(nc):
        F_c, A_c, a_c = chunk(F, ci), chunk(A_p, ci), chunk(a_p, ci)
        M = F_c + jnp.maximum(m, A_c)
        wD = jnp.exp(chunk(Dm, ci) - M)
        wI = jnp.exp(F_c + m - M)
        sc = chunk(s_raw, ci) * wD
        numq = (jnp.concatenate([wI, wI], axis=-1) * _bdot(chunk(Q, ci), CN)
                + jnp.einsum('pik,pkj->pij', sc.astype(bf16), chunk(vo, ci), preferred_element_type=f32))
        hh = numq[:, :, 0:LANE] / jnp.maximum(jnp.abs(numq[:, :, LANE:]), jnp.exp(-M))
        m_new = M[:, L - 1:L, :]
        f_last = F_c[:, L - 1:L, :]
        wk = jnp.exp(f_last - F_c + (a_c + F_c) - m_new)
        dc = jnp.exp(f_last + m - m_new)
        kw = chunk(K, ci) * wk
        upd = jnp.einsum('pki,pkj->pij', kw.astype(bf16), chunk(v_one, ci), preferred_element_type=f32)
        CN = jnp.concatenate([dc, dc], axis=-1) * CN + jnp.where(blk2, upd, 0.0)
        m = m_new
        hh = chunk(OG, ci) * hh
        hc = hh - head_mean(hh)
        var = head_mean(hc * hc)
        y = (hc * lax.rsqrt(var + LN_EPS) * nw).reshape(bb, NP, L, LANE)
        for bi in range(bb):
            for pp in range(NP):
                yc_ref[bi, ci * L:(ci + 1) * L, pp * LANE:(pp + 1) * LANE] = y[bi, pp]
    cn_ref[...] = CN
    mp_ref[:, 0:1, :] = m


def _mlstm_state_in(c0_ref, n0_ref, m0_ref, cn_ref, mp_ref, *, bb):
    DH, NP = ML_DH, ML_HEADS // 2
    if c0_ref is None:
        cn_ref[...] = jnp.zeros(cn_ref.shape, f32)
        mp_ref[...] = jnp.zeros(mp_ref.shape, f32)
        return
    hi_half = lax.broadcasted_iota(jnp.int32, (1, LANE), 1) >= DH
    blk = (lax.broadcasted_iota(jnp.int32, (LANE, LANE), 0) // DH
           == lax.broadcasted_iota(jnp.int32, (LANE, LANE), 1) // DH)
    for bi in range(bb):
        for pp in range(NP):
            c = c0_ref[bi, pp]
            cn_ref[bi * NP + pp, :, 0:LANE] = jnp.concatenate(
                [jnp.where(hi_half, 0.0, c), jnp.where(hi_half, c, 0.0)], axis=0)
            n_col = jnp.broadcast_to(n0_ref[bi, pp:pp + 1, :], (LANE, LANE)).T
            cn_ref[bi * NP + pp, :, LANE:] = jnp.where(blk, n_col, 0.0)
            m_a = m0_ref[bi, 0:1, 2 * pp:2 * pp + 1]
            m_b = m0_ref[bi, 0:1, 2 * pp + 1:2 * pp + 2]
            mp_ref[bi * NP + pp, 0:1, :] = jnp.where(hi_half, m_b, m_a)


def _mlstm_state_out(cn_ref, mp_ref, c_ref, n_ref, m_ref, *, bb):
    DH, NP = ML_DH, ML_HEADS // 2
    for bi in range(bb):
        for pp in range(NP):
            for hl in range(2):
                h = 2 * pp + hl
                rs = slice(hl * DH, (hl + 1) * DH)
                c_ref[bi, h] = cn_ref[bi * NP + pp, rs, hl * DH:(hl + 1) * DH]
                n_blk = cn_ref[bi * NP + pp, rs, LANE + hl * DH:LANE + (hl + 1) * DH]
                n_ref[bi, h:h + 1, :] = n_blk.T[0:1, :]
                m_ref[bi, 0:1, h:h + 1] = mp_ref[bi * NP + pp, 0:1, hl * DH:hl * DH + 1]


def _mlstm_kernel(*refs, l, first, bb, nc, L, has_state, n_prev):
    m4_ref, gates_ref, nw_ref = refs[:3]
    refs = refs[3:]
    c0_ref = n0_ref = m0_ref = None
    if has_state:
        c0_ref, n0_ref, m0_ref = refs[:3]
        refs = refs[3:]
    yc_ref, c_ref, n_ref, m_ref, cn_ref, mp_ref = refs[n_prev:]
    c_ref, n_ref, m_ref = _own_layer([c_ref, n_ref, m_ref], l, first)
    t = pl.program_id(1)

    @pl.when(t == 0)
    def _():
        _mlstm_state_in(c0_ref, n0_ref, m0_ref, cn_ref, mp_ref, bb=bb)

    _mlstm_chunks(m4_ref, gates_ref, nw_ref, yc_ref, cn_ref, mp_ref, bb=bb, nc=nc, L=L)

    @pl.when(t == pl.num_programs(1) - 1)
    def _():
        _mlstm_state_out(cn_ref, mp_ref, c_ref, n_ref, m_ref, bb=bb)


def _mlstm_call(l, m4, gates, p, states_in, prev_states, *, bb, nc, L):
    B, T, _ = m4.shape
    tt = nc * L
    tile = functools.partial(_tile_spec, bb, tt)
    npair = bb * (ML_HEADS // 2)
    return _stacked_call(
        _mlstm_kernel, l=l, B=B, bb=bb, nc=nc, L=L, grid=(B // bb, T // tt),
        inputs=[m4, gates, p["ml_norm_w"]],
        in_specs=[tile(4 * D_ML), tile(LANE), _weight_spec(l, (1, D_ML))],
        states_in=states_in, prev_states=prev_states,
        state_tails=[(ML_HEADS, ML_DH, ML_DH), (ML_HEADS, ML_DH), (1, ML_HEADS)],
        state_in_tails=[(ML_HEADS // 2, ML_DH, LANE), (ML_HEADS // 2, LANE), (1, ML_HEADS)],
        tile_shapes=[jax.ShapeDtypeStruct((B, T, D_ML), f32)], tile_specs=[tile(D_ML)],
        scratch_shapes=[pltpu.VMEM((npair, LANE, 2 * LANE), f32), pltpu.VMEM((npair, 8, LANE), f32)],
        name="mlstm")


def _rec_kernel(*refs, l, first, bb, nc, L, has_state, n_prev):
    qkv_ref, gates_ref, gz_ref, gnw_ref, m4_ref, mnw_ref = refs[:6]
    refs = refs[6:]
    s0_ref = c0_ref = n0_ref = m0_ref = None
    if has_state:
        s0_ref, c0_ref, n0_ref, m0_ref = refs[:4]
        refs = refs[4:]
    yb_ref, yc_ref, s_ref, c_ref, n_ref, m_ref, cn_ref, mp_ref = refs[n_prev:]
    s_ref, c_ref, n_ref, m_ref = _own_layer([s_ref, c_ref, n_ref, m_ref], l, first)
    t = pl.program_id(1)

    @pl.when(t == 0)
    def _():
        s_ref[...] = s0_ref[...] if has_state else jnp.zeros(s_ref.shape, f32)
        _mlstm_state_in(c0_ref, n0_ref, m0_ref, cn_ref, mp_ref, bb=bb)

    _gdn_chunks(qkv_ref, gates_ref, gz_ref, gnw_ref, yb_ref, s_ref, bb=bb, nc=nc, L=L)
    _mlstm_chunks(m4_ref, gates_ref, mnw_ref, yc_ref, cn_ref, mp_ref, bb=bb, nc=nc, L=L)

    @pl.when(t == pl.num_programs(1) - 1)
    def _():
        _mlstm_state_out(cn_ref, mp_ref, c_ref, n_ref, m_ref, bb=bb)


def _rec_call(l, qkv, gates, gz, m4, p, states_in, prev_states, *, bb, nc, L):
    B, T, _ = qkv.shape
    tt = nc * L
    tile = functools.partial(_tile_spec, bb, tt)
    npair = bb * (ML_HEADS // 2)
    gdn_tail = (GDN_HEADS, GDN_D, GDN_D)
    return _stacked_call(
        _rec_kernel, l=l, B=B, bb=bb, nc=nc, L=L, grid=(B // bb, T // tt),
        inputs=[qkv, gates, gz, p["gdn_norm_w"], m4, p["ml_norm_w"]],
        in_specs=[tile(D_QKV), tile(LANE), tile(D_GDN), _weight_spec(l, (1, GDN_D)),
                  tile(4 * D_ML), _weight_spec(l, (1, D_ML))],
        states_in=states_in, prev_states=prev_states,
        state_tails=[gdn_tail, (ML_HEADS, ML_DH, ML_DH), (ML_HEADS, ML_DH), (1, ML_HEADS)],
        state_in_tails=[gdn_tail, (ML_HEADS // 2, ML_DH, LANE), (ML_HEADS // 2, LANE), (1, ML_HEADS)],
        tile_shapes=[jax.ShapeDtypeStruct((B, T, D_GDN), f32), jax.ShapeDtypeStruct((B, T, D_ML), f32)],
        tile_specs=[tile(D_GDN), tile(D_ML)],
        scratch_shapes=[pltpu.VMEM((npair, LANE, 2 * LANE), f32), pltpu.VMEM((npair, 8, LANE), f32)],
        name="recurrent")


def _out_ffn_kernel(*refs, l, first, bb, tt, has_state, n_prev):
    (x_ref, ya_ref, yb_ref, yc_ref, wo_ref, ln1_ref, wup_ref, fcw_ref, wdn_ref, ln2_ref) = refs[:10]
    refs = refs[10:]
    if has_state:
        hf_ref = refs[0]
        refs = refs[1:]
    xo_ref, tf_ref, sf_ref = refs[n_prev:]
    (tf_ref,) = _own_layer([tf_ref], l, first)
    t = pl.program_id(1)

    @pl.when(t == 0)
    def _():
        if has_state:
            sf_ref[:, HIST - 2:HIST, :] = hf_ref[...]
        else:
            sf_ref[:, 0:HIST, :] = jnp.zeros((bb, HIST, D_FF), f32)

    def stages(r0, th):
        rows = bb * th
        rs = slice(r0, r0 + th)
        x = x_ref[:, rs, :].reshape(rows, D_MODEL)
        mix = (jnp.dot(ya_ref[:, rs, :].reshape(rows, D_CONV).astype(bf16), wo_ref[0:D_CONV, :],
                       preferred_element_type=f32)
               + jnp.dot(yb_ref[:, rs, :].reshape(rows, D_GDN).astype(bf16),
                         wo_ref[D_CONV:D_CONV + D_GDN, :], preferred_element_type=f32)
               + jnp.dot(yc_ref[:, rs, :].reshape(rows, D_ML).astype(bf16), wo_ref[D_CONV + D_GDN:, :],
                         preferred_element_type=f32))
        yield
        x1 = _layer_norm(ALPHA * x + mix, ln1_ref[0:1, :], ln1_ref[1:2, :])
        x1b = x1.astype(bf16)
        yield
        acc = None
        for lo, hi in FF_CHUNKS:
            gate = jnp.dot(x1b, wup_ref[:, lo:hi], preferred_element_type=f32)
            val = jnp.dot(x1b, wup_ref[:, D_FF + lo:D_FF + hi], preferred_element_type=f32)
            yield
            sf_ref[:, HIST + r0:HIST + r0 + th, lo:hi] = gate.reshape(bb, th, hi - lo)
            conv = _causal_dwconv(sf_ref[:, r0:r0 + HIST + th, lo:hi], fcw_ref[:, lo:hi])
            hmid = _silu(conv).reshape(rows, hi - lo) * val
            yield
            part = jnp.dot(hmid.astype(bf16), wdn_ref[lo:hi, :], preferred_element_type=f32)
            acc = part if acc is None else acc + part
            yield
        x2 = _layer_norm(ALPHA * x1 + acc, ln2_ref[0:1, :], ln2_ref[1:2, :])
        xo_ref[:, rs, :] = x2.reshape(bb, th, D_MODEL)

    nh = 2 if tt % (2 * MXU_DIM) == 0 else 1
    for _ in itertools.zip_longest(*[stages(i * (tt // nh), tt // nh) for i in range(nh)]):
        pass
    tf_ref[...] = sf_ref[:, tt + HIST - 2:tt + HIST, :]
    sf_ref[:, 0:HIST, :] = sf_ref[:, tt:tt + HIST, :]


def _out_ffn_call(l, x, ya, yb, yc, p, states_in, prev_states, *, bb, tt):
    B, T, _ = x.shape
    tile = functools.partial(_tile_spec, bb, tt)
    return _stacked_call(
        _out_ffn_kernel, l=l, B=B, bb=bb, tt=tt, grid=(B // bb, T // tt),
        inputs=[x, ya, yb, yc, p["w_o"], p["ln1"], p["w_up"], p["ffn_conv_w"], p["w_down"], p["ln2"]],
        in_specs=[tile(D_MODEL), tile(D_CONV), tile(D_GDN), tile(D_ML),
                  _weight_spec(l, (D_MODEL, D_MODEL)), _weight_spec(l, (2, D_MODEL)),
                  _weight_spec(l, (D_MODEL, 2 * D_FF)), _weight_spec(l, (3, D_FF)),
                  _weight_spec(l, (D_FF, D_MODEL)), _weight_spec(l, (2, D_MODEL))],
        states_in=states_in, prev_states=prev_states, state_tails=[(2, D_FF)],
        tile_shapes=[jax.ShapeDtypeStruct((B, T, D_MODEL), f32)], tile_specs=[tile(D_MODEL)],
        scratch_shapes=[pltpu.VMEM((bb, tt + HIST, D_FF), f32)], name="out_ffn")


def _prep_params(w_in, conv_w, gdn_conv_w, gdn_a_log, gdn_dt_bias, gdn_norm_w,
                 ml_i_bias, ml_f_bias, ml_norm_w, w_o, ln1_g, ln1_b,
                 w_up, ffn_conv_w, w_down, ln2_g, ln2_b):
    gate_cols = jnp.concatenate([w_in[:, :, COL_GA:COL_MQ], w_in[:, :, COL_MI:D_IN]], axis=2)
    gate_cols = jnp.pad(gate_cols, ((0, 0), (0, 0), (0, LANE - gate_cols.shape[2])))
    w_tail = jnp.concatenate([w_in[:, :, COL_MQ:COL_MI], gate_cols], axis=2).astype(bf16)
    zeros4 = jnp.zeros((DEPTH, 4), f32)
    bias_row = jnp.pad(jnp.concatenate([gdn_dt_bias, zeros4, ml_i_bias, ml_f_bias], axis=1),
                       ((0, 0), (0, LANE - 16)))
    alog_row = jnp.pad(gdn_a_log, ((0, 0), (0, LANE - GDN_HEADS)))
    gate_p = jnp.zeros((DEPTH, 8, LANE), f32).at[:, 0].set(bias_row).at[:, 1].set(alog_row)
    return dict(
        w_in=w_in.astype(bf16), w_tail=w_tail, conv_w=conv_w, gdn_conv_w=gdn_conv_w, gate_p=gate_p,
        gdn_norm_w=gdn_norm_w.reshape(DEPTH, 1, GDN_D), ml_norm_w=ml_norm_w.reshape(DEPTH, 1, D_ML),
        w_o=w_o.astype(bf16), ln1=jnp.stack([ln1_g, ln1_b], axis=1),
        w_up=w_up.astype(bf16), ffn_conv_w=ffn_conv_w, w_down=w_down.astype(bf16),
        ln2=jnp.stack([ln2_g, ln2_b], axis=1),
    )


def _tiling(B, T):
    L = min(CHUNK, T)
    if T >= 512:
        return dict(bb=1, tt=256, fbb=1, ftt=512, L=L, rbb=min(B, 8), gnc=2, mnc=2, fused_rec=False)
    return dict(bb=256 // T, tt=T, fbb=256 // T, ftt=T, L=L, rbb=min(B, 16), gnc=T // L, mnc=T // L,
                fused_rec=True)


def _trunk(x, states, p):
    B, T, _ = x.shape
    c = _tiling(B, T)
    conv_new = gdn_new = ml_new = ffn_new = None
    for l in range(DEPTH):
        sel = lambda idx: None if states is None else [states[i] for i in idx]
        ya, qkv, gz, m4, gates, *conv_new = _proj_call(l, x, p, sel((0, 1)), conv_new,
                                                       bb=c["bb"], tt=c["tt"])
        if c["fused_rec"]:
            prev = None if gdn_new is None else gdn_new + ml_new
            yb, yc, *rec_new = _rec_call(l, qkv, gates, gz, m4, p, sel((2, 3, 4, 5)), prev,
                                         bb=c["rbb"], nc=c["gnc"], L=c["L"])
            gdn_new, ml_new = rec_new[:1], rec_new[1:]
        else:
            yb, *gdn_new = _gdn_call(l, qkv, gates, gz, p, sel((2,)), gdn_new,
                                     bb=c["rbb"], nc=c["gnc"], L=c["L"])
            yc, *ml_new = _mlstm_call(l, m4, gates, p, sel((3, 4, 5)), ml_new,
                                      bb=c["rbb"], nc=c["mnc"], L=c["L"])
        x, *ffn_new = _out_ffn_call(l, x, ya, yb, yc, p, sel((6,)), ffn_new,
                                    bb=c["fbb"], tt=c["ftt"])
    return x, (conv_new[0], conv_new[1], gdn_new[0], ml_new[0], ml_new[1],
               ml_new[2].reshape(DEPTH, B, ML_HEADS), ffn_new[0])


def _pair_heads(c):
    d, b = c.shape[:2]
    return (c.reshape(d, b, ML_HEADS // 2, 2, ML_DH, ML_DH).transpose(0, 1, 2, 4, 3, 5)
            .reshape(d, b, ML_HEADS // 2, ML_DH, 2 * ML_DH))


def kernel(x_prompt, x_sample, state_conv_mix, state_gdn_conv, state_gdn, state_mlstm_c,
           state_mlstm_n, state_mlstm_m, state_ffn_conv,
           w_in, conv_w, gdn_conv_w, gdn_a_log, gdn_dt_bias, gdn_norm_w,
           ml_i_bias, ml_f_bias, ml_norm_w, w_o, ln1_g, ln1_b,
           w_up, ffn_conv_w, w_down, ln2_g, ln2_b):
    p = _prep_params(w_in, conv_w, gdn_conv_w, gdn_a_log, gdn_dt_bias, gdn_norm_w,
                     ml_i_bias, ml_f_bias, ml_norm_w, w_o, ln1_g, ln1_b,
                     w_up, ffn_conv_w, w_down, ln2_g, ln2_b)
    Bs = x_sample.shape[0]
    y_prompt, p_new = _trunk(x_prompt, None, p)
    s_states = (state_conv_mix, state_gdn_conv, state_gdn, _pair_heads(state_mlstm_c),
                state_mlstm_n.reshape(DEPTH, Bs, ML_HEADS // 2, 2 * ML_DH),
                state_mlstm_m.reshape(DEPTH, Bs, 1, ML_HEADS), state_ffn_conv)
    y_sample, s_new = _trunk(x_sample, s_states, p)
    return (y_prompt, y_sample, *p_new, *s_new)
```

```python
import functools
import itertools

import jax
import jax.numpy as jnp
from jax import lax
from jax.experimental import pallas as pl
from jax.experimental.pallas import tpu as pltpu

f32 = jnp.float32
bf16 = jnp.bfloat16

D_MODEL = 1024
DEPTH = 2
D_CONV = 256
GDN_HEADS = 4
GDN_D = 128
D_GDN = GDN_HEADS * GDN_D
D_QKV = 3 * D_GDN
ML_HEADS = 4
ML_DH = 64
D_ML = ML_HEADS * ML_DH
D_FF = 2816
CHUNK = 64
ALPHA = (2.0 * DEPTH) ** 0.25
LN_EPS = 1e-5
NORM_EPS = 1e-6

LANE = 128
HIST = 8
COL_A = 0
COL_G = 3 * D_CONV
COL_GA = COL_G + 4 * D_GDN
COL_MQ = COL_GA + 2 * GDN_HEADS
COL_MI = COL_MQ + 4 * D_ML
D_IN = COL_MI + 2 * ML_HEADS
D_TAIL = 4 * D_ML + LANE
G_LANE, B_LANE, I_LANE, F_LANE = 0, 4, 8, 12
MXU_DIM = 256
FF_CHUNKS = ((0, 6 * MXU_DIM), (6 * MXU_DIM, D_FF))
VMEM_LIMIT = 56 * 1024 * 1024
_PARAMS = pltpu.CompilerParams(dimension_semantics=("arbitrary", "arbitrary"),
                               vmem_limit_bytes=VMEM_LIMIT)


def _softplus(z):
    return jnp.maximum(z, 0.0) + jnp.log1p(jnp.exp(-jnp.abs(z)))


def _silu(x):
    h = 0.5 * x
    return h + h * jnp.tanh(h)


def _causal_dwconv(xh, w):
    width = w.shape[0]
    y = w[width - 1:width, :] * xh[:, HIST:, :]
    for j in range(width - 1):
        y = y + w[j:j + 1, :] * pltpu.roll(xh, width - 1 - j, axis=1)[:, HIST:, :]
    return y


def _layer_norm(x, g, b):
    mu = jnp.mean(x, -1, keepdims=True)
    xc = x - mu
    var = jnp.mean(xc * xc, -1, keepdims=True)
    return xc * lax.rsqrt(var + LN_EPS) * g + b


def _tile_spec(bb, tt, c):
    return pl.BlockSpec((bb, tt, c), lambda b, t: (b, t, 0))


def _state_spec(l, bb, tail, every_layer=False):
    if every_layer:
        return pl.BlockSpec((DEPTH, bb) + tail, lambda b, t: (0, b) + (0,) * len(tail))
    return pl.BlockSpec((None, bb) + tail, lambda b, t: (l, b) + (0,) * len(tail))


def _own_layer(refs, l, first):
    if not first:
        return refs

    @pl.when(pl.program_id(1) == 0)
    def _():
        for r in refs:
            for j in range(DEPTH):
                if j != l:
                    r[j] = jnp.zeros(r.shape[1:], f32)

    return [r.at[l] for r in refs]


def _weight_spec(l, tail):
    return pl.BlockSpec((None,) + tail, lambda b, t: (l,) + (0,) * len(tail),
                        pipeline_mode=pl.Buffered(1))


def _stacked_call(kernel, *, l, B, bb, grid, inputs, in_specs, states_in, prev_states, state_tails,
                  tile_shapes, tile_specs, scratch_shapes, name, state_in_tails=None, **static):
    first = prev_states is None
    args = list(inputs)
    specs = list(in_specs)
    if states_in is not None:
        args += list(states_in)
        specs += [_state_spec(l, bb, tail) for tail in (state_in_tails or state_tails)]
    aliases = {}
    if not first:
        for i, p in enumerate(prev_states):
            aliases[len(args)] = len(tile_shapes) + i
            args.append(p)
            specs.append(pl.BlockSpec(memory_space=pl.ANY))
    return pl.pallas_call(
        functools.partial(kernel, l=l, first=first, has_state=states_in is not None,
                          n_prev=0 if first else len(prev_states), bb=bb, **static),
        grid=grid, in_specs=specs,
        out_specs=tuple(tile_specs) + tuple(_state_spec(l, bb, tail, every_layer=first)
                                            for tail in state_tails),
        out_shape=tuple(tile_shapes) + tuple(jax.ShapeDtypeStruct((DEPTH, B) + tail, f32)
                                             for tail in state_tails),
        scratch_shapes=scratch_shapes, input_output_aliases=aliases,
        compiler_params=_PARAMS, name=name)(*args)


def _proj_kernel(*refs, l, first, bb, tt, has_state, n_prev):
    x_ref, w_ref, wt_ref, cw_ref, gcw_ref, gp_ref = refs[:6]
    refs = refs[6:]
    if has_state:
        ha_ref, hg_ref = refs[:2]
        refs = refs[2:]
    ya_ref, qkv_ref, gz_ref, m4_ref, gates_ref, ta_ref, tg_ref, sa_ref, sg_ref = refs[n_prev:]
    ta_ref, tg_ref = _own_layer([ta_ref, tg_ref], l, first)
    t = pl.program_id(1)

    @pl.when(t == 0)
    def _():
        if has_state:
            sa_ref[:, HIST - 2:HIST, :] = ha_ref[...]
            sg_ref[:, HIST - 3:HIST, :] = hg_ref[...]
        else:
            sa_ref[:, 0:HIST, :] = jnp.zeros((bb, HIST, D_CONV), f32)
            sg_ref[:, 0:HIST, :] = jnp.zeros((bb, HIST, D_QKV), f32)

    rows = bb * tt
    xb = x_ref[...].reshape(rows, D_MODEL).astype(bf16)

    def post_a(pa):
        a_b = pa[:, 0:D_CONV]
        sa_ref[:, HIST:, :] = (pa[:, D_CONV:2 * D_CONV] * pa[:, 2 * D_CONV:3 * D_CONV]).reshape(bb, tt, D_CONV)
        conv = _causal_dwconv(sa_ref[...], cw_ref[...])
        ya_ref[...] = a_b.reshape(bb, tt, D_CONV) * conv
        ta_ref[...] = sa_ref[:, tt + HIST - 2:tt + HIST, :]
        sa_ref[:, 0:HIST, :] = sa_ref[:, tt:tt + HIST, :]

    def post_qkv(c0, pg):
        w = pg.shape[1]
        sg_ref[:, HIST:, c0:c0 + w] = pg.reshape(bb, tt, w)
        for cb in range(c0 // LANE, (c0 + w) // LANE):
            cs = slice(cb * LANE, (cb + 1) * LANE)
            c = _silu(_causal_dwconv(sg_ref[:, :, cs], gcw_ref[:, cs]))
            if cb < 2 * GDN_HEADS:
                c = c * lax.rsqrt(jnp.sum(c * c, -1, keepdims=True) + NORM_EPS)
                if cb < GDN_HEADS:
                    c = c * (GDN_D ** -0.5)
            qkv_ref[:, :, cs] = c
        tg_ref[:, :, c0:c0 + w] = sg_ref[:, tt + HIST - 3:tt + HIST, c0:c0 + w]
        sg_ref[:, 0:HIST, c0:c0 + w] = sg_ref[:, tt:tt + HIST, c0:c0 + w]

    def post_gz(c0, pz):
        gz_ref[:, :, c0:c0 + pz.shape[1]] = _silu(pz).reshape(bb, tt, pz.shape[1])

    def post_m(part, pm):
        if part == 1:
            pm = pm * (ML_DH ** -0.5)
        elif part == 3:
            pm = jax.nn.sigmoid(pm)
        m4_ref[:, :, part * D_ML:(part + 1) * D_ML] = pm.reshape(bb, tt, D_ML)

    def post_gates(z):
        z = z + gp_ref[0:1, :]
        lane = lax.broadcasted_iota(jnp.int32, z.shape, 1)
        g = -jnp.exp(gp_ref[1:2, :]) * _softplus(z)
        gates = jnp.where(lane < B_LANE, g,
                          jnp.where(lane < I_LANE, jax.nn.sigmoid(z),
                                    jnp.where(lane < F_LANE, z,
                                              jnp.where(lane < F_LANE + ML_HEADS, -_softplus(-z), 0.0))))
        gates_ref[...] = gates.reshape(bb, tt, LANE)

    W = MXU_DIM
    qkv = [(w_ref, COL_G + c, COL_G + c + W, functools.partial(post_qkv, c)) for c in range(0, D_QKV, W)]
    gz = [(w_ref, COL_G + D_QKV + c, COL_G + D_QKV + c + W, functools.partial(post_gz, c))
          for c in range(0, D_GDN, W)]
    mm = [(wt_ref, i * D_ML, (i + 1) * D_ML, functools.partial(post_m, i)) for i in range(4)]
    light = [(w_ref, COL_A, COL_G, post_a)] + mm + gz + [(wt_ref, 4 * D_ML, 4 * D_ML + LANE, post_gates)]
    tasks = []
    for i in range(max(len(qkv), len(light))):
        tasks += qkv[i:i + 1] + light[i:i + 1]
    pending = None
    for ref, lo, hi, post in tasks:
        res = jnp.dot(xb, ref[:, lo:hi], preferred_element_type=f32)
        if pending is not None:
            pending[0](pending[1])
        pending = (post, res)
    pending[0](pending[1])


def _proj_call(l, x, p, states_in, prev_states, *, bb, tt):
    B, T, _ = x.shape
    tile = functools.partial(_tile_spec, bb, tt)
    widths = (D_CONV, D_QKV, D_GDN, 4 * D_ML, LANE)
    return _stacked_call(
        _proj_kernel, l=l, B=B, bb=bb, tt=tt, grid=(B // bb, T // tt),
        inputs=[x, p["w_in"], p["w_tail"], p["conv_w"], p["gdn_conv_w"], p["gate_p"]],
        in_specs=[tile(D_MODEL), _weight_spec(l, (D_MODEL, D_IN)), _weight_spec(l, (D_MODEL, D_TAIL)),
                  _weight_spec(l, (3, D_CONV)), _weight_spec(l, (4, D_QKV)), _weight_spec(l, (8, LANE))],
        states_in=states_in, prev_states=prev_states, state_tails=[(2, D_CONV), (3, D_QKV)],
        tile_shapes=[jax.ShapeDtypeStruct((B, T, w), f32) for w in widths],
        tile_specs=[tile(w) for w in widths],
        scratch_shapes=[pltpu.VMEM((bb, tt + HIST, D_CONV), f32),
                        pltpu.VMEM((bb, tt + HIST, D_QKV), f32)],
        name="in_proj")


def _bdot(a, b):
    return jnp.einsum('pik,pkj->pij', a.astype(bf16), b.astype(bf16), preferred_element_type=f32)


def _bdot_nt(a, b):
    return jnp.einsum('pik,pjk->pij', a.astype(bf16), b.astype(bf16), preferred_element_type=f32)


def _bdot_tn(a, b):
    return jnp.einsum('pki,pkj->pij', a.astype(bf16), b.astype(bf16), preferred_element_type=f32)


def _split3(x):
    hi = x.astype(bf16)
    r1 = x - hi.astype(f32)
    mid = r1.astype(bf16)
    lo = (r1 - mid.astype(f32)).astype(bf16)
    return hi, mid, lo


def _gate_tiles(gates_ref, *, bb, nc, L):
    gt = jnp.stack([gates_ref[bi, ci * L:(ci + 1) * L, :] for bi in range(bb) for ci in range(nc)])
    tril = lax.broadcasted_iota(jnp.int32, (L, L), 1) <= lax.broadcasted_iota(jnp.int32, (L, L), 0)
    trb = jnp.broadcast_to(jnp.concatenate([tril.astype(bf16)] * 3, axis=1), (bb * nc, L, 3 * L))
    cs = jnp.einsum('pij,pjk->pik', trb, jnp.concatenate(_split3(gt), axis=1), preferred_element_type=f32)
    return gt, cs


def _rows_from_cols(x, eye2):
    L = x.shape[1]
    onesb = jnp.ones((x.shape[0], L, 3 * L), bf16)
    return jnp.einsum('pij,pjk->pik', onesb, jnp.concatenate(_split3(x * eye2), axis=1),
                      preferred_element_type=f32)


def _gdn_kernel(*refs, l, first, bb, nc, L, has_state, n_prev):
    qkv_ref, gates_ref, gz_ref, nw_ref = refs[:4]
    refs = refs[4:]
    if has_state:
        s0_ref = refs[0]
        refs = refs[1:]
    yb_ref, s_ref = refs[n_prev:]
    (s_ref,) = _own_layer([s_ref], l, first)
    t = pl.program_id(1)

    @pl.when(t == 0)
    def _():
        s_ref[...] = s0_ref[...] if has_state else jnp.zeros(s_ref.shape, f32)

    _gdn_chunks(qkv_ref, gates_ref, gz_ref, nw_ref, yb_ref, s_ref, bb=bb, nc=nc, L=L)


def _gdn_chunks(qkv_ref, gates_ref, gz_ref, nw_ref, yb_ref, s_ref, *, bb, nc, L):
    NP = GDN_HEADS // 2
    W2 = 2 * L
    nbc, npair = bb * nc, bb * nc * NP
    hi2 = lax.broadcasted_iota(jnp.int32, (1, W2), 1) >= L
    row = lax.broadcasted_iota(jnp.int32, (L, W2), 0)
    col = lax.broadcasted_iota(jnp.int32, (L, W2), 1) % L
    causal2, strict2 = col <= row, col < row
    eye2 = (col == row).astype(f32)
    def head_tiles(col0, hl):
        return jnp.stack([qkv_ref[bi, ci * L:(ci + 1) * L,
                                  col0 + (2 * pp + hl) * GDN_D:col0 + (2 * pp + hl + 1) * GDN_D]
                          for bi in range(bb) for ci in range(nc) for pp in range(NP)])

    def pair_col(x, lane0, hl):
        return jnp.stack([x[:, :, lane0 + 2 * pp + hl:lane0 + 2 * pp + hl + 1] for pp in range(NP)],
                         axis=1).reshape(npair, L, 1)

    gt, cs = _gate_tiles(gates_ref, bb=bb, nc=nc, L=L)
    G = [pair_col(cs, G_LANE, hl) for hl in range(2)]
    beta = [pair_col(gt, B_LANE, hl) for hl in range(2)]
    g_col = jnp.where(hi2, G[1], G[0])
    g_row = _rows_from_cols(g_col, eye2)
    decay = jnp.where(causal2, jnp.exp(jnp.where(causal2, g_col - g_row, 0.0)), 0.0)
    q = [head_tiles(0, hl) for hl in range(2)]
    k = [head_tiles(D_GDN, hl) for hl in range(2)]
    v = [head_tiles(2 * D_GDN, hl) for hl in range(2)]
    kb = [k[hl] * beta[hl] for hl in range(2)]
    zd = jnp.zeros((npair, L, GDN_D), f32)
    lhs = jnp.concatenate([jnp.concatenate(kb, axis=-1), jnp.concatenate(q, axis=-1)], axis=1)
    k_bd = jnp.concatenate([jnp.concatenate([k[0], zd], axis=-1),
                            jnp.concatenate([zd, k[1]], axis=-1)], axis=1)
    kkqk = _bdot_nt(lhs, k_bd)
    kk = kkqk[:, 0:L] * decay
    qk = kkqk[:, L:] * decay
    def block_diag(x):
        return jnp.concatenate([jnp.where(hi2, 0.0, x), jnp.where(hi2, x, 0.0)], axis=1)

    N = jnp.where(strict2, kk, 0.0)
    R = None
    b = 1
    while b < L:
        lower_left = (row // (2 * b) == col // (2 * b)) & (row % (2 * b) >= b) & (col % (2 * b) < b)
        c = jnp.where(lower_left, N, 0.0)
        if R is None:
            R = -c
        else:
            y = c + _bdot(c, block_diag(R))
            R = R - y - _bdot(R, block_diag(y))
        b *= 2
    eG = [jnp.exp(G[hl]) for hl in range(2)]
    rhs = [jnp.concatenate([v[hl] * beta[hl], kb[hl] * eG[hl]], axis=-1) for hl in range(2)]
    z2 = jnp.zeros((npair, L, 2 * GDN_D), f32)
    w = [rhs[0] + _bdot(R, jnp.concatenate([rhs[0], z2], axis=1)),
         rhs[1] + _bdot(R, jnp.concatenate([z2, rhs[1]], axis=1))]
    qg = [q[hl] * eG[hl] for hl in range(2)]
    g_last = [G[hl][:, L - 1:L, :] for hl in range(2)]
    kdec = [k[hl] * jnp.exp(g_last[hl] - G[hl]) for hl in range(2)]
    e_last = [jnp.exp(g_last[hl]) for hl in range(2)]

    def chunk(xs, ci):
        return jnp.concatenate([x.reshape((bb, nc, NP) + x.shape[1:])[:, ci]
                                .reshape((bb * NP,) + x.shape[1:]) for x in xs], axis=0)

    order = [(bi, 2 * pp + hl) for hl in range(2) for bi in range(bb) for pp in range(NP)]
    nw = nw_ref[...]
    S = jnp.stack([s_ref[bi, h] for bi, h in order])
    zu = jnp.zeros((bb * NP, L, GDN_D), f32)
    for ci in range(nc):
        rs = slice(ci * L, (ci + 1) * L)
        w_c = chunk(w, ci)
        xs = _bdot(jnp.concatenate([w_c[:, :, GDN_D:], chunk(qg, ci)], axis=1), S)
        u = w_c[:, :, 0:GDN_D] - xs[:, 0:L]
        u_bd = jnp.concatenate([jnp.concatenate([u[0:bb * NP], zu], axis=1),
                                jnp.concatenate([zu, u[bb * NP:]], axis=1)], axis=0)
        o = xs[:, L:] + _bdot(chunk([qk, qk], ci), u_bd)
        S = S * chunk(e_last, ci) + _bdot_tn(chunk(kdec, ci), u)
        o = o * lax.rsqrt(jnp.mean(o * o, -1, keepdims=True) + NORM_EPS) * nw
        for i, (bi, h) in enumerate(order):
            hs = slice(h * GDN_D, (h + 1) * GDN_D)
            yb_ref[bi, rs, hs] = o[i] * gz_ref[bi, rs, hs]
    for i, (bi, h) in enumerate(order):
        s_ref[bi, h] = S[i]


def _gdn_call(l, qkv, gates, gz, p, states_in, prev_states, *, bb, nc, L):
    B, T, _ = qkv.shape
    tt = nc * L
    tile = functools.partial(_tile_spec, bb, tt)
    return _stacked_call(
        _gdn_kernel, l=l, B=B, bb=bb, nc=nc, L=L, grid=(B // bb, T // tt),
        inputs=[qkv, gates, gz, p["gdn_norm_w"]],
        in_specs=[tile(D_QKV), tile(LANE), tile(D_GDN), _weight_spec(l, (1, GDN_D))],
        states_in=states_in, prev_states=prev_states, state_tails=[(GDN_HEADS, GDN_D, GDN_D)],
        tile_shapes=[jax.ShapeDtypeStruct((B, T, D_GDN), f32)], tile_specs=[tile(D_GDN)],
        scratch_shapes=[], name="gdn")


def _mlstm_chunks(m4_ref, gates_ref, nw_ref, yc_ref, cn_ref, mp_ref, *, bb, nc, L):
    DH = ML_DH
    NP = ML_HEADS // 2
    nbc, npair = bb * nc, bb * nc * NP
    lane = lax.broadcasted_iota(jnp.int32, (1, LANE), 1)
    hi_half = lane >= DH
    row = lax.broadcasted_iota(jnp.int32, (L, LANE), 0)
    col = lax.broadcasted_iota(jnp.int32, (L, LANE), 1) % DH
    causal2 = col <= row
    eye2 = (col == row).astype(f32)
    r2 = lax.broadcasted_iota(jnp.int32, (LANE, LANE), 0) // DH
    c2 = lax.broadcasted_iota(jnp.int32, (LANE, LANE), 1) // DH
    blk = r2 == c2
    blk2 = jnp.concatenate([blk, blk], axis=1)
    ones_bd = blk.astype(bf16)

    def tiles(col0):
        return jnp.stack([m4_ref[bi, ci * L:(ci + 1) * L, col0 + pp * LANE:col0 + (pp + 1) * LANE]
                          for bi in range(bb) for ci in range(nc) for pp in range(NP)])

    def pair_cols(x, lane0):
        a = jnp.stack([x[:, :, lane0 + 2 * pp:lane0 + 2 * pp + 1] for pp in range(NP)], axis=1)
        b = jnp.stack([x[:, :, lane0 + 2 * pp + 1:lane0 + 2 * pp + 2] for pp in range(NP)], axis=1)
        return jnp.where(hi_half, b, a).reshape(npair, L, LANE)

    gt, cs = _gate_tiles(gates_ref, bb=bb, nc=nc, L=L)
    a_t = pltpu.roll(gt, F_LANE - I_LANE, axis=2) - cs
    A_t = a_t
    rows3 = lax.broadcasted_iota(jnp.int32, a_t.shape, 1)
    sh = 1
    while sh < L:
        A_t = jnp.maximum(A_t, jnp.where(rows3 >= sh, pltpu.roll(A_t, sh, axis=1), -jnp.inf))
        sh *= 2
    F = pair_cols(cs, F_LANE)
    a_p = pair_cols(a_t, F_LANE)
    A_p = pair_cols(A_t, F_LANE)
    Dm = jnp.where(causal2, F + _rows_from_cols(a_p, eye2), -jnp.inf)
    Q = tiles(0)
    K = tiles(D_ML)
    V = tiles(2 * D_ML)
    OG = tiles(3 * D_ML)
    if L < DH:
        zpad = jnp.zeros((npair, DH - L, LANE), f32)
        Kp, Vp = jnp.concatenate([K, zpad], axis=1), jnp.concatenate([V, zpad], axis=1)
    else:
        Kp, Vp = K, V
    k_bd = jnp.concatenate([jnp.where(hi_half, 0.0, Kp), jnp.where(hi_half, Kp, 0.0)], axis=1).astype(bf16)
    v_bd = jnp.concatenate([jnp.where(hi_half, 0.0, Vp), jnp.where(hi_half, Vp, 0.0)], axis=1).astype(bf16)
    vo = jnp.concatenate([v_bd, jnp.broadcast_to(ones_bd, (npair, LANE, LANE))], axis=-1)
    s_raw = jnp.einsum('pik,pjk->pij', Q.astype(bf16), k_bd, preferred_element_type=f32)
    v_one = jnp.concatenate([V, jnp.ones((npair, L, LANE), f32)], axis=-1).astype(bf16)

    def chunk(x, ci):
        return x.reshape((bb, nc, NP) + x.shape[1:])[:, ci].reshape((bb * NP,) + x.shape[1:])

    nw = jnp.stack([nw_ref[0:1, pp * LANE:(pp + 1) * LANE] for pp in range(NP)])
    nw = jnp.broadcast_to(nw[None], (bb, NP, 1, LANE)).reshape(bb * NP, 1, LANE)
    mean_w = (blk.astype(f32) * (1.0 / DH)).astype(bf16)

    def head_mean(x):
        hi = x.astype(bf16)
        lo = (x - hi.astype(f32)).astype(bf16)
        mw = jnp.broadcast_to(jnp.concatenate([mean_w, mean_w], axis=0), (x.shape[0], 2 * LANE, LANE))
        return jnp.einsum('pik,pkj->pij', jnp.concatenate([hi, lo], axis=-1), mw, preferred_element_type=f32)

    CN = cn_ref[...]
    m = mp_ref[:, 0:1, :]
    for ci in range(nc):
        F_c, A_c, a_c = chunk(F, ci), chunk(A_p, ci), chunk(a_p, ci)
        M = F_c + jnp.maximum(m, A_c)
        wD = jnp.exp(chunk(Dm, ci) - M)
        wI = jnp.exp(F_c + m - M)
        sc = chunk(s_raw, ci) * wD
        numq = (jnp.concatenate([wI, wI], axis=-1) * _bdot(chunk(Q, ci), CN)
                + jnp.einsum('pik,pkj->pij', sc.astype(bf16), chunk(vo, ci), preferred_element_type=f32))
        hh = numq[:, :, 0:LANE] / jnp.maximum(jnp.abs(numq[:, :, LANE:]), jnp.exp(-M))
        m_new = M[:, L - 1:L, :]
        f_last = F_c[:, L - 1:L, :]
        wk = jnp.exp(f_last - F_c + (a_c + F_c) - m_new)
        dc = jnp.exp(f_last + m - m_new)
        kw = chunk(K, ci) * wk
        upd = jnp.einsum('pki,pkj->pij', kw.astype(bf16), chunk(v_one, ci), preferred_element_type=f32)
        CN = jnp.concatenate([dc, dc], axis=-1) * CN + jnp.where(blk2, upd, 0.0)
        m = m_new
        hh = chunk(OG, ci) * hh
        hc = hh - head_mean(hh)
        var = head_mean(hc * hc)
        y = (hc * lax.rsqrt(var + LN_EPS) * nw).reshape(bb, NP, L, LANE)
        for bi in range(bb):
            for pp in range(NP):
                yc_ref[bi, ci * L:(ci + 1) * L, pp * LANE:(pp + 1) * LANE] = y[bi, pp]
    cn_ref[...] = CN
    mp_ref[:, 0:1, :] = m


def _mlstm_state_in(c0_ref, n0_ref, m0_ref, cn_ref, mp_ref, *, bb):
    DH, NP = ML_DH, ML_HEADS // 2
    if c0_ref is None:
        cn_ref[...] = jnp.zeros(cn_ref.shape, f32)
        mp_ref[...] = jnp.zeros(mp_ref.shape, f32)
        return
    hi_half = lax.broadcasted_iota(jnp.int32, (1, LANE), 1) >= DH
    blk = (lax.broadcasted_iota(jnp.int32, (LANE, LANE), 0) // DH
           == lax.broadcasted_iota(jnp.int32, (LANE, LANE), 1) // DH)
    for bi in range(bb):
        for pp in range(NP):
            c = c0_ref[bi, pp]
            cn_ref[bi * NP + pp, :, 0:LANE] = jnp.concatenate(
                [jnp.where(hi_half, 0.0, c), jnp.where(hi_half, c, 0.0)], axis=0)
            n_col = jnp.broadcast_to(n0_ref[bi, pp:pp + 1, :], (LANE, LANE)).T
            cn_ref[bi * NP + pp, :, LANE:] = jnp.where(blk, n_col, 0.0)
            m_a = m0_ref[bi, 0:1, 2 * pp:2 * pp + 1]
            m_b = m0_ref[bi, 0:1, 2 * pp + 1:2 * pp + 2]
            mp_ref[bi * NP + pp, 0:1, :] = jnp.where(hi_half, m_b, m_a)


def _mlstm_state_out(cn_ref, mp_ref, c_ref, n_ref, m_ref, *, bb):
    DH, NP = ML_DH, ML_HEADS // 2
    for bi in range(bb):
        for pp in range(NP):
            for hl in range(2):
                h = 2 * pp + hl
                rs = slice(hl * DH, (hl + 1) * DH)
                c_ref[bi, h] = cn_ref[bi * NP + pp, rs, hl * DH:(hl + 1) * DH]
                n_blk = cn_ref[bi * NP + pp, rs, LANE + hl * DH:LANE + (hl + 1) * DH]
                n_ref[bi, h:h + 1, :] = n_blk.T[0:1, :]
                m_ref[bi, 0:1, h:h + 1] = mp_ref[bi * NP + pp, 0:1, hl * DH:hl * DH + 1]


def _mlstm_kernel(*refs, l, first, bb, nc, L, has_state, n_prev):
    m4_ref, gates_ref, nw_ref = refs[:3]
    refs = refs[3:]
    c0_ref = n0_ref = m0_ref = None
    if has_state:
        c0_ref, n0_ref, m0_ref = refs[:3]
        refs = refs[3:]
    yc_ref, c_ref, n_ref, m_ref, cn_ref, mp_ref = refs[n_prev:]
    c_ref, n_ref, m_ref = _own_layer([c_ref, n_ref, m_ref], l, first)
    t = pl.program_id(1)

    @pl.when(t == 0)
    def _():
        _mlstm_state_in(c0_ref, n0_ref, m0_ref, cn_ref, mp_ref, bb=bb)

    _mlstm_chunks(m4_ref, gates_ref, nw_ref, yc_ref, cn_ref, mp_ref, bb=bb, nc=nc, L=L)

    @pl.when(t == pl.num_programs(1) - 1)
    def _():
        _mlstm_state_out(cn_ref, mp_ref, c_ref, n_ref, m_ref, bb=bb)


def _mlstm_call(l, m4, gates, p, states_in, prev_states, *, bb, nc, L):
    B, T, _ = m4.shape
    tt = nc * L
    tile = functools.partial(_tile_spec, bb, tt)
    npair = bb * (ML_HEADS // 2)
    return _stacked_call(
        _mlstm_kernel, l=l, B=B, bb=bb, nc=nc, L=L, grid=(B // bb, T // tt),
        inputs=[m4, gates, p["ml_norm_w"]],
        in_specs=[tile(4 * D_ML), tile(LANE), _weight_spec(l, (1, D_ML))],
        states_in=states_in, prev_states=prev_states,
        state_tails=[(ML_HEADS, ML_DH, ML_DH), (ML_HEADS, ML_DH), (1, ML_HEADS)],
        state_in_tails=[(ML_HEADS // 2, ML_DH, LANE), (ML_HEADS // 2, LANE), (1, ML_HEADS)],
        tile_shapes=[jax.ShapeDtypeStruct((B, T, D_ML), f32)], tile_specs=[tile(D_ML)],
        scratch_shapes=[pltpu.VMEM((npair, LANE, 2 * LANE), f32), pltpu.VMEM((npair, 8, LANE), f32)],
        name="mlstm")


def _rec_kernel(*refs, l, first, bb, nc, L, has_state, n_prev):
    qkv_ref, gates_ref, gz_ref, gnw_ref, m4_ref, mnw_ref = refs[:6]
    refs = refs[6:]
    s0_ref = c0_ref = n0_ref = m0_ref = None
    if has_state:
        s0_ref, c0_ref, n0_ref, m0_ref = refs[:4]
        refs = refs[4:]
    yb_ref, yc_ref, s_ref, c_ref, n_ref, m_ref, cn_ref, mp_ref = refs[n_prev:]
    s_ref, c_ref, n_ref, m_ref = _own_layer([s_ref, c_ref, n_ref, m_ref], l, first)
    t = pl.program_id(1)

    @pl.when(t == 0)
    def _():
        s_ref[...] = s0_ref[...] if has_state else jnp.zeros(s_ref.shape, f32)
        _mlstm_state_in(c0_ref, n0_ref, m0_ref, cn_ref, mp_ref, bb=bb)

    _gdn_chunks(qkv_ref, gates_ref, gz_ref, gnw_ref, yb_ref, s_ref, bb=bb, nc=nc, L=L)
    _mlstm_chunks(m4_ref, gates_ref, mnw_ref, yc_ref, cn_ref, mp_ref, bb=bb, nc=nc, L=L)

    @pl.when(t == pl.num_programs(1) - 1)
    def _():
        _mlstm_state_out(cn_ref, mp_ref, c_ref, n_ref, m_ref, bb=bb)


def _rec_call(l, qkv, gates, gz, m4, p, states_in, prev_states, *, bb, nc, L):
    B, T, _ = qkv.shape
    tt = nc * L
    tile = functools.partial(_tile_spec, bb, tt)
    npair = bb * (ML_HEADS // 2)
    gdn_tail = (GDN_HEADS, GDN_D, GDN_D)
    return _stacked_call(
        _rec_kernel, l=l, B=B, bb=bb, nc=nc, L=L, grid=(B // bb, T // tt),
        inputs=[qkv, gates, gz, p["gdn_norm_w"], m4, p["ml_norm_w"]],
        in_specs=[tile(D_QKV), tile(LANE), tile(D_GDN), _weight_spec(l, (1, GDN_D)),
                  tile(4 * D_ML), _weight_spec(l, (1, D_ML))],
        states_in=states_in, prev_states=prev_states,
        state_tails=[gdn_tail, (ML_HEADS, ML_DH, ML_DH), (ML_HEADS, ML_DH), (1, ML_HEADS)],
        state_in_tails=[gdn_tail, (ML_HEADS // 2, ML_DH, LANE), (ML_HEADS // 2, LANE), (1, ML_HEADS)],
        tile_shapes=[jax.ShapeDtypeStruct((B, T, D_GDN), f32), jax.ShapeDtypeStruct((B, T, D_ML), f32)],
        tile_specs=[tile(D_GDN), tile(D_ML)],
        scratch_shapes=[pltpu.VMEM((npair, LANE, 2 * LANE), f32), pltpu.VMEM((npair, 8, LANE), f32)],
        name="recurrent")


def _out_ffn_kernel(*refs, l, first, bb, tt, has_state, n_prev):
    (x_ref, ya_ref, yb_ref, yc_ref, wo_ref, ln1_ref, wup_ref, fcw_ref, wdn_ref, ln2_ref) = refs[:10]
    refs = refs[10:]
    if has_state:
        hf_ref = refs[0]
        refs = refs[1:]
    xo_ref, tf_ref, sf_ref = refs[n_prev:]
    (tf_ref,) = _own_layer([tf_ref], l, first)
    t = pl.program_id(1)

    @pl.when(t == 0)
    def _():
        if has_state:
            sf_ref[:, HIST - 2:HIST, :] = hf_ref[...]
        else:
            sf_ref[:, 0:HIST, :] = jnp.zeros((bb, HIST, D_FF), f32)

    def stages(r0, th):
        rows = bb * th
        rs = slice(r0, r0 + th)
        x = x_ref[:, rs, :].reshape(rows, D_MODEL)
        mix = (jnp.dot(ya_ref[:, rs, :].reshape(rows, D_CONV).astype(bf16), wo_ref[0:D_CONV, :],
                       preferred_element_type=f32)
               + jnp.dot(yb_ref[:, rs, :].reshape(rows, D_GDN).astype(bf16),
                         wo_ref[D_CONV:D_CONV + D_GDN, :], preferred_element_type=f32)
               + jnp.dot(yc_ref[:, rs, :].reshape(rows, D_ML).astype(bf16), wo_ref[D_CONV + D_GDN:, :],
                         preferred_element_type=f32))
        yield
        x1 = _layer_norm(ALPHA * x + mix, ln1_ref[0:1, :], ln1_ref[1:2, :])
        x1b = x1.astype(bf16)
        yield
        acc = None
        for lo, hi in FF_CHUNKS:
            gate = jnp.dot(x1b, wup_ref[:, lo:hi], preferred_element_type=f32)
            val = jnp.dot(x1b, wup_ref[:, D_FF + lo:D_FF + hi], preferred_element_type=f32)
            yield
            sf_ref[:, HIST + r0:HIST + r0 + th, lo:hi] = gate.reshape(bb, th, hi - lo)
            conv = _causal_dwconv(sf_ref[:, r0:r0 + HIST + th, lo:hi], fcw_ref[:, lo:hi])
            hmid = _silu(conv).reshape(rows, hi - lo) * val
            yield
            part = jnp.dot(hmid.astype(bf16), wdn_ref[lo:hi, :], preferred_element_type=f32)
            acc = part if acc is None else acc + part
            yield
        x2 = _layer_norm(ALPHA * x1 + acc, ln2_ref[0:1, :], ln2_ref[1:2, :])
        xo_ref[:, rs, :] = x2.reshape(bb, th, D_MODEL)

    nh = 2 if tt % (2 * MXU_DIM) == 0 else 1
    for _ in itertools.zip_longest(*[stages(i * (tt // nh), tt // nh) for i in range(nh)]):
        pass
    tf_ref[...] = sf_ref[:, tt + HIST - 2:tt + HIST, :]
    sf_ref[:, 0:HIST, :] = sf_ref[:, tt:tt + HIST, :]


def _out_ffn_call(l, x, ya, yb, yc, p, states_in, prev_states, *, bb, tt):
    B, T, _ = x.shape
    tile = functools.partial(_tile_spec, bb, tt)
    return _stacked_call(
        _out_ffn_kernel, l=l, B=B, bb=bb, tt=tt, grid=(B // bb, T // tt),
        inputs=[x, ya, yb, yc, p["w_o"], p["ln1"], p["w_up"], p["ffn_conv_w"], p["w_down"], p["ln2"]],
        in_specs=[tile(D_MODEL), tile(D_CONV), tile(D_GDN), tile(D_ML),
                  _weight_spec(l, (D_MODEL, D_MODEL)), _weight_spec(l, (2, D_MODEL)),
                  _weight_spec(l, (D_MODEL, 2 * D_FF)), _weight_spec(l, (3, D_FF)),
                  _weight_spec(l, (D_FF, D_MODEL)), _weight_spec(l, (2, D_MODEL))],
        states_in=states_in, prev_states=prev_states, state_tails=[(2, D_FF)],
        tile_shapes=[jax.ShapeDtypeStruct((B, T, D_MODEL), f32)], tile_specs=[tile(D_MODEL)],
        scratch_shapes=[pltpu.VMEM((bb, tt + HIST, D_FF), f32)], name="out_ffn")


def _prep_params(w_in, conv_w, gdn_conv_w, gdn_a_log, gdn_dt_bias, gdn_norm_w,
                 ml_i_bias, ml_f_bias, ml_norm_w, w_o, ln1_g, ln1_b,
                 w_up, ffn_conv_w, w_down, ln2_g, ln2_b):
    gate_cols = jnp.concatenate([w_in[:, :, COL_GA:COL_MQ], w_in[:, :, COL_MI:D_IN]], axis=2)
    gate_cols = jnp.pad(gate_cols, ((0, 0), (0, 0), (0, LANE - gate_cols.shape[2])))
    w_tail = jnp.concatenate([w_in[:, :, COL_MQ:COL_MI], gate_cols], axis=2).astype(bf16)
    zeros4 = jnp.zeros((DEPTH, 4), f32)
    bias_row = jnp.pad(jnp.concatenate([gdn_dt_bias, zeros4, ml_i_bias, ml_f_bias], axis=1),
                       ((0, 0), (0, LANE - 16)))
    alog_row = jnp.pad(gdn_a_log, ((0, 0), (0, LANE - GDN_HEADS)))
    gate_p = jnp.zeros((DEPTH, 8, LANE), f32).at[:, 0].set(bias_row).at[:, 1].set(alog_row)
    return dict(
        w_in=w_in.astype(bf16), w_tail=w_tail, conv_w=conv_w, gdn_conv_w=gdn_conv_w, gate_p=gate_p,
        gdn_norm_w=gdn_norm_w.reshape(DEPTH, 1, GDN_D), ml_norm_w=ml_norm_w.reshape(DEPTH, 1, D_ML),
        w_o=w_o.astype(bf16), ln1=jnp.stack([ln1_g, ln1_b], axis=1),
        w_up=w_up.astype(bf16), ffn_conv_w=ffn_conv_w, w_down=w_down.astype(bf16),
        ln2=jnp.stack([ln2_g, ln2_b], axis=1),
    )


def _tiling(B, T):
    L = min(CHUNK, T)
    if T >= 512:
        return dict(bb=1, tt=256, fbb=1, ftt=512, L=L, rbb=min(B, 8), gnc=1, mnc=1, fused_rec=True)
    return dict(bb=256 // T, tt=T, fbb=256 // T, ftt=T, L=L, rbb=min(B, 16), gnc=T // L, mnc=T // L,
                fused_rec=True)


def _trunk(x, states, p):
    B, T, _ = x.shape
    c = _tiling(B, T)
    conv_new = gdn_new = ml_new = ffn_new = None
    for l in range(DEPTH):
        sel = lambda idx: None if states is None else [states[i] for i in idx]
        ya, qkv, gz, m4, gates, *conv_new = _proj_call(l, x, p, sel((0, 1)), conv_new,
                                                       bb=c["bb"], tt=c["tt"])
        if c["fused_rec"]:
            prev = None if gdn_new is None else gdn_new + ml_new
            yb, yc, *rec_new = _rec_call(l, qkv, gates, gz, m4, p, sel((2, 3, 4, 5)), prev,
                                         bb=c["rbb"], nc=c["gnc"], L=c["L"])
            gdn_new, ml_new = rec_new[:1], rec_new[1:]
        else:
            yb, *gdn_new = _gdn_call(l, qkv, gates, gz, p, sel((2,)), gdn_new,
                                     bb=c["rbb"], nc=c["gnc"], L=c["L"])
            yc, *ml_new = _mlstm_call(l, m4, gates, p, sel((3, 4, 5)), ml_new,
                                      bb=c["rbb"], nc=c["mnc"], L=c["L"])
        x, *ffn_new = _out_ffn_call(l, x, ya, yb, yc, p, sel((6,)), ffn_new,
                                    bb=c["fbb"], tt=c["ftt"])
    return x, (conv_new[0], conv_new[1], gdn_new[0], ml_new[0], ml_new[1],
               ml_new[2].reshape(DEPTH, B, ML_HEADS), ffn_new[0])


def _pair_heads(c):
    d, b = c.shape[:2]
    return (c.reshape(d, b, ML_HEADS // 2, 2, ML_DH, ML_DH).transpose(0, 1, 2, 4, 3, 5)
            .reshape(d, b, ML_HEADS // 2, ML_DH, 2 * ML_DH))


def kernel(x_prompt, x_sample, state_conv_mix, state_gdn_conv, state_gdn, state_mlstm_c,
           state_mlstm_n, state_mlstm_m, state_ffn_conv,
           w_in, conv_w, gdn_conv_w, gdn_a_log, gdn_dt_bias, gdn_norm_w,
           ml_i_bias, ml_f_bias, ml_norm_w, w_o, ln1_g, ln1_b,
           w_up, ffn_conv_w, w_down, ln2_g, ln2_b):
    p = _prep_params(w_in, conv_w, gdn_conv_w, gdn_a_log, gdn_dt_bias, gdn_norm_w,
                     ml_i_bias, ml_f_bias, ml_norm_w, w_o, ln1_g, ln1_b,
                     w_up, ffn_conv_w, w_down, ln2_g, ln2_b)
    Bs = x_sample.shape[0]
    y_prompt, p_new = _trunk(x_prompt, None, p)
    s_states = (state_conv_mix, state_gdn_conv, state_gdn, _pair_heads(state_mlstm_c),
                state_mlstm_n.reshape(DEPTH, Bs, ML_HEADS // 2, 2 * ML_DH),
                state_mlstm_m.reshape(DEPTH, Bs, 1, ML_HEADS), state_ffn_conv)
    y_sample, s_new = _trunk(x_sample, s_states, p)
    return (y_prompt, y_sample, *p_new, *s_new)
```
